```python
import jax
import jax.numpy as jnp
from jax import lax
import numpy as np

D_MODEL = 2048
BATCH = 16
SEQ = 2048
DEPTH = 4

CTX_LEN = 256
GRID_W = 64
HEAD_DIM = 128
N_GROUPS = 4
GROUP_WIDTH = D_MODEL // N_GROUPS
D_MIX = N_GROUPS * GROUP_WIDTH
Q_BLOCK = 128
ROPE_THETA = 10000.0
NORM_EPS = 1e-6
NEG_INF = -1e30

MLA_HEADS = GROUP_WIDTH // HEAD_DIM
MLA_NOPE = HEAD_DIM
MLA_ROPE = HEAD_DIM // 2
MLA_V = HEAD_DIM
MLA_Q_RANK = 3 * D_MODEL // 16
MLA_KV_RANK = D_MODEL // 8
GQA_HEADS = GROUP_WIDTH // HEAD_DIM
GQA_KV_HEADS = GQA_HEADS // 2
GDN_HEADS = GROUP_WIDTH // HEAD_DIM
GDN_CONV = 5
GDN_CHUNK = 64
SWA_HEADS = GROUP_WIDTH // HEAD_DIM
SWA_KV_HEADS = SWA_HEADS // 2
SWA_WINDOW = 128
FFN_DIM = 5632
N_EXPERTS = 8
TOP_K = 2
EXPERT_DIM = 2 * D_MODEL

IN_SPLITS = (
    MLA_Q_RANK, MLA_KV_RANK, MLA_ROPE,
    GQA_HEADS * HEAD_DIM, GQA_KV_HEADS * HEAD_DIM, GQA_KV_HEADS * HEAD_DIM,
    GDN_HEADS * HEAD_DIM, GDN_HEADS * HEAD_DIM, GDN_HEADS * HEAD_DIM,
    GDN_HEADS * HEAD_DIM, 2 * GDN_HEADS, 2 * GDN_HEADS,
    SWA_HEADS * HEAD_DIM, SWA_KV_HEADS * HEAD_DIM, SWA_KV_HEADS * HEAD_DIM,
)
D_IN = sum(IN_SPLITS)

kernel_name = 'hybrid_parallel_group_flow_backbone'

F32 = jnp.float32


def _rmsnorm(x, g):
    xf = x.astype(F32)
    y = xf * lax.rsqrt(jnp.mean(xf * xf, axis=-1, keepdims=True) + NORM_EPS)
    return (y * g.astype(F32)).astype(x.dtype)


def _l2norm(x):
    return x * lax.rsqrt(jnp.sum(x * x, axis=-1, keepdims=True) + 1e-6)


def _modulate(h, shift, scale):
    return h * (1.0 + scale) + shift


def _split_in(p):
    offsets = np.cumsum(IN_SPLITS)[:-1].tolist()
    return jnp.split(p, offsets, axis=-1)


def _axial_rope_tables(n_tokens, rot_dim):
    rows = n_tokens // GRID_W
    row = jnp.repeat(jnp.arange(rows), GRID_W).astype(F32)
    col = jnp.tile(jnp.arange(GRID_W), rows).astype(F32)
    half = rot_dim // 2
    inv_freq = ROPE_THETA ** (-jnp.arange(0, half, 2, dtype=F32) / half)
    ang_r = row[:, None] * inv_freq
    ang_c = col[:, None] * inv_freq
    ang = jnp.concatenate([ang_r, ang_r, ang_c, ang_c], axis=-1)
    return jnp.cos(ang), jnp.sin(ang)


def _rotate_half_axial(x):
    q = x.shape[-1] // 4
    x1, x2, x3, x4 = x[..., :q], x[..., q:2 * q], x[..., 2 * q:3 * q], x[..., 3 * q:]
    return jnp.concatenate([-x2, x1, -x4, x3], axis=-1)


def _apply_rope(x, cos, sin):
    shape = (1, x.shape[1]) + (1,) * (x.ndim - 3) + (x.shape[-1],)
    cos = cos.reshape(shape).astype(x.dtype)
    sin = sin.reshape(shape).astype(x.dtype)
    return x * cos + _rotate_half_axial(x) * sin


def _flat(y):
    return None if y is None else y.reshape(y.shape[:2] + (-1,))


def _attend(q, k, v, scale):
    s = jnp.einsum('bqhgd,bkhd->bhgqk', q, k, preferred_element_type=F32) * scale
    p = jax.nn.softmax(s, axis=-1).astype(v.dtype)
    return jnp.einsum('bhgqk,bkhd->bqhgd', p, v)


def _attend_query_blocks(q, k, v, scale):
    B, S = q.shape[:2]
    nb = S // Q_BLOCK
    qb = jnp.moveaxis(q.reshape((B, nb, Q_BLOCK) + q.shape[2:]), 1, 0)
    ob = lax.map(lambda blk: _attend(blk, k, v, scale), qb)
    return jnp.moveaxis(ob, 0, 1).reshape((B, S) + ob.shape[3:])


def _softmax_with_sink(s, sink):
    m = jnp.maximum(jnp.max(s, axis=-1, keepdims=True), sink)
    e = jnp.exp(s - m)
    return e / (jnp.sum(e, axis=-1, keepdims=True) + jnp.exp(sink - m))


def _banded_sink_attention(q, k, v, kc, vc, sink, scale):
    B, S, Hk, G, d = q.shape
    nb = S // Q_BLOCK
    W = 3 * Q_BLOCK

    def windows(t):
        pad = jnp.zeros((B, Q_BLOCK) + t.shape[2:], t.dtype)
        tb = jnp.concatenate([pad, t, pad], axis=1).reshape((B, nb + 2, Q_BLOCK) + t.shape[2:])
        return jnp.concatenate([tb[:, :-2], tb[:, 1:-1], tb[:, 2:]], axis=2)

    kw, vw = windows(k), windows(v)
    qb = q.reshape(B, nb, Q_BLOCK, Hk, G, d)
    s_loc = jnp.einsum('bnqhgd,bnkhd->bhgnqk', qb, kw, preferred_element_type=F32) * scale
    q_pos = jnp.arange(nb)[:, None, None] * Q_BLOCK + jnp.arange(Q_BLOCK)[None, :, None]
    k_pos = jnp.arange(nb)[:, None, None] * Q_BLOCK - Q_BLOCK + jnp.arange(W)[None, None, :]
    valid = (jnp.abs(k_pos - q_pos) <= SWA_WINDOW) & (k_pos >= 0) & (k_pos < S)
    s_loc = jnp.where(valid, s_loc, NEG_INF)
    s_ctx = jnp.einsum('bnqhgd,bkhd->bhgnqk', qb, kc, preferred_element_type=F32) * scale
    p = _softmax_with_sink(jnp.concatenate([s_loc, s_ctx], axis=-1),
                           sink[:, :, None, None, None]).astype(v.dtype)
    o = (jnp.einsum('bhgnqk,bnkhd->bnqhgd', p[..., :W], vw)
         + jnp.einsum('bhgnqk,bkhd->bnqhgd', p[..., W:], vc))
    return o.reshape(B, S, Hk, G, d)


def _centred_depthwise_conv(x, w):
    K = w.shape[0]
    return lax.conv_general_dilated(
        x, w[:, None, :].astype(x.dtype), window_strides=(1,),
        padding=[((K - 1) // 2, (K - 1) // 2)],
        dimension_numbers=('NWC', 'WIO', 'NWC'), feature_group_count=x.shape[-1])


def _gated_delta_chunked(q, k, v, g, beta, state0):
    B, T, H, dk = q.shape
    dv = v.shape[-1]
    C = GDN_CHUNK
    n = T // C

    def chunks(t):
        return jnp.moveaxis(t.reshape((B, n, C, H) + t.shape[3:]), 3, 1)

    q = chunks(q * dk ** -0.5)
    k = chunks(k)
    v = chunks(v)
    g = jnp.cumsum(chunks(g), axis=-1)
    beta = chunks(beta)
    kb = k * beta[..., None]
    tri = jnp.tril(jnp.ones((C, C), bool))
    strict = jnp.tril(jnp.ones((C, C), bool), -1)
    diff = g[..., :, None] - g[..., None, :]
    decay = jnp.where(tri, jnp.exp(jnp.where(tri, diff, 0.0)), 0.0)
    lower = jnp.where(strict, jnp.einsum('bhncd,bhnjd->bhncj', kb, k) * decay, 0.0)
    a = lower + jnp.eye(C, dtype=lower.dtype)
    rhs = jnp.concatenate([v * beta[..., None], kb * jnp.exp(g)[..., None]], axis=-1)
    sol = lax.linalg.triangular_solve(a, rhs, left_side=True, lower=True, unit_diagonal=True)
    u, w = sol[..., :dv], sol[..., dv:]
    intra = jnp.einsum('bhncd,bhnjd->bhncj', q, k) * decay
    xs = tuple(jnp.moveaxis(t, 2, 0) for t in (q, k, u, w, g, intra))

    def step(state, inp):
        qi, ki, ui, wi, gi, ai = inp
        v_new = ui - jnp.einsum('bhcd,bhde->bhce', wi, state)
        o = (jnp.einsum('bhcd,bhde->bhce', qi * jnp.exp(gi)[..., None], state)
             + jnp.einsum('bhcj,bhje->bhce', ai, v_new))
        g_last = gi[..., -1:]
        state = (state * jnp.exp(g_last)[..., None]
                 + jnp.einsum('bhcd,bhce->bhde', ki * jnp.exp(g_last - gi)[..., None], v_new))
        return state, o

    state, o = lax.scan(step, state0, xs)
    o = jnp.moveaxis(jnp.moveaxis(o, 0, 2), 1, 3).reshape(B, T, H, dv)
    return o, state


def _mla(pc, pl, rope, q_norm_g, kv_norm_g, w_uq, w_ukv, with_ctx):
    cos, sin = rope

    def queries(cq, rotate):
        B, T = cq.shape[:2]
        q = (_rmsnorm(cq, q_norm_g) @ w_uq).reshape(B, T, MLA_HEADS, MLA_NOPE + MLA_ROPE)
        q_nope, q_rope = q[..., :MLA_NOPE], q[..., MLA_NOPE:]
        if rotate:
            q_rope = _apply_rope(q_rope, cos, sin)
        return jnp.concatenate([q_nope, q_rope], axis=-1)[:, :, :, None, :]

    def keys_values(ckv, k_rope, rotate):
        B, T = ckv.shape[:2]
        kv = (_rmsnorm(ckv, kv_norm_g) @ w_ukv).reshape(B, T, MLA_HEADS, MLA_NOPE + MLA_V)
        if rotate:
            k_rope = _apply_rope(k_rope, cos, sin)
        k_rope = jnp.broadcast_to(k_rope[:, :, None, :], (B, T, MLA_HEADS, MLA_ROPE))
        return jnp.concatenate([kv[..., :MLA_NOPE], k_rope], axis=-1), kv[..., MLA_NOPE:]

    scale = (MLA_NOPE + MLA_ROPE) ** -0.5
    kc, vc = keys_values(pc[1], pc[2], False)
    kl, vl = keys_values(pl[1], pl[2], True)
    yl = _attend_query_blocks(queries(pl[0], True), jnp.concatenate([kc, kl], axis=1),
                              jnp.concatenate([vc, vl], axis=1), scale)
    yc = _attend(queries(pc[0], False), kc, vc, scale) if with_ctx else None
    return _flat(yc), _flat(yl)


def _gqa(pc, pl, rope, q_norm_g, k_norm_g, with_ctx):
    cos, sin = rope
    G = GQA_HEADS // GQA_KV_HEADS

    def queries(qr, rotate):
        B, T = qr.shape[:2]
        q = _rmsnorm(qr.reshape(B, T, GQA_KV_HEADS, G, HEAD_DIM), q_norm_g)
        return _apply_rope(q, cos, sin) if rotate else q

    def keys_values(kr, vr, rotate):
        B, T = kr.shape[:2]
        k = _rmsnorm(kr.reshape(B, T, GQA_KV_HEADS, HEAD_DIM), k_norm_g)
        if rotate:
            k = _apply_rope(k, cos, sin)
        return k, vr.reshape(B, T, GQA_KV_HEADS, HEAD_DIM)

    scale = HEAD_DIM ** -0.5
    kc, vc = keys_values(pc[1], pc[2], False)
    kl, vl = keys_values(pl[1], pl[2], True)
    yl = _attend_query_blocks(queries(pl[0], True), jnp.concatenate([kc, kl], axis=1),
                              jnp.concatenate([vc, vl], axis=1), scale)
    yc = _attend(queries(pc[0], False), kc, vc, scale) if with_ctx else None
    return _flat(yc), _flat(yl)


def _gdn(pc, pl, conv_w, a_log, dt_bias, norm_g, with_ctx):
    def prep(q, k, v, a, b):
        B, T = q.shape[:2]
        qkv = jax.nn.silu(_centred_depthwise_conv(jnp.concatenate([q, k, v], axis=-1), conv_w)).astype(F32)
        q, k, v = jnp.split(qkv, 3, axis=-1)
        q = _l2norm(q.reshape(B, T, GDN_HEADS, HEAD_DIM))
        k = _l2norm(k.reshape(B, T, GDN_HEADS, HEAD_DIM))
        v = v.reshape(B, T, GDN_HEADS, HEAD_DIM)
        a = a.astype(F32).reshape(B, T, 2, GDN_HEADS)
        b = b.astype(F32).reshape(B, T, 2, GDN_HEADS)
        g = -jnp.exp(a_log.astype(F32)) * jax.nn.softplus(a + dt_bias.astype(F32))
        return q, k, v, g, jax.nn.sigmoid(b)

    qc, kc, vc, gc, bc = prep(pc[0], pc[1], pc[2], pc[4], pc[5])
    ql, kl, vl, gl, bl = prep(pl[0], pl[1], pl[2], pl[4], pl[5])
    zero = jnp.zeros((ql.shape[0], GDN_HEADS, HEAD_DIM, HEAD_DIM), F32)

    def rev(t):
        return jnp.flip(t, axis=1)

    oc_f, state_f = _gated_delta_chunked(qc, kc, vc, gc[:, :, 0], bc[:, :, 0], zero)
    ol_f, _ = _gated_delta_chunked(ql, kl, vl, gl[:, :, 0], bl[:, :, 0], state_f)
    oc_b, state_b = _gated_delta_chunked(rev(qc), rev(kc), rev(vc), rev(gc[:, :, 1]), rev(bc[:, :, 1]), zero)
    ol_b, _ = _gated_delta_chunked(rev(ql), rev(kl), rev(vl), rev(gl[:, :, 1]), rev(bl[:, :, 1]), state_b)

    def out(o, z):
        B, T = z.shape[:2]
        o = _rmsnorm(o, norm_g) * jax.nn.silu(z.astype(F32).reshape(B, T, GDN_HEADS, HEAD_DIM))
        return o.reshape(B, T, GROUP_WIDTH).astype(z.dtype)

    yl = out(ol_f + rev(ol_b), pl[3])
    yc = out(oc_f + rev(oc_b), pc[3]) if with_ctx else None
    return yc, yl


def _swa(pc, pl, rope, sink, with_ctx):
    cos, sin = rope
    G = SWA_HEADS // SWA_KV_HEADS
    sink = sink.astype(F32).reshape(SWA_KV_HEADS, G)

    def queries(qr, rotate):
        B, T = qr.shape[:2]
        q = qr.reshape(B, T, SWA_KV_HEADS, G, HEAD_DIM)
        return _apply_rope(q, cos, sin) if rotate else q

    def keys_values(kr, vr, rotate):
        B, T = kr.shape[:2]
        k = kr.reshape(B, T, SWA_KV_HEADS, HEAD_DIM)
        if rotate:
            k = _apply_rope(k, cos, sin)
        return k, vr.reshape(B, T, SWA_KV_HEADS, HEAD_DIM)

    scale = HEAD_DIM ** -0.5
    kc, vc = keys_values(pc[1], pc[2], False)
    kl, vl = keys_values(pl[1], pl[2], True)
    yl = _banded_sink_attention(queries(pl[0], True), kl, vl, kc, vc, sink, scale)
    yc = None
    if with_ctx:
        qc = queries(pc[0], False)
        s = jnp.einsum('bqhgd,bkhd->bhgqk', qc, kc, preferred_element_type=F32) * scale
        p = _softmax_with_sink(s, sink[:, :, None, None]).astype(vc.dtype)
        yc = jnp.einsum('bhgqk,bkhd->bqhgd', p, vc)
    return _flat(yc), _flat(yl)


def _swiglu(h, w1, w3, w2):
    return (jax.nn.silu(h @ w1) * (h @ w3)) @ w2


def _moe_swiglu(h, router, w1, w3, w2):
    B, T, D = h.shape
    t = h.reshape(-1, D)
    logits = jnp.dot(t, router, preferred_element_type=F32)
    top_v, top_i = lax.top_k(logits, TOP_K)
    gate = jax.nn.softmax(top_v, axis=-1).astype(h.dtype)
    flat = top_i.reshape(-1)
    order = jnp.argsort(flat)
    tok = order // TOP_K
    xs = t[tok]
    sizes = jnp.bincount(flat, length=N_EXPERTS).astype(jnp.int32)
    hid = jax.nn.silu(lax.ragged_dot(xs, w1, sizes)) * lax.ragged_dot(xs, w3, sizes)
    ys = lax.ragged_dot(hid, w2, sizes) * gate.reshape(-1)[order][:, None]
    return jnp.zeros_like(t).at[tok].add(ys).reshape(B, T, D)


def _channel_mixer(layer, h, ffn_w1, ffn_w3, ffn_w2, moe_router, moe_w1, moe_w3, moe_w2):
    i = layer // 2
    if layer % 2 == 0:
        return _swiglu(h, ffn_w1[i], ffn_w3[i], ffn_w2[i])
    return _moe_swiglu(h, moe_router[i], moe_w1[i], moe_w3[i], moe_w2[i])


def setup_inputs(seed: int = 0) -> dict:
    key = jax.random.key(seed)
    keys = iter(jax.random.split(key, 40))

    def normal(shape, scale):
        return jax.random.normal(next(keys), shape, jnp.float32) * scale

    def gain(shape):
        return 1.0 + normal(shape, 0.02)

    D = D_MODEL
    L = DEPTH
    n_dense = (DEPTH + 1) // 2
    n_moe = DEPTH // 2
    dt = jnp.exp(jax.random.uniform(next(keys), (L, 2, GDN_HEADS), jnp.float32,
                                    float(np.log(1e-3)), float(np.log(1e-1))))
    a_init = jax.random.uniform(next(keys), (L, 2, GDN_HEADS), jnp.float32, 1.0, 16.0)
    return {
        'x': normal((BATCH, SEQ, D), 1.0),
        'c': normal((BATCH, D), 1.0),
        'ctx': normal((BATCH, CTX_LEN, D), 1.0),
        'c_ctx': normal((D,), 1.0),
        'norm1_g': gain((L, D)),
        'norm2_g': gain((L, D)),
        'w_mod': normal((L, D, 6 * D), 0.5 * D ** -0.5),
        'b_mod': normal((L, 6 * D), 0.02),
        'w_in': normal((L, D, D_IN), D ** -0.5),
        'mla_q_norm_g': gain((L, MLA_Q_RANK)),
        'mla_kv_norm_g': gain((L, MLA_KV_RANK)),
        'mla_w_uq': normal((L, MLA_Q_RANK, MLA_HEADS * (MLA_NOPE + MLA_ROPE)), MLA_Q_RANK ** -0.5),
        'mla_w_ukv': normal((L, MLA_KV_RANK, MLA_HEADS * (MLA_NOPE + MLA_V)), MLA_KV_RANK ** -0.5),
        'gqa_q_norm_g': gain((L, HEAD_DIM)),
        'gqa_k_norm_g': gain((L, HEAD_DIM)),
        'gdn_conv_w': normal((L, GDN_CONV, 3 * GDN_HEADS * HEAD_DIM), GDN_CONV ** -0.5),
        'gdn_a_log': jnp.log(a_init),
        'gdn_dt_bias': dt + jnp.log(-jnp.expm1(-dt)),
        'gdn_norm_g': gain((L, HEAD_DIM)),
        'swa_sink': normal((L, SWA_HEADS), 0.5),
        'w_out': normal((L, D_MIX, D), D_MIX ** -0.5),
        'ffn_w1': normal((n_dense, D, FFN_DIM), D ** -0.5),
        'ffn_w3': normal((n_dense, D, FFN_DIM), D ** -0.5),
        'ffn_w2': normal((n_dense, FFN_DIM, D), FFN_DIM ** -0.5),
        'moe_router': normal((n_moe, D, N_EXPERTS), D ** -0.5),
        'moe_w1': normal((n_moe, N_EXPERTS, D, EXPERT_DIM), D ** -0.5),
        'moe_w3': normal((n_moe, N_EXPERTS, D, EXPERT_DIM), D ** -0.5),
        'moe_w2': normal((n_moe, N_EXPERTS, EXPERT_DIM, D), EXPERT_DIM ** -0.5),
        'final_norm_g': gain((D,)),
    }


def reference(x, c, ctx, c_ctx, norm1_g, norm2_g, w_mod, b_mod, w_in, mla_q_norm_g, mla_kv_norm_g,
              mla_w_uq, mla_w_ukv, gqa_q_norm_g, gqa_k_norm_g, gdn_conv_w, gdn_a_log, gdn_dt_bias,
              gdn_norm_g, swa_sink, w_out, ffn_w1, ffn_w3, ffn_w2, moe_router, moe_w1, moe_w3, moe_w2,
              final_norm_g):
    n_lat = x.shape[1]
    rope_mla = _axial_rope_tables(n_lat, MLA_ROPE)
    rope_head = _axial_rope_tables(n_lat, HEAD_DIM)
    silu_c = jax.nn.silu(c)[:, None, :]
    silu_cc = jax.nn.silu(c_ctx)[None, None, :]
    xl, xc = x, ctx
    for layer in range(DEPTH):
        with_ctx = layer < DEPTH - 1
        mods_l = jnp.split(silu_c @ w_mod[layer] + b_mod[layer], 6, axis=-1)
        mods_c = jnp.split(silu_cc @ w_mod[layer] + b_mod[layer], 6, axis=-1)
        hl = _modulate(_rmsnorm(xl, norm1_g[layer]), mods_l[0], mods_l[1])
        hc = _modulate(_rmsnorm(xc, norm1_g[layer]), mods_c[0], mods_c[1])
        pl = _split_in(hl @ w_in[layer])
        pc = _split_in(hc @ w_in[layer])
        a_c, a_l = _mla(pc[0:3], pl[0:3], rope_mla, mla_q_norm_g[layer], mla_kv_norm_g[layer],
                        mla_w_uq[layer], mla_w_ukv[layer], with_ctx)
        b_c, b_l = _gqa(pc[3:6], pl[3:6], rope_head, gqa_q_norm_g[layer], gqa_k_norm_g[layer], with_ctx)
        g_c, g_l = _gdn(pc[6:12], pl[6:12], gdn_conv_w[layer], gdn_a_log[layer], gdn_dt_bias[layer],
                        gdn_norm_g[layer], with_ctx)
        d_c, d_l = _swa(pc[12:15], pl[12:15], rope_head, swa_sink[layer], with_ctx)
        xl = xl + mods_l[2] * (jnp.concatenate([a_l, b_l, g_l, d_l], axis=-1) @ w_out[layer])
        hl2 = _modulate(_rmsnorm(xl, norm2_g[layer]), mods_l[3], mods_l[4])
        xl = xl + mods_l[5] * _channel_mixer(layer, hl2, ffn_w1, ffn_w3, ffn_w2,
                                             moe_router, moe_w1, moe_w3, moe_w2)
        if with_ctx:
            xc = xc + mods_c[2] * (jnp.concatenate([a_c, b_c, g_c, d_c], axis=-1) @ w_out[layer])
            hc2 = _modulate(_rmsnorm(xc, norm2_g[layer]), mods_c[3], mods_c[4])
            xc = xc + mods_c[5] * _channel_mixer(layer, hc2, ffn_w1, ffn_w3, ffn_w2,
                                                 moe_router, moe_w1, moe_w3, moe_w2)
    return _rmsnorm(xl, final_norm_g)
```

```python
import functools

import numpy as np
import jax
import jax.numpy as jnp
from jax import lax
from jax.experimental import pallas as pl
from jax.experimental.pallas import tpu as pltpu

F32 = jnp.float32
BF16 = jnp.bfloat16
HIGHEST = lax.Precision.HIGHEST

GRID_W = 64
HEAD_DIM = 128
ROPE_THETA = 10000.0
NORM_EPS = 1e-6
NEG_INF = -1e30
MLA_HEADS = 4
MLA_NOPE = 128
MLA_ROPE = 64
MLA_V = 128
MLA_Q_RANK = 384
MLA_KV_RANK = 256
GQA_HEADS = 4
GQA_KV_HEADS = 2
GDN_HEADS = 4
GDN_CONV = 5
GDN_CHUNK = 64
SWA_HEADS = 4
SWA_KV_HEADS = 2
SWA_WINDOW = 128
N_EXPERTS = 8
TOP_K = 2
GROUP_WIDTH = 512

ROW_BLOCK = 256
LANES = 128
MOE_TILE = 512
VMEM_LIMIT = 56 * 1024 * 1024

P_GDN = 0
P_GQA = 2048
P_SWA = 3072
P_MLA = 4096
P_AB = 4864
P_WIDTH = 5120


def _cparams(sem, vmem=VMEM_LIMIT):
    return pltpu.CompilerParams(dimension_semantics=sem, vmem_limit_bytes=vmem)


def _table_row(gi, nb, ncb, ctx_row):
    b = gi // nb
    r = gi - b * nb
    return jnp.where(r < ncb, ctx_row, b)


def _mods_kernel(c_ref, w_ref, b_ref, o_ref):
    c = c_ref[...]
    a = (c * jax.nn.sigmoid(c)).astype(BF16)
    o_ref[0] = jnp.dot(a, w_ref[0].astype(BF16), preferred_element_type=F32) + b_ref[0]


def _mods(cmat, w_mod, b_mod):
    L, D, N = w_mod.shape
    M = cmat.shape[0]
    tn = 1024
    return pl.pallas_call(
        _mods_kernel,
        grid=(L, N // tn),
        in_specs=[pl.BlockSpec((M, D), lambda l, j: (0, 0)),
                  pl.BlockSpec((1, D, tn), lambda l, j: (l, 0, j)),
                  pl.BlockSpec((1, 1, tn), lambda l, j: (l, 0, j))],
        out_specs=pl.BlockSpec((1, M, tn), lambda l, j: (l, 0, j)),
        out_shape=jax.ShapeDtypeStruct((L, M, N), F32),
        compiler_params=_cparams(("parallel", "parallel")),
        name="mods",
    )(cmat, w_mod, b_mod.reshape(L, 1, N))


def _norm_mod_kernel(x_ref, g_ref, tab_ref, *rest, nb, ncb, ctx_row, which, router):
    idx = _table_row(pl.program_id(0), nb, ncb, ctx_row)
    x = x_ref[...]
    y = x * lax.rsqrt(jnp.mean(x * x, axis=-1, keepdims=True) + NORM_EPS) * g_ref[...]
    shift = tab_ref[pl.ds(idx * 6 + which, 1), :]
    scale = tab_ref[pl.ds(idx * 6 + which + 1, 1), :]
    h = y * (1.0 + scale) + shift
    if router:
        r_ref, h_ref, l_ref = rest
        h_ref[...] = h
        l_ref[...] = jnp.dot(h, r_ref[...], preferred_element_type=F32, precision=HIGHEST)
    else:
        (h_ref,) = rest
        h_ref[...] = h.astype(h_ref.dtype)


def _norm_mod(x, g, tab, *, nb, ncb, ctx_row, which, router=None):
    R, D = x.shape
    kern = functools.partial(_norm_mod_kernel, nb=nb, ncb=ncb, ctx_row=ctx_row, which=which,
                             router=router is not None)
    in_specs = [pl.BlockSpec((ROW_BLOCK, D), lambda i: (i, 0)),
                pl.BlockSpec((1, D), lambda i: (0, 0)),
                pl.BlockSpec(tab.shape, lambda i: (0, 0))]
    args = [x, g.reshape(1, D), tab]
    if router is None:
        out_specs = pl.BlockSpec((ROW_BLOCK, D), lambda i: (i, 0))
        out_shape = jax.ShapeDtypeStruct((R, D), BF16)
    else:
        in_specs.append(pl.BlockSpec(router.shape, lambda i: (0, 0)))
        args.append(router)
        out_specs = [pl.BlockSpec((ROW_BLOCK, D), lambda i: (i, 0)),
                     pl.BlockSpec((ROW_BLOCK, LANES), lambda i: (i, 0))]
        out_shape = [jax.ShapeDtypeStruct((R, D), F32), jax.ShapeDtypeStruct((R, LANES), F32)]
    return pl.pallas_call(kern, grid=(R // ROW_BLOCK,), in_specs=in_specs, out_specs=out_specs,
                          out_shape=out_shape, compiler_params=_cparams(("parallel",)),
                          name="norm_mod")(*args)


def _final_norm_kernel(x_ref, g_ref, o_ref):
    x = x_ref[...]
    o_ref[...] = x * lax.rsqrt(jnp.mean(x * x, axis=-1, keepdims=True) + NORM_EPS) * g_ref[...]


def _final_norm(x, g):
    R, D = x.shape
    return pl.pallas_call(
        _final_norm_kernel, grid=(R // ROW_BLOCK,),
        in_specs=[pl.BlockSpec((ROW_BLOCK, D), lambda i: (i, 0)),
                  pl.BlockSpec((1, D), lambda i: (0, 0))],
        out_specs=pl.BlockSpec((ROW_BLOCK, D), lambda i: (i, 0)),
        out_shape=jax.ShapeDtypeStruct((R, D), F32),
        compiler_params=_cparams(("parallel",)), name="final_norm")(x, g.reshape(1, D))


def _pick_tile(n, options):
    for t in options:
        if n % t == 0:
            return t
    raise ValueError(f"no tile for {n}")


def _mm_plain_kernel(a_ref, w_ref, o_ref):
    o_ref[...] = jnp.dot(a_ref[...], w_ref[...], preferred_element_type=F32).astype(o_ref.dtype)


def _mm_plain(a, w, out_dtype):
    M, K = a.shape
    N = w.shape[1]
    tm = _pick_tile(M, (1024, 512, 256))
    tn = _pick_tile(N, (512, 256, 128))
    return pl.pallas_call(
        _mm_plain_kernel, grid=(M // tm, N // tn),
        in_specs=[pl.BlockSpec((tm, K), lambda i, j: (i, 0)),
                  pl.BlockSpec((K, tn), lambda i, j: (0, j))],
        out_specs=pl.BlockSpec((tm, tn), lambda i, j: (i, j)),
        out_shape=jax.ShapeDtypeStruct((M, N), out_dtype),
        compiler_params=_cparams(("parallel", "parallel")), name="mm_plain")(a, w)


def _mm_swiglu_kernel(a_ref, w1_ref, w3_ref, o_ref):
    a = a_ref[...]
    h1 = jnp.dot(a, w1_ref[...], preferred_element_type=F32)
    h3 = jnp.dot(a, w3_ref[...], preferred_element_type=F32)
    o_ref[...] = (h1 * jax.nn.sigmoid(h1) * h3).astype(o_ref.dtype)


def _mm_swiglu(a, w1, w3):
    M, K = a.shape
    N = w1.shape[1]
    tm = _pick_tile(M, (1024, 512, 256))
    tn = _pick_tile(N, (512, 256, 128))
    return pl.pallas_call(
        _mm_swiglu_kernel, grid=(M // tm, N // tn),
        in_specs=[pl.BlockSpec((tm, K), lambda i, j: (i, 0)),
                  pl.BlockSpec((K, tn), lambda i, j: (0, j)),
                  pl.BlockSpec((K, tn), lambda i, j: (0, j))],
        out_specs=pl.BlockSpec((tm, tn), lambda i, j: (i, j)),
        out_shape=jax.ShapeDtypeStruct((M, N), BF16),
        compiler_params=_cparams(("parallel", "parallel")), name="mm_swiglu")(a, w1, w3)


def _mm_res_kernel(*refs, n_a, tm, tn, nb, ncb, ctx_row, which):
    a_refs = refs[:n_a]
    w_ref, res_ref, tab_ref, o_ref = refs[n_a:]
    i = pl.program_id(0)
    j = pl.program_id(1)
    acc = None
    k0 = 0
    for a_ref in a_refs:
        kk = a_ref.shape[1]
        part = jnp.dot(a_ref[...], w_ref[k0:k0 + kk, :], preferred_element_type=F32)
        acc = part if acc is None else acc + part
        k0 += kk
    for c in range(tm // ROW_BLOCK):
        idx = _table_row(i * (tm // ROW_BLOCK) + c, nb, ncb, ctx_row)
        gate = tab_ref[pl.ds(idx * 6 + which, 1), pl.ds(pl.multiple_of(j * tn, tn), tn)]
        rows = slice(c * ROW_BLOCK, (c + 1) * ROW_BLOCK)
        o_ref[rows, :] = res_ref[rows, :] + gate * acc[rows, :]


def _mm_res(a_list, w, res, tab, *, nb, ncb, ctx_row, which, tm, tn,
            in_row_map=None, out_rows=None):
    K, N = w.shape
    M = res.shape[0] if out_rows is None else out_rows
    rm = (lambda i: i) if in_row_map is None else in_row_map
    kern = functools.partial(_mm_res_kernel, n_a=len(a_list), tm=tm, tn=tn, nb=nb, ncb=ncb,
                             ctx_row=ctx_row, which=which)

    def a_spec(a):
        if a.shape[0] == res.shape[0]:
            return pl.BlockSpec((tm, a.shape[1]), lambda i, j: (rm(i), 0))
        assert a.shape[0] == M
        return pl.BlockSpec((tm, a.shape[1]), lambda i, j: (i, 0))

    in_specs = [a_spec(a) for a in a_list]
    in_specs += [pl.BlockSpec((K, tn), lambda i, j: (0, j)),
                 pl.BlockSpec((tm, tn), lambda i, j: (rm(i), j)),
                 pl.BlockSpec(tab.shape, lambda i, j: (0, 0))]
    return pl.pallas_call(
        kern, grid=(M // tm, N // tn), in_specs=in_specs,
        out_specs=pl.BlockSpec((tm, tn), lambda i, j: (i, j)),
        out_shape=jax.ShapeDtypeStruct((M, N), F32),
        compiler_params=_cparams(("parallel", "parallel")), name="mm_res")(*a_list, w, res, tab)


def _rope_lanes(x, cos, sa, sb, quarter):
    left = pltpu.roll(x, LANES - quarter, axis=1)
    right = pltpu.roll(x, quarter, axis=1)
    return x * cos + left * sa + right * sb


def _rope_tables(S, C, rot_dim):
    rows = S // GRID_W
    row = jnp.repeat(jnp.arange(rows), GRID_W).astype(F32)
    col = jnp.tile(jnp.arange(GRID_W), rows).astype(F32)
    half = rot_dim // 2
    inv_freq = ROPE_THETA ** (-jnp.arange(0, half, 2, dtype=F32) / half)
    ang_r = row[:, None] * inv_freq
    ang_c = col[:, None] * inv_freq
    ang = jnp.concatenate([ang_r, ang_r, ang_c, ang_c], axis=-1)
    cos, sin = jnp.cos(ang), jnp.sin(ang)
    quarter = rot_dim // 4
    even = ((jnp.arange(rot_dim) // quarter) % 2) == 0
    sa = jnp.where(even, -sin, 0.0)
    sb = jnp.where(even, 0.0, sin)
    pad = LANES - rot_dim

    def full(t, ctx_val):
        t = jnp.pad(t, ((0, 0), (0, pad)), constant_values=ctx_val)
        return jnp.concatenate([jnp.full((C, LANES), ctx_val, F32), t], axis=0)

    return full(cos, 1.0), full(sa, 0.0), full(sb, 0.0)


def _mla_prep_kernel(p_ref, qg_ref, kvg_ref, wq_ref, wkv_ref, cos_ref, sa_ref, sb_ref,
                     q_ref, k_ref, v_ref, *, scale):
    cos, sa, sb = cos_ref[...], sa_ref[...], sb_ref[...]
    cq = p_ref[:, 0:MLA_Q_RANK]
    qn = cq * lax.rsqrt(jnp.mean(cq * cq, axis=-1, keepdims=True) + NORM_EPS) * qg_ref[...]
    q_raw = jnp.dot(qn.astype(BF16), wq_ref[...], preferred_element_type=F32)
    ckv = p_ref[:, MLA_Q_RANK:MLA_Q_RANK + MLA_KV_RANK]
    kvn = ckv * lax.rsqrt(jnp.mean(ckv * ckv, axis=-1, keepdims=True) + NORM_EPS) * kvg_ref[...]
    kv = jnp.dot(kvn.astype(BF16), wkv_ref[...], preferred_element_type=F32)
    kr = p_ref[:, MLA_Q_RANK + MLA_KV_RANK:MLA_Q_RANK + MLA_KV_RANK + LANES]
    kr = _rope_lanes(kr, cos, sa, sb, MLA_ROPE // 4).astype(BF16)
    for h in range(MLA_HEADS):
        c0 = 2 * LANES * h
        q_ref[:, c0:c0 + LANES] = (q_raw[:, c0:c0 + LANES] * scale).astype(BF16)
        qr = _rope_lanes(q_raw[:, c0 + LANES:c0 + 2 * LANES], cos, sa, sb, MLA_ROPE // 4)
        q_ref[:, c0 + LANES:c0 + 2 * LANES] = (qr * scale).astype(BF16)
        k_ref[:, c0:c0 + LANES] = kv[:, LANES * h:LANES * (h + 1)].astype(BF16)
        k_ref[:, c0 + LANES:c0 + 2 * LANES] = kr
    v_ref[...] = kv[:, MLA_HEADS * LANES:].astype(BF16)


def _mla_prep(P, qg, kvg, wq_p, wkv_p, tabs, nbt):
    R = P.shape[0]
    width = MLA_HEADS * 2 * LANES
    kern = functools.partial(_mla_prep_kernel, scale=float((MLA_NOPE + MLA_ROPE) ** -0.5))
    tab_spec = pl.BlockSpec((ROW_BLOCK, LANES), lambda i: (i % nbt, 0))
    const = lambda a: pl.BlockSpec(a.shape, lambda i: (0, 0))
    return pl.pallas_call(
        kern, grid=(R // ROW_BLOCK,),
        in_specs=[pl.BlockSpec((ROW_BLOCK, 1024), lambda i: (i, P_MLA // 1024)),
                  const(qg), const(kvg), const(wq_p), const(wkv_p), tab_spec, tab_spec, tab_spec],
        out_specs=[pl.BlockSpec((ROW_BLOCK, width), lambda i: (i, 0)),
                   pl.BlockSpec((ROW_BLOCK, width), lambda i: (i, 0)),
                   pl.BlockSpec((ROW_BLOCK, MLA_HEADS * MLA_V), lambda i: (i, 0))],
        out_shape=[jax.ShapeDtypeStruct((R, width), BF16), jax.ShapeDtypeStruct((R, width), BF16),
                   jax.ShapeDtypeStruct((R, MLA_HEADS * MLA_V), BF16)],
        compiler_params=_cparams(("parallel",)), name="mla_prep")(P, qg, kvg, wq_p, wkv_p, *tabs)


def _head_prep_kernel(p_ref, qg_ref, kg_ref, cos_ref, sa_ref, sb_ref, o_ref, *, norm, scale,
                      n_q, n_k):
    cos, sa, sb = cos_ref[...], sa_ref[...], sb_ref[...]
    for h in range(n_q + n_k):
        x = p_ref[:, LANES * h:LANES * (h + 1)]
        if norm:
            g = qg_ref[...] if h < n_q else kg_ref[...]
            x = x * lax.rsqrt(jnp.mean(x * x, axis=-1, keepdims=True) + NORM_EPS) * g
        x = _rope_lanes(x, cos, sa, sb, HEAD_DIM // 4)
        if h < n_q:
            x = x * scale
        o_ref[:, LANES * h:LANES * (h + 1)] = x.astype(BF16)
    v0 = LANES * (n_q + n_k)
    o_ref[:, v0:] = p_ref[:, v0:].astype(BF16)


def _head_prep(P, col0, qg, kg, tabs, nbt, *, norm, n_q, n_k):
    R = P.shape[0]
    kern = functools.partial(_head_prep_kernel, norm=norm, scale=float(HEAD_DIM ** -0.5),
                             n_q=n_q, n_k=n_k)
    tab_spec = pl.BlockSpec((ROW_BLOCK, LANES), lambda i: (i % nbt, 0))
    const = lambda a: pl.BlockSpec(a.shape, lambda i: (0, 0))
    return pl.pallas_call(
        kern, grid=(R // ROW_BLOCK,),
        in_specs=[pl.BlockSpec((ROW_BLOCK, 1024), lambda i: (i, col0 // 1024)),
                  const(qg), const(kg), tab_spec, tab_spec, tab_spec],
        out_specs=pl.BlockSpec((ROW_BLOCK, 1024), lambda i: (i, 0)),
        out_shape=jax.ShapeDtypeStruct((R, 1024), BF16),
        compiler_params=_cparams(("parallel",)), name="head_prep")(P, qg, kg, *tabs)


def _attn_kernel(q_ref, k_ref, v_ref, o_ref, *, G, dq, dv, C, T, ctx_tile):
    def run(nk):
        k = k_ref[0, :nk, :]
        v = v_ref[0, :nk, :]
        for g in range(G):
            q = q_ref[0, :, g * dq:(g + 1) * dq]
            s = lax.dot_general(q, k, (((1,), (1,)), ((), ())), preferred_element_type=F32)
            m = jnp.max(s, axis=-1, keepdims=True)
            p = jnp.exp(s - m)
            l = jnp.sum(p, axis=-1, keepdims=True)
            o = jnp.dot(p.astype(BF16), v, preferred_element_type=F32) / l
            o_ref[0, :, g * dv:(g + 1) * dv] = o.astype(o_ref.dtype)

    if ctx_tile:
        qi = pl.program_id(2)
        pl.when(qi == 0)(lambda: run(C))
        pl.when(qi > 0)(lambda: run(T))
    else:
        run(T)


def _attention(q, k, v, *, Hk, G, dq, dv, q_col0, k_col0, v_col0, C, with_ctx):
    Bn, T, _ = q.shape
    ncb = C // ROW_BLOCK
    assert ncb == 1
    nq = T // ROW_BLOCK if with_ctx else (T - C) // ROW_BLOCK
    qoff = 0 if with_ctx else ncb
    kern = functools.partial(_attn_kernel, G=G, dq=dq, dv=dv, C=C, T=T, ctx_tile=with_ctx)
    return pl.pallas_call(
        kern, grid=(Bn, Hk, nq),
        in_specs=[pl.BlockSpec((1, ROW_BLOCK, G * dq), lambda b, h, i: (b, i + qoff, q_col0 // (G * dq) + h)),
                  pl.BlockSpec((1, T, dq), lambda b, h, i: (b, 0, k_col0 // dq + h)),
                  pl.BlockSpec((1, T, dv), lambda b, h, i: (b, 0, v_col0 // dv + h))],
        out_specs=pl.BlockSpec((1, ROW_BLOCK, G * dv), lambda b, h, i: (b, i, h)),
        out_shape=jax.ShapeDtypeStruct((Bn, nq * ROW_BLOCK, Hk * G * dv), BF16),
        compiler_params=_cparams(("parallel", "parallel", "parallel")), name="attention")(q, k, v)


def _swa_attention(qkv, sink, *, C, with_ctx):
    Bn, T, _ = qkv.shape
    G = SWA_HEADS // SWA_KV_HEADS
    d = HEAD_DIM
    ncb = C // ROW_BLOCK
    assert ncb == 1 and T - C >= ROW_BLOCK + 2 * SWA_WINDOW
    nq = T // ROW_BLOCK if with_ctx else (T - C) // ROW_BLOCK
    qoff = 0 if with_ctx else ncb
    kern = functools.partial(_swa_kernel, G=G, C=C, S=T - C, ctx_tile=with_ctx, qoff=qoff)
    return pl.pallas_call(
        kern, grid=(Bn, SWA_KV_HEADS, nq),
        in_specs=[pl.BlockSpec(memory_space=pltpu.SMEM),
                  pl.BlockSpec((1, ROW_BLOCK, G * d), lambda b, h, i: (b, i + qoff, h)),
                  pl.BlockSpec((1, T, d), lambda b, h, i: (b, 0, SWA_HEADS + h)),
                  pl.BlockSpec((1, T, d), lambda b, h, i: (b, 0, SWA_HEADS + SWA_KV_HEADS + h))],
        out_specs=pl.BlockSpec((1, ROW_BLOCK, G * d), lambda b, h, i: (b, i, h)),
        out_shape=jax.ShapeDtypeStruct((Bn, nq * ROW_BLOCK, SWA_HEADS * d), BF16),
        compiler_params=_cparams(("parallel", "parallel", "parallel")), name="swa")(
            sink, qkv, qkv, qkv)


def _swa_kernel(sink_ref, q_ref, k_ref, v_ref, o_ref, *, G, C, S, ctx_tile, qoff):
    qi = pl.program_id(2) + qoff
    d = HEAD_DIM
    W = SWA_WINDOW
    WIN = ROW_BLOCK + 2 * W
    hk = pl.program_id(1)
    kc = k_ref[0, :C, :]
    vc = v_ref[0, :C, :]
    nt = (((1,), (1,)), ((), ()))

    def finish(g, parts):
        sink = sink_ref[hk * G + g]
        m = jnp.full((ROW_BLOCK, 1), sink, F32)
        for s, _ in parts:
            m = jnp.maximum(m, jnp.max(s, axis=-1, keepdims=True))
        den = jnp.exp(sink - m)
        o = None
        for s, v in parts:
            e = jnp.exp(s - m)
            den = den + jnp.sum(e, axis=-1, keepdims=True)
            pv = jnp.dot(e.astype(BF16), v, preferred_element_type=F32)
            o = pv if o is None else o + pv
        o_ref[0, :, g * d:(g + 1) * d] = (o / den).astype(o_ref.dtype)

    def run_ctx():
        for g in range(G):
            q = q_ref[0, :, g * d:(g + 1) * d]
            finish(g, [(lax.dot_general(q, kc, nt, preferred_element_type=F32), vc)])

    def run_lat():
        q0 = (qi - (C // ROW_BLOCK)) * ROW_BLOCK
        ws = jnp.clip(q0 - W, 0, S - WIN)
        start = pl.multiple_of(C + ws, LANES)
        kw = k_ref[0, pl.ds(start, WIN), :]
        vw = v_ref[0, pl.ds(start, WIN), :]
        qpos = q0 + lax.broadcasted_iota(jnp.int32, (ROW_BLOCK, WIN), 0)
        kpos = ws + lax.broadcasted_iota(jnp.int32, (ROW_BLOCK, WIN), 1)
        valid = jnp.abs(kpos - qpos) <= W
        for g in range(G):
            q = q_ref[0, :, g * d:(g + 1) * d]
            s_loc = jnp.where(valid, lax.dot_general(q, kw, nt, preferred_element_type=F32), NEG_INF)
            s_ctx = lax.dot_general(q, kc, nt, preferred_element_type=F32)
            finish(g, [(s_loc, vw), (s_ctx, vc)])

    if ctx_tile:
        pl.when(qi == 0)(run_ctx)
        pl.when(qi > 0)(run_lat)
    else:
        run_lat()


def _gdn_prep_kernel(x_ref, w_ref, o_ref, *, C, T):
    j = pl.program_id(1)
    x = x_ref[0]
    t = lax.broadcasted_iota(jnp.int32, (T, 1), 0)
    is_lat = t >= C
    lo = jnp.where(is_lat, C, 0)
    hi = jnp.where(is_lat, T, C)
    acc = None
    half = (GDN_CONV - 1) // 2
    for tap in range(GDN_CONV):
        s = tap - half
        xs = x if s == 0 else pltpu.roll(x, (-s) % T, axis=0)
        ok = jnp.logical_and(t + s >= lo, t + s < hi)
        term = jnp.where(ok, xs, 0.0) * w_ref[tap:tap + 1, :]
        acc = term if acc is None else acc + term
    y = acc * jax.nn.sigmoid(acc)
    inv = lax.rsqrt(jnp.sum(y * y, axis=-1, keepdims=True) + 1e-6)
    nh = GDN_HEADS
    f = jnp.where(j < 2 * nh, inv, 1.0) * jnp.where(j < nh, float(HEAD_DIM ** -0.5), 1.0)
    o_ref[0] = y * f


def _gdn_prep(P3, conv_w8, C):
    Bn, T, _ = P3.shape
    nblk = 3 * GDN_HEADS
    kern = functools.partial(_gdn_prep_kernel, C=C, T=T)
    return pl.pallas_call(
        kern, grid=(Bn, nblk),
        in_specs=[pl.BlockSpec((1, T, LANES), lambda b, j: (b, 0, P_GDN // LANES + j)),
                  pl.BlockSpec((8, LANES), lambda b, j: (0, j))],
        out_specs=pl.BlockSpec((1, T, LANES), lambda b, j: (b, 0, j)),
        out_shape=jax.ShapeDtypeStruct((Bn, T, nblk * LANES), F32),
        compiler_params=_cparams(("parallel", "parallel")), name="gdn_prep")(P3, conv_w8)


def _dot32(a, b):
    return jnp.dot(a, b, preferred_element_type=F32, precision=HIGHEST)


def _dot32_nt(a, b):
    return lax.dot_general(a, b, (((1,), (1,)), ((), ())), preferred_element_type=F32,
                           precision=HIGHEST)


def _dot32_tn(a, b):
    return lax.dot_general(a, b, (((0,), (0,)), ((), ())), preferred_element_type=F32,
                           precision=HIGHEST)


def _gdn_chunk_kernel(qf_ref, qb_ref, abf_ref, abb_ref, par_ref, of_ref, ob_ref, s_ref):
    Cc = GDN_CHUNK
    H = GDN_HEADS
    d = HEAD_DIM

    @pl.when(pl.program_id(1) == 0)
    def _():
        s_ref[...] = jnp.zeros_like(s_ref)

    ri = lax.broadcasted_iota(jnp.int32, (Cc, Cc), 0)
    ci = lax.broadcasted_iota(jnp.int32, (Cc, Cc), 1)
    eye = (ri == ci).astype(F32)
    ones = jnp.ones((Cc, Cc), F32)
    neg_a = par_ref[0:1, :]
    dt_b = par_ref[1:2, :]

    for direction in range(2):
        x_ref = qf_ref if direction == 0 else qb_ref
        ab = (abf_ref if direction == 0 else abb_ref)[0]
        o_ref = of_ref if direction == 0 else ob_ref
        if direction == 0:
            incl, incl_t, strict = ri >= ci, ri <= ci, ri > ci
        else:
            incl, incl_t, strict = ri <= ci, ri >= ci, ri < ci
        incl_f = incl.astype(F32)
        z = ab + dt_b
        sp = jnp.maximum(z, 0.0) + jnp.log1p(jnp.exp(-jnp.abs(z)))
        gl_all = neg_a * sp
        beta_all = jax.nn.sigmoid(ab)
        for h in range(H):
            lane = direction * H + h
            q = x_ref[0, :, h * d:(h + 1) * d]
            k = x_ref[0, :, (H + h) * d:(H + h + 1) * d]
            v = x_ref[0, :, (2 * H + h) * d:(2 * H + h + 1) * d]
            gl = jnp.broadcast_to(gl_all[:, lane:lane + 1], (Cc, Cc))
            beta = beta_all[:, 2 * H + lane:2 * H + lane + 1]
            gc_col = _dot32(incl_f, gl)
            gc_row = _dot32(ones, jnp.where(incl_t, gl, 0.0))
            g_tot = _dot32(ones, gl)[:, 0:1]
            gc = gc_col[:, 0:1]
            decay = jnp.where(incl, jnp.exp(jnp.where(incl, gc_col - gc_row, 0.0)), 0.0)
            kb = k * beta
            a_mat = jnp.where(strict, _dot32_nt(kb, k) * decay, 0.0)
            pw = -a_mat
            tinv = eye + pw
            n_sq = int(np.log2(Cc)) - 1
            for _ in range(n_sq):
                pw = _dot32(pw, pw)
                tinv = tinv + _dot32(tinv, pw)
            eg = jnp.exp(gc)
            u = _dot32(tinv, v * beta)
            w = _dot32(tinv, kb * eg)
            intra = jnp.where(incl, _dot32_nt(q, k) * decay, 0.0)
            state = s_ref[lane]
            v_new = u - _dot32(w, state)
            o = _dot32(q * eg, state) + _dot32(intra, v_new)
            o_ref[0, :, h * d:(h + 1) * d] = o
            s_ref[lane] = state * jnp.exp(g_tot[0:1, :]) + _dot32_tn(k * jnp.exp(g_tot - gc), v_new)


def _gdn_chunks(qkv, P3, par, C):
    Bn, T, _ = qkv.shape
    Cc = GDN_CHUNK
    nc = T // Cc
    ncc = C // Cc
    H = GDN_HEADS

    def bwd_chunk(s):
        return jnp.where(s < ncc, ncc - 1 - s, nc + ncc - 1 - s)

    ab_blk = P_AB // LANES
    return pl.pallas_call(
        _gdn_chunk_kernel, grid=(Bn, nc),
        in_specs=[pl.BlockSpec((1, Cc, 3 * H * HEAD_DIM), lambda b, s: (b, s, 0)),
                  pl.BlockSpec((1, Cc, 3 * H * HEAD_DIM), lambda b, s: (b, bwd_chunk(s), 0)),
                  pl.BlockSpec((1, Cc, LANES), lambda b, s: (b, s, ab_blk)),
                  pl.BlockSpec((1, Cc, LANES), lambda b, s: (b, bwd_chunk(s), ab_blk)),
                  pl.BlockSpec((8, LANES), lambda b, s: (0, 0))],
        out_specs=[pl.BlockSpec((1, Cc, H * HEAD_DIM), lambda b, s: (b, s, 0)),
                   pl.BlockSpec((1, Cc, H * HEAD_DIM), lambda b, s: (b, bwd_chunk(s), 0))],
        out_shape=[jax.ShapeDtypeStruct((Bn, T, H * HEAD_DIM), F32),
                   jax.ShapeDtypeStruct((Bn, T, H * HEAD_DIM), F32)],
        scratch_shapes=[pltpu.VMEM((2 * H, HEAD_DIM, HEAD_DIM), F32)],
        compiler_params=_cparams(("parallel", "arbitrary")), name="gdn_chunks")(
            qkv, qkv, P3, P3, par)


def _gdn_post_kernel(of_ref, ob_ref, z_ref, g_ref, o_ref):
    d = HEAD_DIM
    for h in range(GDN_HEADS):
        o = of_ref[:, h * d:(h + 1) * d] + ob_ref[:, h * d:(h + 1) * d]
        y = o * lax.rsqrt(jnp.mean(o * o, axis=-1, keepdims=True) + NORM_EPS) * g_ref[...]
        z = z_ref[:, h * d:(h + 1) * d]
        o_ref[:, h * d:(h + 1) * d] = (y * (z * jax.nn.sigmoid(z))).astype(o_ref.dtype)


def _gdn_post(of, ob, P, g):
    R, W = of.shape
    zblk = (P_GDN + 3 * W) // W
    return pl.pallas_call(
        _gdn_post_kernel, grid=(R // ROW_BLOCK,),
        in_specs=[pl.BlockSpec((ROW_BLOCK, W), lambda i: (i, 0)),
                  pl.BlockSpec((ROW_BLOCK, W), lambda i: (i, 0)),
                  pl.BlockSpec((ROW_BLOCK, W), lambda i: (i, zblk)),
                  pl.BlockSpec((1, HEAD_DIM), lambda i: (0, 0))],
        out_specs=pl.BlockSpec((ROW_BLOCK, W), lambda i: (i, 0)),
        out_shape=jax.ShapeDtypeStruct((R, W), BF16),
        compiler_params=_cparams(("parallel",)), name="gdn_post")(of, ob, P, g)


def _gather_rows_kernel(tok_ref, h_hbm, o_ref, buf, sem):
    n = buf.shape[0]

    def row_copy(r, t):
        return pltpu.make_async_copy(h_hbm.at[pl.ds(t, 1), :], buf.at[pl.ds(r, 1), :], sem)

    def start(r, c):
        row_copy(r, tok_ref[0, 0, r]).start()
        return c

    def wait(r, c):
        row_copy(r, tok_ref[0, 0, r]).wait()
        return c

    lax.fori_loop(0, n, start, 0)
    lax.fori_loop(0, n, wait, 0)
    o_ref[...] = buf[...].astype(o_ref.dtype)


def _gather_rows(h, src_tok):
    D = h.shape[1]
    n = src_tok.shape[0]
    nt = n // ROW_BLOCK
    return pl.pallas_call(
        _gather_rows_kernel, grid=(nt,),
        in_specs=[pl.BlockSpec((1, 1, ROW_BLOCK), lambda i: (i, 0, 0), memory_space=pltpu.SMEM),
                  pl.BlockSpec(memory_space=pl.ANY)],
        out_specs=pl.BlockSpec((ROW_BLOCK, D), lambda i: (i, 0)),
        out_shape=jax.ShapeDtypeStruct((n, D), BF16),
        scratch_shapes=[pltpu.VMEM((ROW_BLOCK, D), F32), pltpu.SemaphoreType.DMA(())],
        compiler_params=_cparams(("arbitrary",)), name="moe_gather")(
            src_tok.reshape(nt, 1, ROW_BLOCK), h)


def _gmm_kernel(te_ref, tv_ref, x_ref, w1_ref, w3_ref, w2_ref, g_ref, o_ref, acc_ref):
    i = pl.program_id(0)
    f = pl.program_id(1)
    nf = pl.num_programs(1)

    @pl.when(tv_ref[i] == 1)
    def _():
        x = x_ref[...]
        h1 = jnp.dot(x, w1_ref[0], preferred_element_type=F32)
        h3 = jnp.dot(x, w3_ref[0], preferred_element_type=F32)
        hid = (h1 * jax.nn.sigmoid(h1) * h3).astype(BF16)
        part = jnp.dot(hid, w2_ref[0], preferred_element_type=F32)

        @pl.when(f == 0)
        def _():
            acc_ref[...] = part

        @pl.when(f > 0)
        def _():
            acc_ref[...] += part

        @pl.when(f == nf - 1)
        def _():
            o_ref[...] = acc_ref[...] * g_ref[...]

    @pl.when(jnp.logical_and(tv_ref[i] == 0, f == nf - 1))
    def _():
        o_ref[...] = jnp.zeros_like(o_ref)


def _gmm(xs, w1, w3, w2, gate_sorted, tile_e, tile_valid):
    n, D = xs.shape
    E, _, F = w1.shape
    tf = 512
    nt = n // MOE_TILE
    grid_spec = pltpu.PrefetchScalarGridSpec(
        num_scalar_prefetch=2, grid=(nt, F // tf),
        in_specs=[pl.BlockSpec((MOE_TILE, D), lambda i, f, te, tv: (i, 0)),
                  pl.BlockSpec((1, D, tf), lambda i, f, te, tv: (te[i], 0, f)),
                  pl.BlockSpec((1, D, tf), lambda i, f, te, tv: (te[i], 0, f)),
                  pl.BlockSpec((1, tf, D), lambda i, f, te, tv: (te[i], f, 0)),
                  pl.BlockSpec((MOE_TILE, 1), lambda i, f, te, tv: (i, 0))],
        out_specs=pl.BlockSpec((MOE_TILE, D), lambda i, f, te, tv: (i, 0)),
        scratch_shapes=[pltpu.VMEM((MOE_TILE, D), F32)])
    return pl.pallas_call(
        _gmm_kernel, grid_spec=grid_spec,
        out_shape=jax.ShapeDtypeStruct((n, D), F32),
        compiler_params=_cparams(("parallel", "arbitrary")), name="moe_gmm")(
            tile_e, tile_valid, xs, w1, w3, w2, gate_sorted.reshape(n, 1))


def _combine_kernel(pos_ref, ys_hbm, x_ref, tab_ref, o_ref, buf0, buf1, sem, *, nb, ncb, ctx_row,
                    which):
    n = buf0.shape[0]
    idx = _table_row(pl.program_id(0), nb, ncb, ctx_row)

    def copies(r):
        p0 = pos_ref[0, 0, r]
        p1 = pos_ref[0, 1, r]
        return (pltpu.make_async_copy(ys_hbm.at[pl.ds(p0, 1), :], buf0.at[pl.ds(r, 1), :], sem.at[0]),
                pltpu.make_async_copy(ys_hbm.at[pl.ds(p1, 1), :], buf1.at[pl.ds(r, 1), :], sem.at[1]))

    def start(r, c):
        c0, c1 = copies(r)
        c0.start()
        c1.start()
        return c

    def wait(r, c):
        c0, c1 = copies(r)
        c0.wait()
        c1.wait()
        return c

    lax.fori_loop(0, n, start, 0)
    lax.fori_loop(0, n, wait, 0)
    gate = tab_ref[pl.ds(idx * 6 + which, 1), :]
    o_ref[...] = x_ref[...] + gate * (buf0[...] + buf1[...])


def _combine(ys, pos, x, tab, *, nb, ncb, ctx_row, which):
    R, D = x.shape
    nt = R // ROW_BLOCK
    pos3 = pos.reshape(nt, ROW_BLOCK, TOP_K).transpose(0, 2, 1)
    kern = functools.partial(_combine_kernel, nb=nb, ncb=ncb, ctx_row=ctx_row, which=which)
    return pl.pallas_call(
        kern, grid=(nt,),
        in_specs=[pl.BlockSpec((1, TOP_K, ROW_BLOCK), lambda i: (i, 0, 0), memory_space=pltpu.SMEM),
                  pl.BlockSpec(memory_space=pl.ANY),
                  pl.BlockSpec((ROW_BLOCK, D), lambda i: (i, 0)),
                  pl.BlockSpec(tab.shape, lambda i: (0, 0))],
        out_specs=pl.BlockSpec((ROW_BLOCK, D), lambda i: (i, 0)),
        out_shape=jax.ShapeDtypeStruct((R, D), F32),
        scratch_shapes=[pltpu.VMEM((ROW_BLOCK, D), F32), pltpu.VMEM((ROW_BLOCK, D), F32),
                        pltpu.SemaphoreType.DMA((2,))],
        compiler_params=_cparams(("arbitrary",)), name="moe_combine")(pos3, ys, x, tab)


def _route(logits):
    R = logits.shape[0]
    E = N_EXPERTS
    top_v, top_i = lax.top_k(logits, TOP_K)
    gate = jax.nn.softmax(top_v, axis=-1)
    onehot = jnp.sum((top_i[:, :, None] == jnp.arange(E)[None, None, :]).astype(jnp.int32), axis=1)
    before = jnp.cumsum(onehot, axis=0) - onehot
    sizes = jnp.sum(onehot, axis=0)
    padded = ((sizes + MOE_TILE - 1) // MOE_TILE) * MOE_TILE
    ends = jnp.cumsum(padded)
    off = ends - padded
    pos = off[top_i] + jnp.take_along_axis(before, top_i, axis=1)
    nt = (TOP_K * R) // MOE_TILE + E
    n = nt * MOE_TILE
    flat = pos.reshape(-1)
    src_tok = jnp.zeros((n,), jnp.int32).at[flat].set(jnp.repeat(jnp.arange(R, dtype=jnp.int32), TOP_K))
    gate_sorted = jnp.zeros((n,), F32).at[flat].set(gate.reshape(-1))
    tile_start = jnp.arange(nt, dtype=jnp.int32) * MOE_TILE
    tile_valid = (tile_start < ends[-1]).astype(jnp.int32)
    tile_e = jnp.searchsorted(ends, tile_start, side="right").astype(jnp.int32)
    last_e = jnp.max(jnp.where(tile_valid == 1, tile_e, 0))
    tile_e = jnp.where(tile_valid == 1, tile_e, last_e)
    return pos.astype(jnp.int32), src_tok, gate_sorted, tile_e, tile_valid


def _prep_w_in(w):
    D = w.shape[0]
    z = lambda n: jnp.zeros((D, n), w.dtype)
    cols = [w[:, 1728:3776], w[:, 704:1728], w[:, 3792:4816], w[:, 0:704], z(64),
            w[:, 3776:3792], z(LANES - 16 + LANES)]
    out = jnp.concatenate(cols, axis=1).astype(BF16)
    assert out.shape[1] == P_WIDTH
    return out


def _prep_w_uq(w):
    r = w.shape[0]
    w = w.reshape(r, MLA_HEADS, MLA_NOPE + MLA_ROPE)
    w = jnp.pad(w, ((0, 0), (0, 0), (0, 2 * LANES - (MLA_NOPE + MLA_ROPE))))
    return w.reshape(r, MLA_HEADS * 2 * LANES).astype(BF16)


def _prep_w_ukv(w):
    r = w.shape[0]
    w = w.reshape(r, MLA_HEADS, MLA_NOPE + MLA_V)
    return jnp.concatenate([w[:, :, :MLA_NOPE].reshape(r, -1), w[:, :, MLA_NOPE:].reshape(r, -1)],
                           axis=1).astype(BF16)


def kernel(x, c, ctx, c_ctx, norm1_g, norm2_g, w_mod, b_mod, w_in, mla_q_norm_g, mla_kv_norm_g,
           mla_w_uq, mla_w_ukv, gqa_q_norm_g, gqa_k_norm_g, gdn_conv_w, gdn_a_log, gdn_dt_bias,
           gdn_norm_g, swa_sink, w_out, ffn_w1, ffn_w3, ffn_w2, moe_router, moe_w1, moe_w3, moe_w2,
           final_norm_g):
    Bn, S, D = x.shape
    C = ctx.shape[1]
    T = C + S
    R = Bn * T
    depth = w_in.shape[0]
    nbt = T // ROW_BLOCK
    ncb = C // ROW_BLOCK
    nbl = S // ROW_BLOCK

    m_rows = ((Bn + 1 + 7) // 8) * 8
    cmat = jnp.zeros((m_rows, D), F32).at[:Bn].set(c).at[Bn].set(c_ctx)
    mods = _mods(cmat, w_mod, b_mod).reshape(depth, m_rows * 6, D)

    rope_mla = _rope_tables(S, C, MLA_ROPE)
    rope_head = _rope_tables(S, C, HEAD_DIM)

    xc = jnp.concatenate([ctx, x], axis=1).reshape(R, D)
    lay = dict(nb=nbt, ncb=ncb, ctx_row=Bn)

    for layer in range(depth):
        with_ctx = layer < depth - 1
        tab = mods[layer]
        h = _norm_mod(xc, norm1_g[layer], tab, which=0, **lay)
        P = _mm_plain(h, _prep_w_in(w_in[layer]), F32)
        P3 = P.reshape(Bn, T, P_WIDTH)

        qa, ka, va = _mla_prep(P, mla_q_norm_g[layer].reshape(1, -1), mla_kv_norm_g[layer].reshape(1, -1),
                               _prep_w_uq(mla_w_uq[layer]), _prep_w_ukv(mla_w_ukv[layer]), rope_mla, nbt)
        ya = _attention(qa.reshape(Bn, T, -1), ka.reshape(Bn, T, -1), va.reshape(Bn, T, -1),
                        Hk=MLA_HEADS, G=1, dq=2 * LANES, dv=MLA_V, q_col0=0, k_col0=0, v_col0=0,
                        C=C, with_ctx=with_ctx)
        qkv_b = _head_prep(P, P_GQA, gqa_q_norm_g[layer].reshape(1, -1), gqa_k_norm_g[layer].reshape(1, -1),
                           rope_head, nbt, norm=True, n_q=GQA_HEADS, n_k=GQA_KV_HEADS).reshape(Bn, T, -1)
        Gb = GQA_HEADS // GQA_KV_HEADS
        yb = _attention(qkv_b, qkv_b, qkv_b, Hk=GQA_KV_HEADS, G=Gb, dq=HEAD_DIM, dv=HEAD_DIM,
                        q_col0=0, k_col0=GQA_HEADS * HEAD_DIM,
                        v_col0=(GQA_HEADS + GQA_KV_HEADS) * HEAD_DIM, C=C, with_ctx=with_ctx)
        conv_w8 = jnp.pad(gdn_conv_w[layer], ((0, 8 - GDN_CONV), (0, 0)))
        qkv_c = _gdn_prep(P3, conv_w8, C)
        par = jnp.zeros((8, LANES), F32)
        par = par.at[0, :2 * GDN_HEADS].set(-jnp.exp(gdn_a_log[layer].reshape(-1)))
        par = par.at[1, :2 * GDN_HEADS].set(gdn_dt_bias[layer].reshape(-1))
        of, ob = _gdn_chunks(qkv_c, P3, par, C)
        yc = _gdn_post(of.reshape(R, -1), ob.reshape(R, -1), P, gdn_norm_g[layer].reshape(1, -1))
        qkv_d = _head_prep(P, P_SWA, gqa_q_norm_g[layer].reshape(1, -1), gqa_k_norm_g[layer].reshape(1, -1),
                           rope_head, nbt, norm=False, n_q=SWA_HEADS, n_k=SWA_KV_HEADS).reshape(Bn, T, -1)
        yd = _swa_attention(qkv_d, swa_sink[layer], C=C, with_ctx=with_ctx)

        mixed = [ya.reshape(-1, GROUP_WIDTH), yb.reshape(-1, GROUP_WIDTH), yc,
                 yd.reshape(-1, GROUP_WIDTH)]
        w_o = w_out[layer].astype(BF16)
        if with_ctx:
            xc = _mm_res(mixed, w_o, xc, tab, which=2, tm=ROW_BLOCK, tn=D, **lay)
        else:
            lay = dict(nb=nbl, ncb=0, ctx_row=Bn)
            xc = _mm_res(mixed, w_o, xc, tab, which=2, tm=ROW_BLOCK, tn=D, **lay,
                         in_row_map=lambda i: (i // nbl) * nbt + ncb + i % nbl, out_rows=Bn * S)

        i2 = layer // 2
        if layer % 2 == 0:
            h2 = _norm_mod(xc, norm2_g[layer], tab, which=3, **lay)
            hid = _mm_swiglu(h2, ffn_w1[i2].astype(BF16), ffn_w3[i2].astype(BF16))
            xc = _mm_res([hid], ffn_w2[i2].astype(BF16), xc, tab, which=5, tm=512, tn=512, **lay)
        else:
            router_p = jnp.pad(moe_router[i2], ((0, 0), (0, LANES - N_EXPERTS)))
            h2, logits = _norm_mod(xc, norm2_g[layer], tab, which=3, router=router_p, **lay)
            pos, src_tok, gate_sorted, tile_e, tile_valid = _route(logits[:, :N_EXPERTS])
            xs = _gather_rows(h2, src_tok)
            ys = _gmm(xs, moe_w1[i2].astype(BF16), moe_w3[i2].astype(BF16), moe_w2[i2].astype(BF16),
                      gate_sorted, tile_e, tile_valid)
            xc = _combine(ys, pos, xc, tab, which=5, **lay)

    return _final_norm(xc, final_norm_g).reshape(Bn, S, D)
```

```python
import functools

import numpy as np
import jax
import jax.numpy as jnp
from jax import lax
from jax.experimental import pallas as pl
from jax.experimental.pallas import tpu as pltpu

F32 = jnp.float32
BF16 = jnp.bfloat16
HIGHEST = lax.Precision.HIGHEST

GRID_W = 64
HEAD_DIM = 128
ROPE_THETA = 10000.0
NORM_EPS = 1e-6
NEG_INF = -1e30
MLA_HEADS = 4
MLA_NOPE = 128
MLA_ROPE = 64
MLA_V = 128
MLA_Q_RANK = 384
MLA_KV_RANK = 256
GQA_HEADS = 4
GQA_KV_HEADS = 2
GDN_HEADS = 4
GDN_CONV = 5
GDN_CHUNK = 64
SWA_HEADS = 4
SWA_KV_HEADS = 2
SWA_WINDOW = 128
N_EXPERTS = 8
TOP_K = 2
GROUP_WIDTH = 512

ROW_BLOCK = 256
LANES = 128
MOE_TILE = 512
VMEM_LIMIT = 56 * 1024 * 1024

P_GDN = 0
P_GQA = 2048
P_SWA = 3072
P_MLA = 4096
P_AB = 4864
P_WIDTH = 5120


def _cparams(sem, vmem=VMEM_LIMIT):
    return pltpu.CompilerParams(dimension_semantics=sem, vmem_limit_bytes=vmem)


def _table_row(gi, nb, ncb, ctx_row):
    b = gi // nb
    r = gi - b * nb
    return jnp.where(r < ncb, ctx_row, b)


def _mods_kernel(c_ref, w_ref, b_ref, o_ref):
    c = c_ref[...]
    a = (c * jax.nn.sigmoid(c)).astype(BF16)
    o_ref[0] = jnp.dot(a, w_ref[0].astype(BF16), preferred_element_type=F32) + b_ref[0]


def _mods(cmat, w_mod, b_mod):
    L, D, N = w_mod.shape
    M = cmat.shape[0]
    tn = 1024
    return pl.pallas_call(
        _mods_kernel,
        grid=(L, N // tn),
        in_specs=[pl.BlockSpec((M, D), lambda l, j: (0, 0)),
                  pl.BlockSpec((1, D, tn), lambda l, j: (l, 0, j)),
                  pl.BlockSpec((1, 1, tn), lambda l, j: (l, 0, j))],
        out_specs=pl.BlockSpec((1, M, tn), lambda l, j: (l, 0, j)),
        out_shape=jax.ShapeDtypeStruct((L, M, N), F32),
        compiler_params=_cparams(("parallel", "parallel")),
        name="mods",
    )(cmat, w_mod, b_mod.reshape(L, 1, N))


def _norm_mod_kernel(x_ref, g_ref, tab_ref, *rest, nb, ncb, ctx_row, which, router):
    idx = _table_row(pl.program_id(0), nb, ncb, ctx_row)
    x = x_ref[...]
    y = x * lax.rsqrt(jnp.mean(x * x, axis=-1, keepdims=True) + NORM_EPS) * g_ref[...]
    shift = tab_ref[pl.ds(idx * 6 + which, 1), :]
    scale = tab_ref[pl.ds(idx * 6 + which + 1, 1), :]
    h = y * (1.0 + scale) + shift
    if router:
        r_ref, h_ref, l_ref = rest
        h_ref[...] = h
        l_ref[...] = jnp.dot(h, r_ref[...], preferred_element_type=F32, precision=HIGHEST)
    else:
        (h_ref,) = rest
        h_ref[...] = h.astype(h_ref.dtype)


def _norm_mod(x, g, tab, *, nb, ncb, ctx_row, which, router=None):
    R, D = x.shape
    kern = functools.partial(_norm_mod_kernel, nb=nb, ncb=ncb, ctx_row=ctx_row, which=which,
                             router=router is not None)
    in_specs = [pl.BlockSpec((ROW_BLOCK, D), lambda i: (i, 0)),
                pl.BlockSpec((1, D), lambda i: (0, 0)),
                pl.BlockSpec(tab.shape, lambda i: (0, 0))]
    args = [x, g.reshape(1, D), tab]
    if router is None:
        out_specs = pl.BlockSpec((ROW_BLOCK, D), lambda i: (i, 0))
        out_shape = jax.ShapeDtypeStruct((R, D), BF16)
    else:
        in_specs.append(pl.BlockSpec(router.shape, lambda i: (0, 0)))
        args.append(router)
        out_specs = [pl.BlockSpec((ROW_BLOCK, D), lambda i: (i, 0)),
                     pl.BlockSpec((ROW_BLOCK, LANES), lambda i: (i, 0))]
        out_shape = [jax.ShapeDtypeStruct((R, D), F32), jax.ShapeDtypeStruct((R, LANES), F32)]
    return pl.pallas_call(kern, grid=(R // ROW_BLOCK,), in_specs=in_specs, out_specs=out_specs,
                          out_shape=out_shape, compiler_params=_cparams(("parallel",)),
                          name="norm_mod")(*args)


def _final_norm_kernel(x_ref, g_ref, o_ref):
    x = x_ref[...]
    o_ref[...] = x * lax.rsqrt(jnp.mean(x * x, axis=-1, keepdims=True) + NORM_EPS) * g_ref[...]


def _final_norm(x, g):
    R, D = x.shape
    return pl.pallas_call(
        _final_norm_kernel, grid=(R // ROW_BLOCK,),
        in_specs=[pl.BlockSpec((ROW_BLOCK, D), lambda i: (i, 0)),
                  pl.BlockSpec((1, D), lambda i: (0, 0))],
        out_specs=pl.BlockSpec((ROW_BLOCK, D), lambda i: (i, 0)),
        out_shape=jax.ShapeDtypeStruct((R, D), F32),
        compiler_params=_cparams(("parallel",)), name="final_norm")(x, g.reshape(1, D))


def _pick_tile(n, options):
    for t in options:
        if n % t == 0:
            return t
    raise ValueError(f"no tile for {n}")


def _mm_plain_kernel(a_ref, w_ref, o_ref):
    o_ref[...] = jnp.dot(a_ref[...], w_ref[...], preferred_element_type=F32).astype(o_ref.dtype)


def _mm_plain(a, w, out_dtype):
    M, K = a.shape
    N = w.shape[1]
    tm = _pick_tile(M, (1024, 512, 256))
    tn = _pick_tile(N, (1024, 512, 256, 128))
    return pl.pallas_call(
        _mm_plain_kernel, grid=(M // tm, N // tn),
        in_specs=[pl.BlockSpec((tm, K), lambda i, j: (i, 0)),
                  pl.BlockSpec((K, tn), lambda i, j: (0, j))],
        out_specs=pl.BlockSpec((tm, tn), lambda i, j: (i, j)),
        out_shape=jax.ShapeDtypeStruct((M, N), out_dtype),
        compiler_params=_cparams(("parallel", "parallel")), name="mm_plain")(a, w)


def _mm_swiglu_kernel(a_ref, w1_ref, w3_ref, o_ref):
    a = a_ref[...]
    h1 = jnp.dot(a, w1_ref[...], preferred_element_type=F32)
    h3 = jnp.dot(a, w3_ref[...], preferred_element_type=F32)
    o_ref[...] = (h1 * jax.nn.sigmoid(h1) * h3).astype(o_ref.dtype)


def _mm_swiglu(a, w1, w3):
    M, K = a.shape
    N = w1.shape[1]
    tm = _pick_tile(M, (1024, 512, 256))
    tn = _pick_tile(N, (512, 256, 128))
    return pl.pallas_call(
        _mm_swiglu_kernel, grid=(M // tm, N // tn),
        in_specs=[pl.BlockSpec((tm, K), lambda i, j: (i, 0)),
                  pl.BlockSpec((K, tn), lambda i, j: (0, j)),
                  pl.BlockSpec((K, tn), lambda i, j: (0, j))],
        out_specs=pl.BlockSpec((tm, tn), lambda i, j: (i, j)),
        out_shape=jax.ShapeDtypeStruct((M, N), BF16),
        compiler_params=_cparams(("parallel", "parallel")), name="mm_swiglu")(a, w1, w3)


def _mm_res_kernel(*refs, n_a, tm, tn, nb, ncb, ctx_row, which):
    a_refs = refs[:n_a]
    w_ref, res_ref, tab_ref, o_ref = refs[n_a:]
    i = pl.program_id(0)
    j = pl.program_id(1)
    acc = None
    k0 = 0
    for a_ref in a_refs:
        kk = a_ref.shape[1]
        part = jnp.dot(a_ref[...], w_ref[k0:k0 + kk, :], preferred_element_type=F32)
        acc = part if acc is None else acc + part
        k0 += kk
    for c in range(tm // ROW_BLOCK):
        idx = _table_row(i * (tm // ROW_BLOCK) + c, nb, ncb, ctx_row)
        gate = tab_ref[pl.ds(idx * 6 + which, 1), pl.ds(pl.multiple_of(j * tn, tn), tn)]
        rows = slice(c * ROW_BLOCK, (c + 1) * ROW_BLOCK)
        o_ref[rows, :] = res_ref[rows, :] + gate * acc[rows, :]


def _mm_res(a_list, w, res, tab, *, nb, ncb, ctx_row, which, tm, tn,
            in_row_map=None, out_rows=None):
    K, N = w.shape
    M = res.shape[0] if out_rows is None else out_rows
    rm = (lambda i: i) if in_row_map is None else in_row_map
    kern = functools.partial(_mm_res_kernel, n_a=len(a_list), tm=tm, tn=tn, nb=nb, ncb=ncb,
                             ctx_row=ctx_row, which=which)

    def a_spec(a):
        if a.shape[0] == res.shape[0]:
            return pl.BlockSpec((tm, a.shape[1]), lambda i, j: (rm(i), 0))
        assert a.shape[0] == M
        return pl.BlockSpec((tm, a.shape[1]), lambda i, j: (i, 0))

    in_specs = [a_spec(a) for a in a_list]
    in_specs += [pl.BlockSpec((K, tn), lambda i, j: (0, j)),
                 pl.BlockSpec((tm, tn), lambda i, j: (rm(i), j)),
                 pl.BlockSpec(tab.shape, lambda i, j: (0, 0))]
    return pl.pallas_call(
        kern, grid=(M // tm, N // tn), in_specs=in_specs,
        out_specs=pl.BlockSpec((tm, tn), lambda i, j: (i, j)),
        out_shape=jax.ShapeDtypeStruct((M, N), F32),
        compiler_params=_cparams(("parallel", "parallel")), name="mm_res")(*a_list, w, res, tab)


def _rope_lanes(x, cos, sa, sb, quarter):
    left = pltpu.roll(x, LANES - quarter, axis=1)
    right = pltpu.roll(x, quarter, axis=1)
    return x * cos + left * sa + right * sb


def _rope_tables(S, C, rot_dim):
    rows = S // GRID_W
    row = jnp.repeat(jnp.arange(rows), GRID_W).astype(F32)
    col = jnp.tile(jnp.arange(GRID_W), rows).astype(F32)
    half = rot_dim // 2
    inv_freq = ROPE_THETA ** (-jnp.arange(0, half, 2, dtype=F32) / half)
    ang_r = row[:, None] * inv_freq
    ang_c = col[:, None] * inv_freq
    ang = jnp.concatenate([ang_r, ang_r, ang_c, ang_c], axis=-1)
    cos, sin = jnp.cos(ang), jnp.sin(ang)
    quarter = rot_dim // 4
    even = ((jnp.arange(rot_dim) // quarter) % 2) == 0
    sa = jnp.where(even, -sin, 0.0)
    sb = jnp.where(even, 0.0, sin)
    pad = LANES - rot_dim

    def full(t, ctx_val):
        t = jnp.pad(t, ((0, 0), (0, pad)), constant_values=ctx_val)
        return jnp.concatenate([jnp.full((C, LANES), ctx_val, F32), t], axis=0)

    return full(cos, 1.0), full(sa, 0.0), full(sb, 0.0)


def _mla_prep_kernel(p_ref, qg_ref, kvg_ref, wq_ref, wkv_ref, cos_ref, sa_ref, sb_ref,
                     q_ref, k_ref, v_ref, *, scale):
    cos, sa, sb = cos_ref[...], sa_ref[...], sb_ref[...]
    cq = p_ref[:, 0:MLA_Q_RANK]
    qn = cq * lax.rsqrt(jnp.mean(cq * cq, axis=-1, keepdims=True) + NORM_EPS) * qg_ref[...]
    q_raw = jnp.dot(qn.astype(BF16), wq_ref[...], preferred_element_type=F32)
    ckv = p_ref[:, MLA_Q_RANK:MLA_Q_RANK + MLA_KV_RANK]
    kvn = ckv * lax.rsqrt(jnp.mean(ckv * ckv, axis=-1, keepdims=True) + NORM_EPS) * kvg_ref[...]
    kv = jnp.dot(kvn.astype(BF16), wkv_ref[...], preferred_element_type=F32)
    kr = p_ref[:, MLA_Q_RANK + MLA_KV_RANK:MLA_Q_RANK + MLA_KV_RANK + LANES]
    kr = _rope_lanes(kr, cos, sa, sb, MLA_ROPE // 4).astype(BF16)
    for h in range(MLA_HEADS):
        c0 = 2 * LANES * h
        q_ref[:, c0:c0 + LANES] = (q_raw[:, c0:c0 + LANES] * scale).astype(BF16)
        qr = _rope_lanes(q_raw[:, c0 + LANES:c0 + 2 * LANES], cos, sa, sb, MLA_ROPE // 4)
        q_ref[:, c0 + LANES:c0 + 2 * LANES] = (qr * scale).astype(BF16)
        k_ref[:, c0:c0 + LANES] = kv[:, LANES * h:LANES * (h + 1)].astype(BF16)
        k_ref[:, c0 + LANES:c0 + 2 * LANES] = kr
    v_ref[...] = kv[:, MLA_HEADS * LANES:].astype(BF16)


def _mla_prep(P, qg, kvg, wq_p, wkv_p, tabs, nbt):
    R = P.shape[0]
    width = MLA_HEADS * 2 * LANES
    kern = functools.partial(_mla_prep_kernel, scale=float((MLA_NOPE + MLA_ROPE) ** -0.5))
    tab_spec = pl.BlockSpec((ROW_BLOCK, LANES), lambda i: (i % nbt, 0))
    const = lambda a: pl.BlockSpec(a.shape, lambda i: (0, 0))
    return pl.pallas_call(
        kern, grid=(R // ROW_BLOCK,),
        in_specs=[pl.BlockSpec((ROW_BLOCK, 1024), lambda i: (i, P_MLA // 1024)),
                  const(qg), const(kvg), const(wq_p), const(wkv_p), tab_spec, tab_spec, tab_spec],
        out_specs=[pl.BlockSpec((ROW_BLOCK, width), lambda i: (i, 0)),
                   pl.BlockSpec((ROW_BLOCK, width), lambda i: (i, 0)),
                   pl.BlockSpec((ROW_BLOCK, MLA_HEADS * MLA_V), lambda i: (i, 0))],
        out_shape=[jax.ShapeDtypeStruct((R, width), BF16), jax.ShapeDtypeStruct((R, width), BF16),
                   jax.ShapeDtypeStruct((R, MLA_HEADS * MLA_V), BF16)],
        compiler_params=_cparams(("parallel",)), name="mla_prep")(P, qg, kvg, wq_p, wkv_p, *tabs)


def _head_prep_kernel(p_ref, qg_ref, kg_ref, cos_ref, sa_ref, sb_ref, o_ref, *, norm, scale,
                      n_q, n_k):
    cos, sa, sb = cos_ref[...], sa_ref[...], sb_ref[...]
    for h in range(n_q + n_k):
        x = p_ref[:, LANES * h:LANES * (h + 1)]
        if norm:
            g = qg_ref[...] if h < n_q else kg_ref[...]
            x = x * lax.rsqrt(jnp.mean(x * x, axis=-1, keepdims=True) + NORM_EPS) * g
        x = _rope_lanes(x, cos, sa, sb, HEAD_DIM // 4)
        if h < n_q:
            x = x * scale
        o_ref[:, LANES * h:LANES * (h + 1)] = x.astype(BF16)
    v0 = LANES * (n_q + n_k)
    o_ref[:, v0:] = p_ref[:, v0:].astype(BF16)


def _head_prep(P, col0, qg, kg, tabs, nbt, *, norm, n_q, n_k):
    R = P.shape[0]
    kern = functools.partial(_head_prep_kernel, norm=norm, scale=float(HEAD_DIM ** -0.5),
                             n_q=n_q, n_k=n_k)
    tab_spec = pl.BlockSpec((ROW_BLOCK, LANES), lambda i: (i % nbt, 0))
    const = lambda a: pl.BlockSpec(a.shape, lambda i: (0, 0))
    return pl.pallas_call(
        kern, grid=(R // ROW_BLOCK,),
        in_specs=[pl.BlockSpec((ROW_BLOCK, 1024), lambda i: (i, col0 // 1024)),
                  const(qg), const(kg), tab_spec, tab_spec, tab_spec],
        out_specs=pl.BlockSpec((ROW_BLOCK, 1024), lambda i: (i, 0)),
        out_shape=jax.ShapeDtypeStruct((R, 1024), BF16),
        compiler_params=_cparams(("parallel",)), name="head_prep")(P, qg, kg, *tabs)


def _attn_kernel(q_ref, k_ref, v_ref, o_ref, *, G, dq, dv, C, T, ctx_tile):
    def run(nk):
        k = k_ref[0, :nk, :]
        v = v_ref[0, :nk, :]
        for g in range(G):
            q = q_ref[0, :, g * dq:(g + 1) * dq]
            s = lax.dot_general(q, k, (((1,), (1,)), ((), ())), preferred_element_type=F32)
            m = jnp.max(s, axis=-1, keepdims=True)
            p = jnp.exp(s - m)
            l = jnp.sum(p, axis=-1, keepdims=True)
            o = jnp.dot(p.astype(BF16), v, preferred_element_type=F32) / l
            o_ref[0, :, g * dv:(g + 1) * dv] = o.astype(o_ref.dtype)

    if ctx_tile:
        qi = pl.program_id(2)
        pl.when(qi == 0)(lambda: run(C))
        pl.when(qi > 0)(lambda: run(T))
    else:
        run(T)


def _attention(q, k, v, *, Hk, G, dq, dv, q_col0, k_col0, v_col0, C, with_ctx):
    Bn, T, _ = q.shape
    ncb = C // ROW_BLOCK
    assert ncb == 1
    nq = T // ROW_BLOCK if with_ctx else (T - C) // ROW_BLOCK
    qoff = 0 if with_ctx else ncb
    kern = functools.partial(_attn_kernel, G=G, dq=dq, dv=dv, C=C, T=T, ctx_tile=with_ctx)
    return pl.pallas_call(
        kern, grid=(Bn, Hk, nq),
        in_specs=[pl.BlockSpec((1, ROW_BLOCK, G * dq), lambda b, h, i: (b, i + qoff, q_col0 // (G * dq) + h)),
                  pl.BlockSpec((1, T, dq), lambda b, h, i: (b, 0, k_col0 // dq + h)),
                  pl.BlockSpec((1, T, dv), lambda b, h, i: (b, 0, v_col0 // dv + h))],
        out_specs=pl.BlockSpec((1, ROW_BLOCK, G * dv), lambda b, h, i: (b, i, h)),
        out_shape=jax.ShapeDtypeStruct((Bn, nq * ROW_BLOCK, Hk * G * dv), BF16),
        compiler_params=_cparams(("parallel", "parallel", "parallel")), name="attention")(q, k, v)


def _swa_attention(qkv, sink, *, C, with_ctx):
    Bn, T, _ = qkv.shape
    G = SWA_HEADS // SWA_KV_HEADS
    d = HEAD_DIM
    ncb = C // ROW_BLOCK
    assert ncb == 1 and T - C >= ROW_BLOCK + 2 * SWA_WINDOW
    nq = T // ROW_BLOCK if with_ctx else (T - C) // ROW_BLOCK
    qoff = 0 if with_ctx else ncb
    kern = functools.partial(_swa_kernel, G=G, C=C, S=T - C, ctx_tile=with_ctx, qoff=qoff)
    return pl.pallas_call(
        kern, grid=(Bn, SWA_KV_HEADS, nq),
        in_specs=[pl.BlockSpec(memory_space=pltpu.SMEM),
                  pl.BlockSpec((1, ROW_BLOCK, G * d), lambda b, h, i: (b, i + qoff, h)),
                  pl.BlockSpec((1, T, d), lambda b, h, i: (b, 0, SWA_HEADS + h)),
                  pl.BlockSpec((1, T, d), lambda b, h, i: (b, 0, SWA_HEADS + SWA_KV_HEADS + h))],
        out_specs=pl.BlockSpec((1, ROW_BLOCK, G * d), lambda b, h, i: (b, i, h)),
        out_shape=jax.ShapeDtypeStruct((Bn, nq * ROW_BLOCK, SWA_HEADS * d), BF16),
        compiler_params=_cparams(("parallel", "parallel", "parallel")), name="swa")(
            sink, qkv, qkv, qkv)


def _swa_kernel(sink_ref, q_ref, k_ref, v_ref, o_ref, *, G, C, S, ctx_tile, qoff):
    qi = pl.program_id(2) + qoff
    d = HEAD_DIM
    W = SWA_WINDOW
    WIN = ROW_BLOCK + 2 * W
    hk = pl.program_id(1)
    kc = k_ref[0, :C, :]
    vc = v_ref[0, :C, :]
    nt = (((1,), (1,)), ((), ()))

    def finish(g, parts):
        sink = sink_ref[hk * G + g]
        m = jnp.full((ROW_BLOCK, 1), sink, F32)
        for s, _ in parts:
            m = jnp.maximum(m, jnp.max(s, axis=-1, keepdims=True))
        den = jnp.exp(sink - m)
        o = None
        for s, v in parts:
            e = jnp.exp(s - m)
            den = den + jnp.sum(e, axis=-1, keepdims=True)
            pv = jnp.dot(e.astype(BF16), v, preferred_element_type=F32)
            o = pv if o is None else o + pv
        o_ref[0, :, g * d:(g + 1) * d] = (o / den).astype(o_ref.dtype)

    def run_ctx():
        for g in range(G):
            q = q_ref[0, :, g * d:(g + 1) * d]
            finish(g, [(lax.dot_general(q, kc, nt, preferred_element_type=F32), vc)])

    def run_lat():
        q0 = (qi - (C // ROW_BLOCK)) * ROW_BLOCK
        ws = jnp.clip(q0 - W, 0, S - WIN)
        start = pl.multiple_of(C + ws, LANES)
        kw = k_ref[0, pl.ds(start, WIN), :]
        vw = v_ref[0, pl.ds(start, WIN), :]
        qpos = q0 + lax.broadcasted_iota(jnp.int32, (ROW_BLOCK, WIN), 0)
        kpos = ws + lax.broadcasted_iota(jnp.int32, (ROW_BLOCK, WIN), 1)
        valid = jnp.abs(kpos - qpos) <= W
        for g in range(G):
            q = q_ref[0, :, g * d:(g + 1) * d]
            s_loc = jnp.where(valid, lax.dot_general(q, kw, nt, preferred_element_type=F32), NEG_INF)
            s_ctx = lax.dot_general(q, kc, nt, preferred_element_type=F32)
            finish(g, [(s_loc, vw), (s_ctx, vc)])

    if ctx_tile:
        pl.when(qi == 0)(run_ctx)
        pl.when(qi > 0)(run_lat)
    else:
        run_lat()


def _gdn_prep_kernel(x_ref, w_ref, o_ref, *, C, T):
    j = pl.program_id(1)
    x = x_ref[0]
    t = lax.broadcasted_iota(jnp.int32, (T, 1), 0)
    is_lat = t >= C
    lo = jnp.where(is_lat, C, 0)
    hi = jnp.where(is_lat, T, C)
    acc = None
    half = (GDN_CONV - 1) // 2
    for tap in range(GDN_CONV):
        s = tap - half
        xs = x if s == 0 else pltpu.roll(x, (-s) % T, axis=0)
        ok = jnp.logical_and(t + s >= lo, t + s < hi)
        term = jnp.where(ok, xs, 0.0) * w_ref[tap:tap + 1, :]
        acc = term if acc is None else acc + term
    y = acc * jax.nn.sigmoid(acc)
    inv = lax.rsqrt(jnp.sum(y * y, axis=-1, keepdims=True) + 1e-6)
    nh = GDN_HEADS
    f = jnp.where(j < 2 * nh, inv, 1.0) * jnp.where(j < nh, float(HEAD_DIM ** -0.5), 1.0)
    o_ref[0] = y * f


def _gdn_prep(P3, conv_w8, C):
    Bn, T, _ = P3.shape
    nblk = 3 * GDN_HEADS
    kern = functools.partial(_gdn_prep_kernel, C=C, T=T)
    return pl.pallas_call(
        kern, grid=(Bn, nblk),
        in_specs=[pl.BlockSpec((1, T, LANES), lambda b, j: (b, 0, P_GDN // LANES + j)),
                  pl.BlockSpec((8, LANES), lambda b, j: (0, j))],
        out_specs=pl.BlockSpec((1, T, LANES), lambda b, j: (b, 0, j)),
        out_shape=jax.ShapeDtypeStruct((Bn, T, nblk * LANES), F32),
        compiler_params=_cparams(("parallel", "parallel")), name="gdn_prep")(P3, conv_w8)


_GDN_INV_BASE = 8


def _dot16(a, b):
    return jnp.dot(a, b, preferred_element_type=F32)


def _dot16_nt(a, b):
    return lax.dot_general(a, b, (((1,), (1,)), ((), ())), preferred_element_type=F32)


def _dot16_tn(a, b):
    return lax.dot_general(a, b, (((0,), (0,)), ((), ())), preferred_element_type=F32)


def _dot32(a, b):
    return jnp.dot(a, b, preferred_element_type=F32, precision=HIGHEST)


def _dot32_nt(a, b):
    return lax.dot_general(a, b, (((1,), (1,)), ((), ())), preferred_element_type=F32,
                           precision=HIGHEST)


def _dot32_tn(a, b):
    return lax.dot_general(a, b, (((0,), (0,)), ((), ())), preferred_element_type=F32,
                           precision=HIGHEST)


def _gdn_chunk_kernel(qf_ref, qb_ref, abf_ref, abb_ref, par_ref, of_ref, ob_ref, s_ref):
    Cc = GDN_CHUNK
    H = GDN_HEADS
    d = HEAD_DIM

    @pl.when(pl.program_id(1) == 0)
    def _():
        s_ref[...] = jnp.zeros_like(s_ref)

    ri = lax.broadcasted_iota(jnp.int32, (Cc, Cc), 0)
    ci = lax.broadcasted_iota(jnp.int32, (Cc, Cc), 1)
    eye = (ri == ci).astype(F32)
    neg_a = par_ref[0:1, :]
    dt_b = par_ref[1:2, :]
    base = _GDN_INV_BASE
    diag_blk = (ri // base) == (ci // base)
    off_blks = []
    s = base
    while s < Cc:
        off_blks.append(jnp.logical_and((ri // (2 * s)) == (ci // (2 * s)), (ri // s) != (ci // s)))
        s *= 2

    dirs = []
    for direction in range(2):
        ab = (abf_ref if direction == 0 else abb_ref)[0]
        if direction == 0:
            incl, incl_t, strict, last = ri >= ci, ri <= ci, ri > ci, Cc - 1
        else:
            incl, incl_t, strict, last = ri <= ci, ri >= ci, ri < ci, 0
        z = ab + dt_b
        sp = jnp.maximum(z, 0.0) + jnp.log1p(jnp.exp(-jnp.abs(z)))
        gl_all = neg_a * sp
        gc_all = _dot32(incl.astype(F32), gl_all)
        g_tot = gc_all[last:last + 1, :]
        dirs.append(dict(
            x_ref=qf_ref if direction == 0 else qb_ref, o_ref=of_ref if direction == 0 else ob_ref,
            incl=incl, strict=strict, gc_all=gc_all,
            gct_all=_dot32_tn(gl_all, incl_t.astype(F32)),
            beta_all=jax.nn.sigmoid(ab),
            eg_all=jnp.exp(gc_all), ekg_all=jnp.exp(g_tot - gc_all), egt_all=jnp.exp(g_tot)))

    units = [(dirs[direction], direction * H + h, h) for direction in range(2) for h in range(H)]
    U = range(len(units))

    k16, kb, decay, a_mat = [], [], [], []
    for dd, lane, h in units:
        k = dd["x_ref"][0, :, (H + h) * d:(H + h + 1) * d]
        beta = dd["beta_all"][:, 2 * H + lane:2 * H + lane + 1]
        diff = dd["gc_all"][:, lane:lane + 1] - dd["gct_all"][lane:lane + 1, :]
        decay.append(jnp.where(dd["incl"], jnp.exp(jnp.where(dd["incl"], diff, 0.0)), 0.0))
        kb.append(k * beta)
        k16.append(k.astype(BF16))
    for i in U:
        a_mat.append(jnp.where(units[i][0]["strict"],
                               _dot16_nt(kb[i].astype(BF16), k16[i]) * decay[i], 0.0))

    npow = [jnp.where(diag_blk, a, 0.0).astype(BF16) for a in a_mat]
    t = [eye - jnp.where(diag_blk, a, 0.0) for a in a_mat]
    sq = 2 * base
    while sq > 4:
        npow = [_dot16(n, n).astype(BF16) for n in npow]
        t = [t[i] + _dot16(t[i].astype(BF16), npow[i]) for i in U]
        sq //= 2
    for off_blk in off_blks:
        t16 = [x.astype(BF16) for x in t]
        xo = [_dot16(t16[i], jnp.where(off_blk, a_mat[i], 0.0).astype(BF16)).astype(BF16) for i in U]
        t = [t[i] - _dot16(xo[i], t16[i]) for i in U]

    uw, intra, qg = [], [], []
    for i in U:
        dd, lane, h = units[i]
        q = dd["x_ref"][0, :, h * d:(h + 1) * d]
        v = dd["x_ref"][0, :, (2 * H + h) * d:(2 * H + h + 1) * d]
        beta = dd["beta_all"][:, 2 * H + lane:2 * H + lane + 1]
        eg = dd["eg_all"][:, lane:lane + 1]
        rhs = jnp.concatenate([v * beta, kb[i] * eg], axis=1).astype(BF16)
        uw.append(_dot16(t[i].astype(BF16), rhs))
        intra.append(jnp.where(dd["incl"], _dot16_nt(q.astype(BF16), k16[i]) * decay[i], 0.0)
                     .astype(BF16))
        qg.append((q * eg).astype(BF16))
    state = [s_ref[units[i][1]] for i in U]
    s16 = [x.astype(BF16) for x in state]
    v_new = [uw[i][:, :d] - _dot16(uw[i][:, d:].astype(BF16), s16[i]) for i in U]
    vn16 = [x.astype(BF16) for x in v_new]
    for i in U:
        dd, lane, h = units[i]
        dd["o_ref"][0, :, h * d:(h + 1) * d] = _dot16(qg[i], s16[i]) + _dot16(intra[i], vn16[i])
    for i in U:
        dd, lane, h = units[i]
        k = dd["x_ref"][0, :, (H + h) * d:(H + h + 1) * d]
        kg = (k * dd["ekg_all"][:, lane:lane + 1]).astype(BF16)
        s_ref[lane] = state[i] * dd["egt_all"][:, lane:lane + 1] + _dot16_tn(kg, vn16[i])


def _gdn_chunks(qkv, P3, par, C):
    Bn, T, _ = qkv.shape
    Cc = GDN_CHUNK
    nc = T // Cc
    ncc = C // Cc
    H = GDN_HEADS

    def bwd_chunk(s):
        return jnp.where(s < ncc, ncc - 1 - s, nc + ncc - 1 - s)

    ab_blk = P_AB // LANES
    return pl.pallas_call(
        _gdn_chunk_kernel, grid=(Bn, nc),
        in_specs=[pl.BlockSpec((1, Cc, 3 * H * HEAD_DIM), lambda b, s: (b, s, 0)),
                  pl.BlockSpec((1, Cc, 3 * H * HEAD_DIM), lambda b, s: (b, bwd_chunk(s), 0)),
                  pl.BlockSpec((1, Cc, LANES), lambda b, s: (b, s, ab_blk)),
                  pl.BlockSpec((1, Cc, LANES), lambda b, s: (b, bwd_chunk(s), ab_blk)),
                  pl.BlockSpec((8, LANES), lambda b, s: (0, 0))],
        out_specs=[pl.BlockSpec((1, Cc, H * HEAD_DIM), lambda b, s: (b, s, 0)),
                   pl.BlockSpec((1, Cc, H * HEAD_DIM), lambda b, s: (b, bwd_chunk(s), 0))],
        out_shape=[jax.ShapeDtypeStruct((Bn, T, H * HEAD_DIM), F32),
                   jax.ShapeDtypeStruct((Bn, T, H * HEAD_DIM), F32)],
        scratch_shapes=[pltpu.VMEM((2 * H, HEAD_DIM, HEAD_DIM), F32)],
        compiler_params=_cparams(("parallel", "arbitrary")), name="gdn_chunks")(
            qkv, qkv, P3, P3, par)


def _gdn_post_kernel(of_ref, ob_ref, z_ref, g_ref, o_ref):
    d = HEAD_DIM
    for h in range(GDN_HEADS):
        o = of_ref[:, h * d:(h + 1) * d] + ob_ref[:, h * d:(h + 1) * d]
        y = o * lax.rsqrt(jnp.mean(o * o, axis=-1, keepdims=True) + NORM_EPS) * g_ref[...]
        z = z_ref[:, h * d:(h + 1) * d]
        o_ref[:, h * d:(h + 1) * d] = (y * (z * jax.nn.sigmoid(z))).astype(o_ref.dtype)


def _gdn_post(of, ob, P, g):
    R, W = of.shape
    zblk = (P_GDN + 3 * W) // W
    return pl.pallas_call(
        _gdn_post_kernel, grid=(R // ROW_BLOCK,),
        in_specs=[pl.BlockSpec((ROW_BLOCK, W), lambda i: (i, 0)),
                  pl.BlockSpec((ROW_BLOCK, W), lambda i: (i, 0)),
                  pl.BlockSpec((ROW_BLOCK, W), lambda i: (i, zblk)),
                  pl.BlockSpec((1, HEAD_DIM), lambda i: (0, 0))],
        out_specs=pl.BlockSpec((ROW_BLOCK, W), lambda i: (i, 0)),
        out_shape=jax.ShapeDtypeStruct((R, W), BF16),
        compiler_params=_cparams(("parallel",)), name="gdn_post")(of, ob, P, g)


def _gather_rows_kernel(tok_ref, tok_next_ref, h_hbm, o_ref, buf, sem):
    i = pl.program_id(0)
    nt = pl.num_programs(0)
    n = buf.shape[1]
    slot = i % 2

    def issue(t_ref, s):
        def body(r, c):
            pltpu.make_async_copy(h_hbm.at[pl.ds(t_ref[0, 0, r], 1), :],
                                  buf.at[s, pl.ds(r, 1), :], sem.at[s]).start()
            return c
        lax.fori_loop(0, n, body, 0, unroll=8)

    @pl.when(i == 0)
    def _():
        issue(tok_ref, 0)

    @pl.when(i + 1 < nt)
    def _():
        issue(tok_next_ref, 1 - slot)

    pltpu.make_async_copy(h_hbm.at[pl.ds(0, n), :], buf.at[slot], sem.at[slot]).wait()
    o_ref[...] = buf[slot].astype(o_ref.dtype)


def _gather_rows(h, src_tok):
    D = h.shape[1]
    n = src_tok.shape[0]
    nt = n // ROW_BLOCK
    return pl.pallas_call(
        _gather_rows_kernel, grid=(nt,),
        in_specs=[pl.BlockSpec((1, 1, ROW_BLOCK), lambda i: (i, 0, 0), memory_space=pltpu.SMEM),
                  pl.BlockSpec((1, 1, ROW_BLOCK), lambda i: (jnp.minimum(i + 1, nt - 1), 0, 0),
                               memory_space=pltpu.SMEM),
                  pl.BlockSpec(memory_space=pl.ANY)],
        out_specs=pl.BlockSpec((ROW_BLOCK, D), lambda i: (i, 0)),
        out_shape=jax.ShapeDtypeStruct((n, D), BF16),
        scratch_shapes=[pltpu.VMEM((2, ROW_BLOCK, D), F32), pltpu.SemaphoreType.DMA((2,))],
        compiler_params=_cparams(("arbitrary",)), name="moe_gather")(
            src_tok.reshape(nt, 1, ROW_BLOCK), src_tok.reshape(nt, 1, ROW_BLOCK), h)


def _gmm_kernel(te_ref, tv_ref, x_ref, w1_ref, w3_ref, w2_ref, g_ref, o_ref, acc_ref):
    i = pl.program_id(0)
    f = pl.program_id(1)
    nf = pl.num_programs(1)

    @pl.when(tv_ref[i] == 1)
    def _():
        x = x_ref[...]
        h1 = jnp.dot(x, w1_ref[0], preferred_element_type=F32)
        h3 = jnp.dot(x, w3_ref[0], preferred_element_type=F32)
        hid = (h1 * jax.nn.sigmoid(h1) * h3).astype(BF16)
        part = jnp.dot(hid, w2_ref[0], preferred_element_type=F32)

        @pl.when(f == 0)
        def _():
            acc_ref[...] = part

        @pl.when(f > 0)
        def _():
            acc_ref[...] += part

        @pl.when(f == nf - 1)
        def _():
            o_ref[...] = acc_ref[...] * g_ref[...]

    @pl.when(jnp.logical_and(tv_ref[i] == 0, f == nf - 1))
    def _():
        o_ref[...] = jnp.zeros_like(o_ref)


def _gmm(xs, w1, w3, w2, gate_sorted, tile_e, tile_valid):
    n, D = xs.shape
    E, _, F = w1.shape
    tf = 1024
    nt = n // MOE_TILE
    grid_spec = pltpu.PrefetchScalarGridSpec(
        num_scalar_prefetch=2, grid=(nt, F // tf),
        in_specs=[pl.BlockSpec((MOE_TILE, D), lambda i, f, te, tv: (i, 0)),
                  pl.BlockSpec((1, D, tf), lambda i, f, te, tv: (te[i], 0, f)),
                  pl.BlockSpec((1, D, tf), lambda i, f, te, tv: (te[i], 0, f)),
                  pl.BlockSpec((1, tf, D), lambda i, f, te, tv: (te[i], f, 0)),
                  pl.BlockSpec((MOE_TILE, 1), lambda i, f, te, tv: (i, 0))],
        out_specs=pl.BlockSpec((MOE_TILE, D), lambda i, f, te, tv: (i, 0)),
        scratch_shapes=[pltpu.VMEM((MOE_TILE, D), F32)])
    return pl.pallas_call(
        _gmm_kernel, grid_spec=grid_spec,
        out_shape=jax.ShapeDtypeStruct((n, D), F32),
        compiler_params=_cparams(("parallel", "arbitrary")), name="moe_gmm")(
            tile_e, tile_valid, xs, w1, w3, w2, gate_sorted.reshape(n, 1))


def _combine_kernel(pos_ref, pos_next_ref, ys_hbm, x_ref, tab_ref, o_ref, buf, sem, *, nb, ncb,
                    ctx_row, which):
    i = pl.program_id(0)
    nt = pl.num_programs(0)
    n = buf.shape[2]
    slot = i % 2
    idx = _table_row(i, nb, ncb, ctx_row)

    def issue(p_ref, s):
        def body(r, c):
            for k in range(TOP_K):
                pltpu.make_async_copy(ys_hbm.at[pl.ds(p_ref[0, k, r], 1), :],
                                      buf.at[s, k, pl.ds(r, 1), :], sem.at[s]).start()
            return c
        lax.fori_loop(0, n, body, 0, unroll=4)

    @pl.when(i == 0)
    def _():
        issue(pos_ref, 0)

    @pl.when(i + 1 < nt)
    def _():
        issue(pos_next_ref, 1 - slot)

    for k in range(TOP_K):
        pltpu.make_async_copy(ys_hbm.at[pl.ds(0, n), :], buf.at[slot, k], sem.at[slot]).wait()
    gate = tab_ref[pl.ds(idx * 6 + which, 1), :]
    o_ref[...] = x_ref[...] + gate * (buf[slot, 0] + buf[slot, 1])


def _combine(ys, pos, x, tab, *, nb, ncb, ctx_row, which):
    R, D = x.shape
    nt = R // ROW_BLOCK
    pos3 = pos.reshape(nt, ROW_BLOCK, TOP_K).transpose(0, 2, 1)
    kern = functools.partial(_combine_kernel, nb=nb, ncb=ncb, ctx_row=ctx_row, which=which)
    return pl.pallas_call(
        kern, grid=(nt,),
        in_specs=[pl.BlockSpec((1, TOP_K, ROW_BLOCK), lambda i: (i, 0, 0), memory_space=pltpu.SMEM),
                  pl.BlockSpec((1, TOP_K, ROW_BLOCK), lambda i: (jnp.minimum(i + 1, nt - 1), 0, 0),
                               memory_space=pltpu.SMEM),
                  pl.BlockSpec(memory_space=pl.ANY),
                  pl.BlockSpec((ROW_BLOCK, D), lambda i: (i, 0)),
                  pl.BlockSpec(tab.shape, lambda i: (0, 0))],
        out_specs=pl.BlockSpec((ROW_BLOCK, D), lambda i: (i, 0)),
        out_shape=jax.ShapeDtypeStruct((R, D), F32),
        scratch_shapes=[pltpu.VMEM((2, TOP_K, ROW_BLOCK, D), F32), pltpu.SemaphoreType.DMA((2,))],
        compiler_params=_cparams(("arbitrary",)), name="moe_combine")(pos3, pos3, ys, x, tab)


def _route(logits):
    R = logits.shape[0]
    E = N_EXPERTS
    top_v, top_i = lax.top_k(logits, TOP_K)
    gate = jax.nn.softmax(top_v, axis=-1)
    onehot = jnp.sum((top_i[:, :, None] == jnp.arange(E)[None, None, :]).astype(jnp.int32), axis=1)
    before = jnp.cumsum(onehot, axis=0) - onehot
    sizes = jnp.sum(onehot, axis=0)
    padded = ((sizes + MOE_TILE - 1) // MOE_TILE) * MOE_TILE
    ends = jnp.cumsum(padded)
    off = ends - padded
    pos = off[top_i] + jnp.take_along_axis(before, top_i, axis=1)
    nt = (TOP_K * R) // MOE_TILE + E
    n = nt * MOE_TILE
    flat = pos.reshape(-1)
    src_tok = jnp.zeros((n,), jnp.int32).at[flat].set(jnp.repeat(jnp.arange(R, dtype=jnp.int32), TOP_K))
    gate_sorted = jnp.zeros((n,), F32).at[flat].set(gate.reshape(-1))
    tile_start = jnp.arange(nt, dtype=jnp.int32) * MOE_TILE
    tile_valid = (tile_start < ends[-1]).astype(jnp.int32)
    tile_e = jnp.searchsorted(ends, tile_start, side="right").astype(jnp.int32)
    last_e = jnp.max(jnp.where(tile_valid == 1, tile_e, 0))
    tile_e = jnp.where(tile_valid == 1, tile_e, last_e)
    return pos.astype(jnp.int32), src_tok, gate_sorted, tile_e, tile_valid


def _prep_w_in(w):
    D = w.shape[0]
    z = lambda n: jnp.zeros((D, n), w.dtype)
    cols = [w[:, 1728:3776], w[:, 704:1728], w[:, 3792:4816], w[:, 0:704], z(64),
            w[:, 3776:3792], z(LANES - 16 + LANES)]
    out = jnp.concatenate(cols, axis=1).astype(BF16)
    assert out.shape[1] == P_WIDTH
    return out


def _prep_w_uq(w):
    r = w.shape[0]
    w = w.reshape(r, MLA_HEADS, MLA_NOPE + MLA_ROPE)
    w = jnp.pad(w, ((0, 0), (0, 0), (0, 2 * LANES - (MLA_NOPE + MLA_ROPE))))
    return w.reshape(r, MLA_HEADS * 2 * LANES).astype(BF16)


def _prep_w_ukv(w):
    r = w.shape[0]
    w = w.reshape(r, MLA_HEADS, MLA_NOPE + MLA_V)
    return jnp.concatenate([w[:, :, :MLA_NOPE].reshape(r, -1), w[:, :, MLA_NOPE:].reshape(r, -1)],
                           axis=1).astype(BF16)


def kernel(x, c, ctx, c_ctx, norm1_g, norm2_g, w_mod, b_mod, w_in, mla_q_norm_g, mla_kv_norm_g,
           mla_w_uq, mla_w_ukv, gqa_q_norm_g, gqa_k_norm_g, gdn_conv_w, gdn_a_log, gdn_dt_bias,
           gdn_norm_g, swa_sink, w_out, ffn_w1, ffn_w3, ffn_w2, moe_router, moe_w1, moe_w3, moe_w2,
           final_norm_g):
    Bn, S, D = x.shape
    C = ctx.shape[1]
    T = C + S
    R = Bn * T
    depth = w_in.shape[0]
    nbt = T // ROW_BLOCK
    ncb = C // ROW_BLOCK
    nbl = S // ROW_BLOCK

    m_rows = ((Bn + 1 + 7) // 8) * 8
    cmat = jnp.zeros((m_rows, D), F32).at[:Bn].set(c).at[Bn].set(c_ctx)
    mods = _mods(cmat, w_mod, b_mod).reshape(depth, m_rows * 6, D)

    rope_mla = _rope_tables(S, C, MLA_ROPE)
    rope_head = _rope_tables(S, C, HEAD_DIM)

    xc = jnp.concatenate([ctx, x], axis=1).reshape(R, D)
    lay = dict(nb=nbt, ncb=ncb, ctx_row=Bn)

    for layer in range(depth):
        with_ctx = layer < depth - 1
        tab = mods[layer]
        h = _norm_mod(xc, norm1_g[layer], tab, which=0, **lay)
        P = _mm_plain(h, _prep_w_in(w_in[layer]), F32)
        P3 = P.reshape(Bn, T, P_WIDTH)

        qa, ka, va = _mla_prep(P, mla_q_norm_g[layer].reshape(1, -1), mla_kv_norm_g[layer].reshape(1, -1),
                               _prep_w_uq(mla_w_uq[layer]), _prep_w_ukv(mla_w_ukv[layer]), rope_mla, nbt)
        ya = _attention(qa.reshape(Bn, T, -1), ka.reshape(Bn, T, -1), va.reshape(Bn, T, -1),
                        Hk=MLA_HEADS, G=1, dq=2 * LANES, dv=MLA_V, q_col0=0, k_col0=0, v_col0=0,
                        C=C, with_ctx=with_ctx)
        qkv_b = _head_prep(P, P_GQA, gqa_q_norm_g[layer].reshape(1, -1), gqa_k_norm_g[layer].reshape(1, -1),
                           rope_head, nbt, norm=True, n_q=GQA_HEADS, n_k=GQA_KV_HEADS).reshape(Bn, T, -1)
        Gb = GQA_HEADS // GQA_KV_HEADS
        yb = _attention(qkv_b, qkv_b, qkv_b, Hk=GQA_KV_HEADS, G=Gb, dq=HEAD_DIM, dv=HEAD_DIM,
                        q_col0=0, k_col0=GQA_HEADS * HEAD_DIM,
                        v_col0=(GQA_HEADS + GQA_KV_HEADS) * HEAD_DIM, C=C, with_ctx=with_ctx)
        conv_w8 = jnp.pad(gdn_conv_w[layer], ((0, 8 - GDN_CONV), (0, 0)))
        qkv_c = _gdn_prep(P3, conv_w8, C)
        par = jnp.zeros((8, LANES), F32)
        par = par.at[0, :2 * GDN_HEADS].set(-jnp.exp(gdn_a_log[layer].reshape(-1)))
        par = par.at[1, :2 * GDN_HEADS].set(gdn_dt_bias[layer].reshape(-1))
        of, ob = _gdn_chunks(qkv_c, P3, par, C)
        yc = _gdn_post(of.reshape(R, -1), ob.reshape(R, -1), P, gdn_norm_g[layer].reshape(1, -1))
        qkv_d = _head_prep(P, P_SWA, gqa_q_norm_g[layer].reshape(1, -1), gqa_k_norm_g[layer].reshape(1, -1),
                           rope_head, nbt, norm=False, n_q=SWA_HEADS, n_k=SWA_KV_HEADS).reshape(Bn, T, -1)
        yd = _swa_attention(qkv_d, swa_sink[layer], C=C, with_ctx=with_ctx)

        mixed = [ya.reshape(-1, GROUP_WIDTH), yb.reshape(-1, GROUP_WIDTH), yc,
                 yd.reshape(-1, GROUP_WIDTH)]
        w_o = w_out[layer].astype(BF16)
        if with_ctx:
            xc = _mm_res(mixed, w_o, xc, tab, which=2, tm=ROW_BLOCK, tn=D, **lay)
        else:
            lay = dict(nb=nbl, ncb=0, ctx_row=Bn)
            xc = _mm_res(mixed, w_o, xc, tab, which=2, tm=ROW_BLOCK, tn=D, **lay,
                         in_row_map=lambda i: (i // nbl) * nbt + ncb + i % nbl, out_rows=Bn * S)

        i2 = layer // 2
        if layer % 2 == 0:
            h2 = _norm_mod(xc, norm2_g[layer], tab, which=3, **lay)
            hid = _mm_swiglu(h2, ffn_w1[i2].astype(BF16), ffn_w3[i2].astype(BF16))
            xc = _mm_res([hid], ffn_w2[i2].astype(BF16), xc, tab, which=5, tm=_pick_tile(xc.shape[0], (1024, 512)), tn=512, **lay)
        else:
            router_p = jnp.pad(moe_router[i2], ((0, 0), (0, LANES - N_EXPERTS)))
            h2, logits = _norm_mod(xc, norm2_g[layer], tab, which=3, router=router_p, **lay)
            pos, src_tok, gate_sorted, tile_e, tile_valid = _route(logits[:, :N_EXPERTS])
            xs = _gather_rows(h2, src_tok)
            ys = _gmm(xs, moe_w1[i2].astype(BF16), moe_w3[i2].astype(BF16), moe_w2[i2].astype(BF16),
                      gate_sorted, tile_e, tile_valid)
            xc = _combine(ys, pos, xc, tab, which=5, **lay)

    return _final_norm(xc, final_norm_g).reshape(Bn, S, D)
```

```python
import functools

import numpy as np
import jax
import jax.numpy as jnp
from jax import lax
from jax.experimental import pallas as pl
from jax.experimental.pallas import tpu as pltpu

F32 = jnp.float32
BF16 = jnp.bfloat16
HIGHEST = lax.Precision.HIGHEST

GRID_W = 64
HEAD_DIM = 128
ROPE_THETA = 10000.0
NORM_EPS = 1e-6
NEG_INF = -1e30
MLA_HEADS = 4
MLA_NOPE = 128
MLA_ROPE = 64
MLA_V = 128
MLA_Q_RANK = 384
MLA_KV_RANK = 256
GQA_HEADS = 4
GQA_KV_HEADS = 2
GDN_HEADS = 4
GDN_CONV = 5
GDN_CHUNK = 64
SWA_HEADS = 4
SWA_KV_HEADS = 2
SWA_WINDOW = 128
N_EXPERTS = 8
TOP_K = 2
GROUP_WIDTH = 512

ROW_BLOCK = 256
LANES = 128
MOE_TILE = 512
ATTN_KV_HEADS_PER_STEP = 2
VMEM_LIMIT = 56 * 1024 * 1024

P_GDN = 0
P_AB = 2048
PG_WIDTH = 2304
P_GQA = 0
P_SWA = 1024
P_MLA = 2048
PA_WIDTH = 3072


def _cparams(sem, vmem=VMEM_LIMIT):
    return pltpu.CompilerParams(dimension_semantics=sem, vmem_limit_bytes=vmem)


def _table_row(gi, nb, ncb, ctx_row):
    b = gi // nb
    r = gi - b * nb
    return jnp.where(r < ncb, ctx_row, b)


def _mods_kernel(c_ref, w_ref, b_ref, o_ref):
    c = c_ref[...]
    a = (c * jax.nn.sigmoid(c)).astype(BF16)
    o_ref[0] = jnp.dot(a, w_ref[0].astype(BF16), preferred_element_type=F32) + b_ref[0]


def _mods(cmat, w_mod, b_mod):
    L, D, N = w_mod.shape
    M = cmat.shape[0]
    tn = 1024
    return pl.pallas_call(
        _mods_kernel,
        grid=(L, N // tn),
        in_specs=[pl.BlockSpec((M, D), lambda l, j: (0, 0)),
                  pl.BlockSpec((1, D, tn), lambda l, j: (l, 0, j)),
                  pl.BlockSpec((1, 1, tn), lambda l, j: (l, 0, j))],
        out_specs=pl.BlockSpec((1, M, tn), lambda l, j: (l, 0, j)),
        out_shape=jax.ShapeDtypeStruct((L, M, N), F32),
        compiler_params=_cparams(("parallel", "parallel")),
        name="mods",
    )(cmat, w_mod, b_mod.reshape(L, 1, N))


def _norm_mod_kernel(x_ref, g_ref, tab_ref, *rest, nb, ncb, ctx_row, which, router):
    idx = _table_row(pl.program_id(0), nb, ncb, ctx_row)
    x = x_ref[...]
    y = x * lax.rsqrt(jnp.mean(x * x, axis=-1, keepdims=True) + NORM_EPS) * g_ref[...]
    shift = tab_ref[pl.ds(idx * 6 + which, 1), :]
    scale = tab_ref[pl.ds(idx * 6 + which + 1, 1), :]
    h = y * (1.0 + scale) + shift
    if router:
        r_ref, h_ref, l_ref = rest
        h_ref[...] = h
        l_ref[...] = jnp.dot(h, r_ref[...], preferred_element_type=F32, precision=HIGHEST)
    else:
        (h_ref,) = rest
        h_ref[...] = h.astype(h_ref.dtype)


def _norm_mod(x, g, tab, *, nb, ncb, ctx_row, which, router=None):
    R, D = x.shape
    kern = functools.partial(_norm_mod_kernel, nb=nb, ncb=ncb, ctx_row=ctx_row, which=which,
                             router=router is not None)
    in_specs = [pl.BlockSpec((ROW_BLOCK, D), lambda i: (i, 0)),
                pl.BlockSpec((1, D), lambda i: (0, 0)),
                pl.BlockSpec(tab.shape, lambda i: (0, 0))]
    args = [x, g.reshape(1, D), tab]
    if router is None:
        out_specs = pl.BlockSpec((ROW_BLOCK, D), lambda i: (i, 0))
        out_shape = jax.ShapeDtypeStruct((R, D), BF16)
    else:
        in_specs.append(pl.BlockSpec(router.shape, lambda i: (0, 0)))
        args.append(router)
        out_specs = [pl.BlockSpec((ROW_BLOCK, D), lambda i: (i, 0)),
                     pl.BlockSpec((ROW_BLOCK, LANES), lambda i: (i, 0))]
        out_shape = [jax.ShapeDtypeStruct((R, D), F32), jax.ShapeDtypeStruct((R, LANES), F32)]
    return pl.pallas_call(kern, grid=(R // ROW_BLOCK,), in_specs=in_specs, out_specs=out_specs,
                          out_shape=out_shape, compiler_params=_cparams(("parallel",)),
                          name="norm_mod")(*args)


def _final_norm_kernel(x_ref, g_ref, o_ref):
    x = x_ref[...]
    o_ref[...] = x * lax.rsqrt(jnp.mean(x * x, axis=-1, keepdims=True) + NORM_EPS) * g_ref[...]


def _final_norm(x, g):
    R, D = x.shape
    return pl.pallas_call(
        _final_norm_kernel, grid=(R // ROW_BLOCK,),
        in_specs=[pl.BlockSpec((ROW_BLOCK, D), lambda i: (i, 0)),
                  pl.BlockSpec((1, D), lambda i: (0, 0))],
        out_specs=pl.BlockSpec((ROW_BLOCK, D), lambda i: (i, 0)),
        out_shape=jax.ShapeDtypeStruct((R, D), F32),
        compiler_params=_cparams(("parallel",)), name="final_norm")(x, g.reshape(1, D))


def _pick_tile(n, options):
    for t in options:
        if n % t == 0:
            return t
    raise ValueError(f"no tile for {n}")


def _mm_plain_kernel(a_ref, w_ref, o_ref):
    o_ref[...] = jnp.dot(a_ref[...], w_ref[...], preferred_element_type=F32).astype(o_ref.dtype)


def _mm_plain(a, w, out_dtype):
    M, K = a.shape
    N = w.shape[1]
    tm = _pick_tile(M, (1024, 512, 256))
    tn = _pick_tile(N, (1024, 768, 512, 256, 128))
    return pl.pallas_call(
        _mm_plain_kernel, grid=(M // tm, N // tn),
        in_specs=[pl.BlockSpec((tm, K), lambda i, j: (i, 0)),
                  pl.BlockSpec((K, tn), lambda i, j: (0, j))],
        out_specs=pl.BlockSpec((tm, tn), lambda i, j: (i, j)),
        out_shape=jax.ShapeDtypeStruct((M, N), out_dtype),
        compiler_params=_cparams(("parallel", "parallel")), name="mm_plain")(a, w)


def _mm_swiglu_kernel(a_ref, w1_ref, w3_ref, o_ref):
    a = a_ref[...]
    h1 = jnp.dot(a, w1_ref[...], preferred_element_type=F32)
    h3 = jnp.dot(a, w3_ref[...], preferred_element_type=F32)
    o_ref[...] = (h1 * jax.nn.sigmoid(h1) * h3).astype(o_ref.dtype)


def _mm_swiglu(a, w1, w3):
    M, K = a.shape
    N = w1.shape[1]
    tm = _pick_tile(M, (1024, 512, 256))
    tn = _pick_tile(N, (512, 256, 128))
    return pl.pallas_call(
        _mm_swiglu_kernel, grid=(M // tm, N // tn),
        in_specs=[pl.BlockSpec((tm, K), lambda i, j: (i, 0)),
                  pl.BlockSpec((K, tn), lambda i, j: (0, j)),
                  pl.BlockSpec((K, tn), lambda i, j: (0, j))],
        out_specs=pl.BlockSpec((tm, tn), lambda i, j: (i, j)),
        out_shape=jax.ShapeDtypeStruct((M, N), BF16),
        compiler_params=_cparams(("parallel", "parallel")), name="mm_swiglu")(a, w1, w3)


def _mm_res_kernel(*refs, n_a, tm, tn, nb, ncb, ctx_row, which):
    a_refs = refs[:n_a]
    w_ref, res_ref, tab_ref, o_ref = refs[n_a:]
    i = pl.program_id(0)
    j = pl.program_id(1)
    acc = None
    k0 = 0
    for a_ref in a_refs:
        kk = a_ref.shape[1]
        part = jnp.dot(a_ref[...], w_ref[k0:k0 + kk, :], preferred_element_type=F32)
        acc = part if acc is None else acc + part
        k0 += kk
    for c in range(tm // ROW_BLOCK):
        idx = _table_row(i * (tm // ROW_BLOCK) + c, nb, ncb, ctx_row)
        gate = tab_ref[pl.ds(idx * 6 + which, 1), pl.ds(pl.multiple_of(j * tn, tn), tn)]
        rows = slice(c * ROW_BLOCK, (c + 1) * ROW_BLOCK)
        o_ref[rows, :] = res_ref[rows, :] + gate * acc[rows, :]


def _mm_res(a_list, w, res, tab, *, nb, ncb, ctx_row, which, tm, tn,
            in_row_map=None, out_rows=None):
    K, N = w.shape
    M = res.shape[0] if out_rows is None else out_rows
    rm = (lambda i: i) if in_row_map is None else in_row_map
    kern = functools.partial(_mm_res_kernel, n_a=len(a_list), tm=tm, tn=tn, nb=nb, ncb=ncb,
                             ctx_row=ctx_row, which=which)

    def a_spec(a):
        if a.shape[0] == res.shape[0]:
            return pl.BlockSpec((tm, a.shape[1]), lambda i, j: (rm(i), 0))
        assert a.shape[0] == M
        return pl.BlockSpec((tm, a.shape[1]), lambda i, j: (i, 0))

    in_specs = [a_spec(a) for a in a_list]
    in_specs += [pl.BlockSpec((K, tn), lambda i, j: (0, j)),
                 pl.BlockSpec((tm, tn), lambda i, j: (rm(i), j)),
                 pl.BlockSpec(tab.shape, lambda i, j: (0, 0))]
    return pl.pallas_call(
        kern, grid=(M // tm, N // tn), in_specs=in_specs,
        out_specs=pl.BlockSpec((tm, tn), lambda i, j: (i, j)),
        out_shape=jax.ShapeDtypeStruct((M, N), F32),
        compiler_params=_cparams(("parallel", "parallel")), name="mm_res")(*a_list, w, res, tab)


def _ones_column(rows):
    lane = lax.broadcasted_iota(jnp.int32, (rows, LANES), 1)
    return jnp.where(lane == 0, 1.0, 0.0).astype(BF16)


def _rope_lanes(x, cos, sa, sb, quarter):
    left = pltpu.roll(x, LANES - quarter, axis=1)
    right = pltpu.roll(x, quarter, axis=1)
    return x * cos + left * sa + right * sb


def _rope_tables(S, C, rot_dim):
    rows = S // GRID_W
    row = jnp.repeat(jnp.arange(rows), GRID_W).astype(F32)
    col = jnp.tile(jnp.arange(GRID_W), rows).astype(F32)
    half = rot_dim // 2
    inv_freq = ROPE_THETA ** (-jnp.arange(0, half, 2, dtype=F32) / half)
    ang_r = row[:, None] * inv_freq
    ang_c = col[:, None] * inv_freq
    ang = jnp.concatenate([ang_r, ang_r, ang_c, ang_c], axis=-1)
    cos, sin = jnp.cos(ang), jnp.sin(ang)
    quarter = rot_dim // 4
    even = ((jnp.arange(rot_dim) // quarter) % 2) == 0
    sa = jnp.where(even, -sin, 0.0)
    sb = jnp.where(even, 0.0, sin)
    pad = LANES - rot_dim

    def full(t, ctx_val):
        t = jnp.pad(t, ((0, 0), (0, pad)), constant_values=ctx_val)
        return jnp.concatenate([jnp.full((C, LANES), ctx_val, F32), t], axis=0)

    return full(cos, 1.0), full(sa, 0.0), full(sb, 0.0)


def _mla_prep_kernel(p_ref, qg_ref, kvg_ref, wq_ref, wkv_ref, cos_ref, sa_ref, sb_ref,
                     q_ref, k_ref, v_ref, *, scale):
    cos, sa, sb = cos_ref[...], sa_ref[...], sb_ref[...]
    cq = p_ref[:, 0:MLA_Q_RANK].astype(F32)
    qn = cq * lax.rsqrt(jnp.mean(cq * cq, axis=-1, keepdims=True) + NORM_EPS) * qg_ref[...]
    q_raw = jnp.dot(qn.astype(BF16), wq_ref[...], preferred_element_type=F32)
    ckv = p_ref[:, MLA_Q_RANK:MLA_Q_RANK + MLA_KV_RANK].astype(F32)
    kvn = ckv * lax.rsqrt(jnp.mean(ckv * ckv, axis=-1, keepdims=True) + NORM_EPS) * kvg_ref[...]
    kv = jnp.dot(kvn.astype(BF16), wkv_ref[...], preferred_element_type=F32)
    kr = p_ref[:, MLA_Q_RANK + MLA_KV_RANK:MLA_Q_RANK + MLA_KV_RANK + LANES].astype(F32)
    kr = _rope_lanes(kr, cos, sa, sb, MLA_ROPE // 4).astype(BF16)
    for h in range(MLA_HEADS):
        c0 = 2 * LANES * h
        q_ref[:, c0:c0 + LANES] = (q_raw[:, c0:c0 + LANES] * scale).astype(BF16)
        qr = _rope_lanes(q_raw[:, c0 + LANES:c0 + 2 * LANES], cos, sa, sb, MLA_ROPE // 4)
        q_ref[:, c0 + LANES:c0 + 2 * LANES] = (qr * scale).astype(BF16)
        k_ref[:, c0:c0 + LANES] = kv[:, LANES * h:LANES * (h + 1)].astype(BF16)
        k_ref[:, c0 + LANES:c0 + 2 * LANES] = kr
    ones_col = _ones_column(p_ref.shape[0])
    for h in range(MLA_HEADS):
        c0 = 2 * LANES * h
        v_ref[:, c0:c0 + LANES] = kv[:, (MLA_HEADS + h) * LANES:(MLA_HEADS + h + 1) * LANES].astype(BF16)
        v_ref[:, c0 + LANES:c0 + 2 * LANES] = ones_col


def _mla_prep(P, qg, kvg, wq_p, wkv_p, tabs, nbt):
    R = P.shape[0]
    width = MLA_HEADS * 2 * LANES
    kern = functools.partial(_mla_prep_kernel, scale=float((MLA_NOPE + MLA_ROPE) ** -0.5))
    tab_spec = pl.BlockSpec((ROW_BLOCK, LANES), lambda i: (i % nbt, 0))
    const = lambda a: pl.BlockSpec(a.shape, lambda i: (0, 0))
    return pl.pallas_call(
        kern, grid=(R // ROW_BLOCK,),
        in_specs=[pl.BlockSpec((ROW_BLOCK, 1024), lambda i: (i, P_MLA // 1024)),
                  const(qg), const(kvg), const(wq_p), const(wkv_p), tab_spec, tab_spec, tab_spec],
        out_specs=[pl.BlockSpec((ROW_BLOCK, width), lambda i: (i, 0)),
                   pl.BlockSpec((ROW_BLOCK, width), lambda i: (i, 0)),
                   pl.BlockSpec((ROW_BLOCK, width), lambda i: (i, 0))],
        out_shape=[jax.ShapeDtypeStruct((R, width), BF16), jax.ShapeDtypeStruct((R, width), BF16),
                   jax.ShapeDtypeStruct((R, width), BF16)],
        compiler_params=_cparams(("parallel",)), name="mla_prep")(P, qg, kvg, wq_p, wkv_p, *tabs)


def _head_prep_kernel(p_ref, qg_ref, kg_ref, cos_ref, sa_ref, sb_ref, o_ref, *, norm, scale,
                      n_q, n_k):
    cos, sa, sb = cos_ref[...], sa_ref[...], sb_ref[...]
    for h in range(n_q + n_k):
        x = p_ref[:, LANES * h:LANES * (h + 1)].astype(F32)
        if norm:
            g = qg_ref[...] if h < n_q else kg_ref[...]
            x = x * lax.rsqrt(jnp.mean(x * x, axis=-1, keepdims=True) + NORM_EPS) * g
        x = _rope_lanes(x, cos, sa, sb, HEAD_DIM // 4)
        if h < n_q:
            x = x * scale
        c0 = LANES * h if h < n_q else LANES * (h + 2 * n_k)
        o_ref[:, c0:c0 + LANES] = x.astype(BF16)
    v_in = LANES * (n_q + n_k)
    v_out = LANES * n_q
    ones_col = _ones_column(p_ref.shape[0])
    for h in range(n_k):
        o_ref[:, v_out + 2 * LANES * h:v_out + 2 * LANES * h + LANES] = (
            p_ref[:, v_in + LANES * h:v_in + LANES * (h + 1)])
        o_ref[:, v_out + 2 * LANES * h + LANES:v_out + 2 * LANES * (h + 1)] = ones_col


def _head_prep(P, col0, qg, kg, tabs, nbt, *, norm, n_q, n_k):
    R = P.shape[0]
    kern = functools.partial(_head_prep_kernel, norm=norm, scale=float(HEAD_DIM ** -0.5),
                             n_q=n_q, n_k=n_k)
    tab_spec = pl.BlockSpec((ROW_BLOCK, LANES), lambda i: (i % nbt, 0))
    const = lambda a: pl.BlockSpec(a.shape, lambda i: (0, 0))
    return pl.pallas_call(
        kern, grid=(R // ROW_BLOCK,),
        in_specs=[pl.BlockSpec((ROW_BLOCK, 1024), lambda i: (i, col0 // 1024)),
                  const(qg), const(kg), tab_spec, tab_spec, tab_spec],
        out_specs=pl.BlockSpec((ROW_BLOCK, 1024 + n_k * LANES), lambda i: (i, 0)),
        out_shape=jax.ShapeDtypeStruct((R, 1024 + n_k * LANES), BF16),
        compiler_params=_cparams(("parallel",)), name="head_prep")(P, qg, kg, *tabs)


def _attn_kernel(q_ref, k_ref, v_ref, o_ref, *, HS, G, dq, dv, C, T, ctx_tile):
    def run(nk):
        nt = (((1,), (1,)), ((), ()))
        chains = [(hs, g) for hs in range(HS) for g in range(G)]
        s = []
        for hs, g in chains:
            q = q_ref[0, :, (hs * G + g) * dq:(hs * G + g + 1) * dq]
            k = k_ref[0, :nk, hs * dq:(hs + 1) * dq]
            s.append(lax.dot_general(q, k, nt, preferred_element_type=F32).astype(BF16))
        p = [jnp.exp(x - jnp.max(x, axis=-1, keepdims=True)) for x in s]
        for (hs, g), pc in zip(chains, p):
            o = jnp.dot(pc, v_ref[0, :nk, hs * 2 * dv:(hs + 1) * 2 * dv], preferred_element_type=F32)
            c0 = (hs * G + g) * dv
            o_ref[0, :, c0:c0 + dv] = (o[:, :dv] / o[:, dv:dv + 1]).astype(o_ref.dtype)

    if ctx_tile:
        qi = pl.program_id(2)
        pl.when(qi == 0)(lambda: run(C))
        pl.when(qi > 0)(lambda: run(T))
    else:
        run(T)


def _attention(q, k, v, *, Hk, G, dq, dv, q_col0, k_col0, v_col0, C, with_ctx):
    Bn, T, _ = q.shape
    ncb = C // ROW_BLOCK
    assert ncb == 1
    nq = T // ROW_BLOCK if with_ctx else (T - C) // ROW_BLOCK
    qoff = 0 if with_ctx else ncb
    HS = ATTN_KV_HEADS_PER_STEP
    assert Hk % HS == 0
    kern = functools.partial(_attn_kernel, HS=HS, G=G, dq=dq, dv=dv, C=C, T=T, ctx_tile=with_ctx)
    qw, kw, vw = HS * G * dq, HS * dq, HS * 2 * dv
    assert q_col0 % qw == 0 and k_col0 % kw == 0 and v_col0 % vw == 0
    return pl.pallas_call(
        kern, grid=(Bn, Hk // HS, nq),
        in_specs=[pl.BlockSpec((1, ROW_BLOCK, qw), lambda b, h, i: (b, i + qoff, q_col0 // qw + h)),
                  pl.BlockSpec((1, T, kw), lambda b, h, i: (b, 0, k_col0 // kw + h)),
                  pl.BlockSpec((1, T, vw), lambda b, h, i: (b, 0, v_col0 // vw + h))],
        out_specs=pl.BlockSpec((1, ROW_BLOCK, HS * G * dv), lambda b, h, i: (b, i, h)),
        out_shape=jax.ShapeDtypeStruct((Bn, nq * ROW_BLOCK, Hk * G * dv), BF16),
        compiler_params=_cparams(("parallel", "parallel", "parallel")), name="attention")(q, k, v)


def _swa_attention(qkv, sink, *, C, with_ctx):
    Bn, T, _ = qkv.shape
    Hq, Hk, d = SWA_HEADS, SWA_KV_HEADS, HEAD_DIM
    ncb = C // ROW_BLOCK
    assert ncb == 1 and T - C >= ROW_BLOCK + 2 * SWA_WINDOW
    nq = T // ROW_BLOCK if with_ctx else (T - C) // ROW_BLOCK
    qoff = 0 if with_ctx else ncb
    kern = functools.partial(_swa_kernel, C=C, S=T - C, ctx_tile=with_ctx, qoff=qoff)
    k_blk = (Hq * d + Hk * 2 * d) // (Hk * d)
    return pl.pallas_call(
        kern, grid=(Bn, nq),
        in_specs=[pl.BlockSpec(memory_space=pltpu.SMEM),
                  pl.BlockSpec((1, ROW_BLOCK, Hq * d), lambda b, i: (b, i + qoff, 0)),
                  pl.BlockSpec((1, T, Hk * d), lambda b, i: (b, 0, k_blk)),
                  pl.BlockSpec((1, T, Hk * 2 * d), lambda b, i: (b, 0, 1))],
        out_specs=pl.BlockSpec((1, ROW_BLOCK, Hq * d), lambda b, i: (b, i, 0)),
        out_shape=jax.ShapeDtypeStruct((Bn, nq * ROW_BLOCK, Hq * d), BF16),
        compiler_params=_cparams(("parallel", "parallel")), name="swa")(sink, qkv, qkv, qkv)


def _swa_kernel(sink_ref, q_ref, k_ref, v_ref, o_ref, *, C, S, ctx_tile, qoff):
    qi = pl.program_id(1) + qoff
    d = HEAD_DIM
    G = SWA_HEADS // SWA_KV_HEADS
    W = SWA_WINDOW
    WIN = ROW_BLOCK + 2 * W
    nt = (((1,), (1,)), ((), ()))
    heads = range(SWA_HEADS)

    def q_of(h):
        return q_ref[0, :, h * d:(h + 1) * d]

    def finish(parts):
        m16 = []
        for h in heads:
            m = jnp.full((ROW_BLOCK, 1), sink_ref[h], F32)
            for s, _ in parts[h]:
                m = jnp.maximum(m, jnp.max(s, axis=-1, keepdims=True).astype(F32))
            m16.append(m.astype(BF16))
        for h in heads:
            o = None
            for s, v in parts[h]:
                pv = jnp.dot(jnp.exp(s - m16[h]), v, preferred_element_type=F32)
                o = pv if o is None else o + pv
            den = jnp.exp(sink_ref[h] - m16[h].astype(F32)) + o[:, d:d + 1]
            o_ref[0, :, h * d:(h + 1) * d] = (o[:, :d] / den).astype(o_ref.dtype)

    def ctx_scores(h):
        kc = k_ref[0, :C, (h // G) * d:(h // G + 1) * d]
        return (lax.dot_general(q_of(h), kc, nt, preferred_element_type=F32).astype(BF16),
                v_ref[0, :C, (h // G) * 2 * d:(h // G + 1) * 2 * d])

    def run_ctx():
        finish([[ctx_scores(h)] for h in heads])

    def run_lat():
        q0 = (qi - (C // ROW_BLOCK)) * ROW_BLOCK
        ws = jnp.clip(q0 - W, 0, S - WIN)
        start = pl.multiple_of(C + ws, LANES)
        qpos = q0 + lax.broadcasted_iota(jnp.int32, (ROW_BLOCK, WIN), 0)
        kpos = ws + lax.broadcasted_iota(jnp.int32, (ROW_BLOCK, WIN), 1)
        valid = jnp.abs(kpos - qpos) <= W
        parts = []
        for h in heads:
            kw = k_ref[0, pl.ds(start, WIN), (h // G) * d:(h // G + 1) * d]
            vw = v_ref[0, pl.ds(start, WIN), (h // G) * 2 * d:(h // G + 1) * 2 * d]
            s_loc = jnp.where(valid, lax.dot_general(q_of(h), kw, nt, preferred_element_type=F32),
                              NEG_INF).astype(BF16)
            parts.append([(s_loc, vw), ctx_scores(h)])
        finish(parts)

    if ctx_tile:
        pl.when(qi == 0)(run_ctx)
        pl.when(qi > 0)(run_lat)
    else:
        run_lat()


def _gdn_prep_kernel(x_ref, w_ref, o_ref, *, C, T):
    j = pl.program_id(1)
    x = x_ref[0]
    t = lax.broadcasted_iota(jnp.int32, (T, 1), 0)
    is_lat = t >= C
    lo = jnp.where(is_lat, C, 0)
    hi = jnp.where(is_lat, T, C)
    acc = None
    half = (GDN_CONV - 1) // 2
    for tap in range(GDN_CONV):
        s = tap - half
        xs = x if s == 0 else pltpu.roll(x, (-s) % T, axis=0)
        ok = jnp.logical_and(t + s >= lo, t + s < hi)
        term = jnp.where(ok, xs, 0.0) * w_ref[tap:tap + 1, :]
        acc = term if acc is None else acc + term
    y = acc * jax.nn.sigmoid(acc)
    inv = lax.rsqrt(jnp.sum(y * y, axis=-1, keepdims=True) + 1e-6)
    nh = GDN_HEADS
    f = jnp.where(j < 2 * nh, inv, 1.0) * jnp.where(j < nh, float(HEAD_DIM ** -0.5), 1.0)
    o_ref[0] = y * f


def _gdn_prep(P3, conv_w8, C):
    Bn, T, _ = P3.shape
    nblk = 3 * GDN_HEADS
    kern = functools.partial(_gdn_prep_kernel, C=C, T=T)
    return pl.pallas_call(
        kern, grid=(Bn, nblk),
        in_specs=[pl.BlockSpec((1, T, LANES), lambda b, j: (b, 0, P_GDN // LANES + j)),
                  pl.BlockSpec((8, LANES), lambda b, j: (0, j))],
        out_specs=pl.BlockSpec((1, T, LANES), lambda b, j: (b, 0, j)),
        out_shape=jax.ShapeDtypeStruct((Bn, T, nblk * LANES), F32),
        compiler_params=_cparams(("parallel", "parallel")), name="gdn_prep")(P3, conv_w8)


_GDN_INV_BASE = 8
GDN_BATCH_PER_STEP = 2


def _dot16(a, b):
    return jnp.dot(a, b, preferred_element_type=F32)


def _dot16_nt(a, b):
    return lax.dot_general(a, b, (((1,), (1,)), ((), ())), preferred_element_type=F32)


def _dot16_tn(a, b):
    return lax.dot_general(a, b, (((0,), (0,)), ((), ())), preferred_element_type=F32)


def _dot32(a, b):
    return jnp.dot(a, b, preferred_element_type=F32, precision=HIGHEST)


def _dot32_nt(a, b):
    return lax.dot_general(a, b, (((1,), (1,)), ((), ())), preferred_element_type=F32,
                           precision=HIGHEST)


def _dot32_tn(a, b):
    return lax.dot_general(a, b, (((0,), (0,)), ((), ())), preferred_element_type=F32,
                           precision=HIGHEST)


def _gdn_chunk_kernel(qf_ref, qb_ref, abf_ref, abb_ref, par_ref, of_ref, ob_ref, s_ref):
    Cc = GDN_CHUNK
    H = GDN_HEADS
    d = HEAD_DIM

    @pl.when(pl.program_id(1) == 0)
    def _():
        s_ref[...] = jnp.zeros_like(s_ref)

    ri = lax.broadcasted_iota(jnp.int32, (Cc, Cc), 0)
    ci = lax.broadcasted_iota(jnp.int32, (Cc, Cc), 1)
    eye = (ri == ci).astype(F32)
    neg_a = par_ref[0:1, :]
    dt_b = par_ref[1:2, :]
    base = _GDN_INV_BASE
    diag_blk = (ri // base) == (ci // base)
    off_blks = []
    s = base
    while s < Cc:
        off_blks.append(jnp.logical_and((ri // (2 * s)) == (ci // (2 * s)), (ri // s) != (ci // s)))
        s *= 2

    dirs = []
    for bb, direction in [(bb, direction) for bb in range(qf_ref.shape[0]) for direction in range(2)]:
        ab = (abf_ref if direction == 0 else abb_ref)[bb]
        if direction == 0:
            incl, incl_t, strict, last = ri >= ci, ri <= ci, ri > ci, Cc - 1
        else:
            incl, incl_t, strict, last = ri <= ci, ri >= ci, ri < ci, 0
        z = ab + dt_b
        sp = jnp.maximum(z, 0.0) + jnp.log1p(jnp.exp(-jnp.abs(z)))
        gl_all = neg_a * sp
        gc_all = _dot32(incl.astype(F32), gl_all)
        g_tot = gc_all[last:last + 1, :]
        dirs.append(dict(
            bb=bb, direction=direction,
            x_ref=qf_ref if direction == 0 else qb_ref, o_ref=of_ref if direction == 0 else ob_ref,
            incl=incl, strict=strict, gc_all=gc_all,
            gct_all=_dot32_tn(gl_all, incl_t.astype(F32)),
            beta_all=jax.nn.sigmoid(ab),
            eg_all=jnp.exp(gc_all), ekg_all=jnp.exp(g_tot - gc_all), egt_all=jnp.exp(g_tot)))

    units = [(dd, dd["direction"] * H + h, h) for dd in dirs for h in range(H)]
    U = range(len(units))

    k16, kb, decay, a_mat = [], [], [], []
    for dd, lane, h in units:
        k = dd["x_ref"][dd["bb"],:, (H + h) * d:(H + h + 1) * d]
        beta = dd["beta_all"][:, 2 * H + lane:2 * H + lane + 1]
        diff = dd["gc_all"][:, lane:lane + 1] - dd["gct_all"][lane:lane + 1, :]
        decay.append(jnp.where(dd["incl"], jnp.exp(jnp.where(dd["incl"], diff, 0.0)), 0.0))
        kb.append(k * beta)
        k16.append(k.astype(BF16))
    for i in U:
        a_mat.append(jnp.where(units[i][0]["strict"],
                               _dot16_nt(kb[i].astype(BF16), k16[i]) * decay[i], 0.0))

    npow = [jnp.where(diag_blk, a, 0.0).astype(BF16) for a in a_mat]
    t = [eye - jnp.where(diag_blk, a, 0.0) for a in a_mat]
    sq = 2 * base
    while sq > 4:
        npow = [_dot16(n, n).astype(BF16) for n in npow]
        t = [t[i] + _dot16(t[i].astype(BF16), npow[i]) for i in U]
        sq //= 2
    for off_blk in off_blks:
        t16 = [x.astype(BF16) for x in t]
        xo = [_dot16(t16[i], jnp.where(off_blk, a_mat[i], 0.0).astype(BF16)).astype(BF16) for i in U]
        t = [t[i] - _dot16(xo[i], t16[i]) for i in U]

    uw, intra, qg = [], [], []
    for i in U:
        dd, lane, h = units[i]
        q = dd["x_ref"][dd["bb"],:, h * d:(h + 1) * d]
        v = dd["x_ref"][dd["bb"],:, (2 * H + h) * d:(2 * H + h + 1) * d]
        beta = dd["beta_all"][:, 2 * H + lane:2 * H + lane + 1]
        eg = dd["eg_all"][:, lane:lane + 1]
        rhs = jnp.concatenate([v * beta, kb[i] * eg], axis=1).astype(BF16)
        uw.append(_dot16(t[i].astype(BF16), rhs))
        intra.append(jnp.where(dd["incl"], _dot16_nt(q.astype(BF16), k16[i]) * decay[i], 0.0)
                     .astype(BF16))
        qg.append((q * eg).astype(BF16))
    state = [s_ref[units[i][0]["bb"] * 2 * H + units[i][1]] for i in U]
    s16 = [x.astype(BF16) for x in state]
    v_new = [uw[i][:, :d] - _dot16(uw[i][:, d:].astype(BF16), s16[i]) for i in U]
    vn16 = [x.astype(BF16) for x in v_new]
    for i in U:
        dd, lane, h = units[i]
        dd["o_ref"][dd["bb"], :, h * d:(h + 1) * d] = (_dot16(qg[i], s16[i])
                                                       + _dot16(intra[i], vn16[i]))
    for i in U:
        dd, lane, h = units[i]
        k = dd["x_ref"][dd["bb"],:, (H + h) * d:(H + h + 1) * d]
        kg = (k * dd["ekg_all"][:, lane:lane + 1]).astype(BF16)
        s_ref[dd["bb"] * 2 * H + lane] = (state[i] * dd["egt_all"][:, lane:lane + 1]
                                          + _dot16_tn(kg, vn16[i]))


def _gdn_chunks(qkv, P3, par, C):
    Bn, T, _ = qkv.shape
    Cc = GDN_CHUNK
    nc = T // Cc
    ncc = C // Cc
    H = GDN_HEADS

    def bwd_chunk(s):
        return jnp.where(s < ncc, ncc - 1 - s, nc + ncc - 1 - s)

    ab_blk = P_AB // LANES
    nbb = _pick_tile(Bn, (GDN_BATCH_PER_STEP, 1))
    return pl.pallas_call(
        _gdn_chunk_kernel, grid=(Bn // nbb, nc),
        in_specs=[pl.BlockSpec((nbb, Cc, 3 * H * HEAD_DIM), lambda b, s: (b, s, 0)),
                  pl.BlockSpec((nbb, Cc, 3 * H * HEAD_DIM), lambda b, s: (b, bwd_chunk(s), 0)),
                  pl.BlockSpec((nbb, Cc, LANES), lambda b, s: (b, s, ab_blk)),
                  pl.BlockSpec((nbb, Cc, LANES), lambda b, s: (b, bwd_chunk(s), ab_blk)),
                  pl.BlockSpec((8, LANES), lambda b, s: (0, 0))],
        out_specs=[pl.BlockSpec((nbb, Cc, H * HEAD_DIM), lambda b, s: (b, s, 0)),
                   pl.BlockSpec((nbb, Cc, H * HEAD_DIM), lambda b, s: (b, bwd_chunk(s), 0))],
        out_shape=[jax.ShapeDtypeStruct((Bn, T, H * HEAD_DIM), F32),
                   jax.ShapeDtypeStruct((Bn, T, H * HEAD_DIM), F32)],
        scratch_shapes=[pltpu.VMEM((nbb * 2 * H, HEAD_DIM, HEAD_DIM), F32)],
        compiler_params=_cparams(("parallel", "arbitrary")), name="gdn_chunks")(
            qkv, qkv, P3, P3, par)


def _gdn_post_kernel(of_ref, ob_ref, z_ref, g_ref, o_ref):
    d = HEAD_DIM
    for h in range(GDN_HEADS):
        o = of_ref[:, h * d:(h + 1) * d] + ob_ref[:, h * d:(h + 1) * d]
        y = o * lax.rsqrt(jnp.mean(o * o, axis=-1, keepdims=True) + NORM_EPS) * g_ref[...]
        z = z_ref[:, h * d:(h + 1) * d]
        o_ref[:, h * d:(h + 1) * d] = (y * (z * jax.nn.sigmoid(z))).astype(o_ref.dtype)


def _gdn_post(of, ob, P, g):
    R, W = of.shape
    zblk = (P_GDN + 3 * W) // W
    return pl.pallas_call(
        _gdn_post_kernel, grid=(R // ROW_BLOCK,),
        in_specs=[pl.BlockSpec((ROW_BLOCK, W), lambda i: (i, 0)),
                  pl.BlockSpec((ROW_BLOCK, W), lambda i: (i, 0)),
                  pl.BlockSpec((ROW_BLOCK, W), lambda i: (i, zblk)),
                  pl.BlockSpec((1, HEAD_DIM), lambda i: (0, 0))],
        out_specs=pl.BlockSpec((ROW_BLOCK, W), lambda i: (i, 0)),
        out_shape=jax.ShapeDtypeStruct((R, W), BF16),
        compiler_params=_cparams(("parallel",)), name="gdn_post")(of, ob, P, g)


def _gather_rows_kernel(tok_ref, tok_next_ref, h_hbm, o_ref, buf, sem):
    i = pl.program_id(0)
    nt = pl.num_programs(0)
    n = buf.shape[1]
    slot = i % 2

    def issue(t_ref, s):
        def body(r, c):
            pltpu.make_async_copy(h_hbm.at[pl.ds(t_ref[0, 0, r], 1), :],
                                  buf.at[s, pl.ds(r, 1), :], sem.at[s]).start()
            return c
        lax.fori_loop(0, n, body, 0, unroll=8)

    @pl.when(i == 0)
    def _():
        issue(tok_ref, 0)

    @pl.when(i + 1 < nt)
    def _():
        issue(tok_next_ref, 1 - slot)

    pltpu.make_async_copy(h_hbm.at[pl.ds(0, n), :], buf.at[slot], sem.at[slot]).wait()
    o_ref[...] = buf[slot].astype(o_ref.dtype)


def _gather_rows(h, src_tok):
    D = h.shape[1]
    n = src_tok.shape[0]
    nt = n // ROW_BLOCK
    return pl.pallas_call(
        _gather_rows_kernel, grid=(nt,),
        in_specs=[pl.BlockSpec((1, 1, ROW_BLOCK), lambda i: (i, 0, 0), memory_space=pltpu.SMEM),
                  pl.BlockSpec((1, 1, ROW_BLOCK), lambda i: (jnp.minimum(i + 1, nt - 1), 0, 0),
                               memory_space=pltpu.SMEM),
                  pl.BlockSpec(memory_space=pl.ANY)],
        out_specs=pl.BlockSpec((ROW_BLOCK, D), lambda i: (i, 0)),
        out_shape=jax.ShapeDtypeStruct((n, D), BF16),
        scratch_shapes=[pltpu.VMEM((2, ROW_BLOCK, D), F32), pltpu.SemaphoreType.DMA((2,))],
        compiler_params=_cparams(("arbitrary",)), name="moe_gather")(
            src_tok.reshape(nt, 1, ROW_BLOCK), src_tok.reshape(nt, 1, ROW_BLOCK), h)


def _gmm_kernel(te_ref, tv_ref, x_ref, w1_ref, w3_ref, w2_ref, o_ref, acc_ref):
    i = pl.program_id(0)
    f = pl.program_id(1)
    nf = pl.num_programs(1)

    @pl.when(tv_ref[i] == 1)
    def _():
        x = x_ref[...]
        h1 = jnp.dot(x, w1_ref[0], preferred_element_type=F32)
        h3 = jnp.dot(x, w3_ref[0], preferred_element_type=F32)
        hid = (h1 * jax.nn.sigmoid(h1) * h3).astype(BF16)
        part = jnp.dot(hid, w2_ref[0], preferred_element_type=F32)

        @pl.when(f == 0)
        def _():
            acc_ref[...] = part

        @pl.when(jnp.logical_and(f > 0, f < nf - 1))
        def _():
            acc_ref[...] += part

        @pl.when(f == nf - 1)
        def _():
            o_ref[...] = acc_ref[...] + part

    @pl.when(jnp.logical_and(tv_ref[i] == 0, f == nf - 1))
    def _():
        o_ref[...] = jnp.zeros_like(o_ref)


def _gmm(xs, w1, w3, w2, tile_e, tile_valid):
    n, D = xs.shape
    E, _, F = w1.shape
    tf = 1024
    nt = n // MOE_TILE
    grid_spec = pltpu.PrefetchScalarGridSpec(
        num_scalar_prefetch=2, grid=(nt, F // tf),
        in_specs=[pl.BlockSpec((MOE_TILE, D), lambda i, f, te, tv: (i, 0)),
                  pl.BlockSpec((1, D, tf), lambda i, f, te, tv: (te[i], 0, f)),
                  pl.BlockSpec((1, D, tf), lambda i, f, te, tv: (te[i], 0, f)),
                  pl.BlockSpec((1, tf, D), lambda i, f, te, tv: (te[i], f, 0))],
        out_specs=pl.BlockSpec((MOE_TILE, D), lambda i, f, te, tv: (i, 0)),
        scratch_shapes=[pltpu.VMEM((MOE_TILE, D), F32)])
    return pl.pallas_call(
        _gmm_kernel, grid_spec=grid_spec,
        out_shape=jax.ShapeDtypeStruct((n, D), F32),
        compiler_params=_cparams(("parallel", "arbitrary")), name="moe_gmm")(
            tile_e, tile_valid, xs, w1, w3, w2)


def _combine_kernel(pos_ref, pos_next_ref, ys_hbm, x_ref, eg_ref, tab_ref, o_ref, buf, sem, *, nb,
                    ncb, ctx_row, which):
    i = pl.program_id(0)
    nt = pl.num_programs(0)
    n = buf.shape[2]
    slot = i % 2
    idx = _table_row(i, nb, ncb, ctx_row)

    def issue(p_ref, s):
        def body(r, c):
            for k in range(TOP_K):
                pltpu.make_async_copy(ys_hbm.at[pl.ds(p_ref[0, k, r], 1), :],
                                      buf.at[s, k, pl.ds(r, 1), :], sem.at[s]).start()
            return c
        lax.fori_loop(0, n, body, 0, unroll=4)

    @pl.when(i == 0)
    def _():
        issue(pos_ref, 0)

    @pl.when(i + 1 < nt)
    def _():
        issue(pos_next_ref, 1 - slot)

    for k in range(TOP_K):
        pltpu.make_async_copy(ys_hbm.at[pl.ds(0, n), :], buf.at[slot, k], sem.at[slot]).wait()
    gate = tab_ref[pl.ds(idx * 6 + which, 1), :]
    eg = eg_ref[...]
    y = eg[:, 0:1] * buf[slot, 0]
    for k in range(1, TOP_K):
        y = y + eg[:, k:k + 1] * buf[slot, k]
    o_ref[...] = x_ref[...] + gate * y


def _combine(ys, pos, egate, x, tab, *, nb, ncb, ctx_row, which):
    R, D = x.shape
    nt = R // ROW_BLOCK
    pos3 = pos.reshape(nt, ROW_BLOCK, TOP_K).transpose(0, 2, 1)
    kern = functools.partial(_combine_kernel, nb=nb, ncb=ncb, ctx_row=ctx_row, which=which)
    return pl.pallas_call(
        kern, grid=(nt,),
        in_specs=[pl.BlockSpec((1, TOP_K, ROW_BLOCK), lambda i: (i, 0, 0), memory_space=pltpu.SMEM),
                  pl.BlockSpec((1, TOP_K, ROW_BLOCK), lambda i: (jnp.minimum(i + 1, nt - 1), 0, 0),
                               memory_space=pltpu.SMEM),
                  pl.BlockSpec(memory_space=pl.ANY),
                  pl.BlockSpec((ROW_BLOCK, D), lambda i: (i, 0)),
                  pl.BlockSpec((ROW_BLOCK, LANES), lambda i: (i, 0)),
                  pl.BlockSpec(tab.shape, lambda i: (0, 0))],
        out_specs=pl.BlockSpec((ROW_BLOCK, D), lambda i: (i, 0)),
        out_shape=jax.ShapeDtypeStruct((R, D), F32),
        scratch_shapes=[pltpu.VMEM((2, TOP_K, ROW_BLOCK, D), F32), pltpu.SemaphoreType.DMA((2,))],
        compiler_params=_cparams(("arbitrary",)), name="moe_combine")(pos3, pos3, ys, x, egate, tab)


def _route(logits):
    R = logits.shape[0]
    E = N_EXPERTS
    top_v, top_i = lax.top_k(logits, TOP_K)
    gate = jax.nn.softmax(top_v, axis=-1)
    onehot = jnp.sum((top_i[:, :, None] == jnp.arange(E)[None, None, :]).astype(jnp.int32), axis=1)
    before = jnp.cumsum(onehot, axis=0) - onehot
    sizes = jnp.sum(onehot, axis=0)
    padded = ((sizes + MOE_TILE - 1) // MOE_TILE) * MOE_TILE
    ends = jnp.cumsum(padded)
    off = ends - padded
    pos = off[top_i] + jnp.take_along_axis(before, top_i, axis=1)
    nt = (TOP_K * R) // MOE_TILE + E
    n = nt * MOE_TILE
    flat = pos.reshape(-1)
    src_tok = jnp.zeros((n,), jnp.int32).at[flat].set(jnp.repeat(jnp.arange(R, dtype=jnp.int32), TOP_K))
    egate = jnp.pad(gate, ((0, 0), (0, LANES - TOP_K)))
    tile_start = jnp.arange(nt, dtype=jnp.int32) * MOE_TILE
    tile_valid = (tile_start < ends[-1]).astype(jnp.int32)
    tile_e = jnp.searchsorted(ends, tile_start, side="right").astype(jnp.int32)
    last_e = jnp.max(jnp.where(tile_valid == 1, tile_e, 0))
    tile_e = jnp.where(tile_valid == 1, tile_e, last_e)
    return pos.astype(jnp.int32), src_tok, egate, tile_e, tile_valid


def _prep_w_in(w):
    D = w.shape[0]
    z = lambda n: jnp.zeros((D, n), w.dtype)
    w_g = jnp.concatenate([w[:, 1728:3776], w[:, 3776:3792], z(PG_WIDTH - 2048 - 16)], axis=1)
    w_a = jnp.concatenate([w[:, 704:1728], w[:, 3792:4816], w[:, 0:704], z(PA_WIDTH - 2048 - 704)],
                          axis=1)
    assert w_g.shape[1] == PG_WIDTH and w_a.shape[1] == PA_WIDTH
    return w_g.astype(BF16), w_a.astype(BF16)


def _prep_w_uq(w):
    r = w.shape[0]
    w = w.reshape(r, MLA_HEADS, MLA_NOPE + MLA_ROPE)
    w = jnp.pad(w, ((0, 0), (0, 0), (0, 2 * LANES - (MLA_NOPE + MLA_ROPE))))
    return w.reshape(r, MLA_HEADS * 2 * LANES).astype(BF16)


def _prep_w_ukv(w):
    r = w.shape[0]
    w = w.reshape(r, MLA_HEADS, MLA_NOPE + MLA_V)
    return jnp.concatenate([w[:, :, :MLA_NOPE].reshape(r, -1), w[:, :, MLA_NOPE:].reshape(r, -1)],
                           axis=1).astype(BF16)


def kernel(x, c, ctx, c_ctx, norm1_g, norm2_g, w_mod, b_mod, w_in, mla_q_norm_g, mla_kv_norm_g,
           mla_w_uq, mla_w_ukv, gqa_q_norm_g, gqa_k_norm_g, gdn_conv_w, gdn_a_log, gdn_dt_bias,
           gdn_norm_g, swa_sink, w_out, ffn_w1, ffn_w3, ffn_w2, moe_router, moe_w1, moe_w3, moe_w2,
           final_norm_g):
    Bn, S, D = x.shape
    C = ctx.shape[1]
    T = C + S
    R = Bn * T
    depth = w_in.shape[0]
    nbt = T // ROW_BLOCK
    ncb = C // ROW_BLOCK
    nbl = S // ROW_BLOCK

    m_rows = ((Bn + 1 + 7) // 8) * 8
    cmat = jnp.zeros((m_rows, D), F32).at[:Bn].set(c).at[Bn].set(c_ctx)
    mods = _mods(cmat, w_mod, b_mod).reshape(depth, m_rows * 6, D)

    rope_mla = _rope_tables(S, C, MLA_ROPE)
    rope_head = _rope_tables(S, C, HEAD_DIM)

    xc = jnp.concatenate([ctx, x], axis=1).reshape(R, D)
    lay = dict(nb=nbt, ncb=ncb, ctx_row=Bn)

    for layer in range(depth):
        with_ctx = layer < depth - 1
        tab = mods[layer]
        h = _norm_mod(xc, norm1_g[layer], tab, which=0, **lay)
        w_g, w_a = _prep_w_in(w_in[layer])
        Pg = _mm_plain(h, w_g, F32)
        Pa = _mm_plain(h, w_a, BF16)
        Pg3 = Pg.reshape(Bn, T, PG_WIDTH)

        qa, ka, va = _mla_prep(Pa, mla_q_norm_g[layer].reshape(1, -1), mla_kv_norm_g[layer].reshape(1, -1),
                               _prep_w_uq(mla_w_uq[layer]), _prep_w_ukv(mla_w_ukv[layer]), rope_mla, nbt)
        ya = _attention(qa.reshape(Bn, T, -1), ka.reshape(Bn, T, -1), va.reshape(Bn, T, -1),
                        Hk=MLA_HEADS, G=1, dq=2 * LANES, dv=MLA_V, q_col0=0, k_col0=0, v_col0=0,
                        C=C, with_ctx=with_ctx)
        qkv_b = _head_prep(Pa, P_GQA, gqa_q_norm_g[layer].reshape(1, -1), gqa_k_norm_g[layer].reshape(1, -1),
                           rope_head, nbt, norm=True, n_q=GQA_HEADS, n_k=GQA_KV_HEADS).reshape(Bn, T, -1)
        Gb = GQA_HEADS // GQA_KV_HEADS
        yb = _attention(qkv_b, qkv_b, qkv_b, Hk=GQA_KV_HEADS, G=Gb, dq=HEAD_DIM, dv=HEAD_DIM,
                        q_col0=0, v_col0=GQA_HEADS * HEAD_DIM,
                        k_col0=(GQA_HEADS + 2 * GQA_KV_HEADS) * HEAD_DIM, C=C, with_ctx=with_ctx)
        conv_w8 = jnp.pad(gdn_conv_w[layer], ((0, 8 - GDN_CONV), (0, 0)))
        qkv_c = _gdn_prep(Pg3, conv_w8, C)
        par = jnp.zeros((8, LANES), F32)
        par = par.at[0, :2 * GDN_HEADS].set(-jnp.exp(gdn_a_log[layer].reshape(-1)))
        par = par.at[1, :2 * GDN_HEADS].set(gdn_dt_bias[layer].reshape(-1))
        of, ob = _gdn_chunks(qkv_c, Pg3, par, C)
        yc = _gdn_post(of.reshape(R, -1), ob.reshape(R, -1), Pg, gdn_norm_g[layer].reshape(1, -1))
        qkv_d = _head_prep(Pa, P_SWA, gqa_q_norm_g[layer].reshape(1, -1), gqa_k_norm_g[layer].reshape(1, -1),
                           rope_head, nbt, norm=False, n_q=SWA_HEADS, n_k=SWA_KV_HEADS).reshape(Bn, T, -1)
        yd = _swa_attention(qkv_d, swa_sink[layer], C=C, with_ctx=with_ctx)

        mixed = [ya.reshape(-1, GROUP_WIDTH), yb.reshape(-1, GROUP_WIDTH), yc,
                 yd.reshape(-1, GROUP_WIDTH)]
        w_o = w_out[layer].astype(BF16)
        if with_ctx:
            xc = _mm_res(mixed, w_o, xc, tab, which=2, tm=_pick_tile(R, (1024, 512, 256)),
                         tn=_pick_tile(D, (1024, 512)), **lay)
        else:
            lay = dict(nb=nbl, ncb=0, ctx_row=Bn)
            xc = _mm_res(mixed, w_o, xc, tab, which=2, tm=ROW_BLOCK, tn=D, **lay,
                         in_row_map=lambda i: (i // nbl) * nbt + ncb + i % nbl, out_rows=Bn * S)

        i2 = layer // 2
        if layer % 2 == 0:
            h2 = _norm_mod(xc, norm2_g[layer], tab, which=3, **lay)
            hid = _mm_swiglu(h2, ffn_w1[i2].astype(BF16), ffn_w3[i2].astype(BF16))
            xc = _mm_res([hid], ffn_w2[i2].astype(BF16), xc, tab, which=5, tm=_pick_tile(xc.shape[0], (1024, 512)), tn=512, **lay)
        else:
            router_p = jnp.pad(moe_router[i2], ((0, 0), (0, LANES - N_EXPERTS)))
            h2, logits = _norm_mod(xc, norm2_g[layer], tab, which=3, router=router_p, **lay)
            pos, src_tok, egate, tile_e, tile_valid = _route(logits[:, :N_EXPERTS])
            xs = _gather_rows(h2, src_tok)
            ys = _gmm(xs, moe_w1[i2].astype(BF16), moe_w3[i2].astype(BF16), moe_w2[i2].astype(BF16),
                      tile_e, tile_valid)
            xc = _combine(ys, pos, egate, xc, tab, which=5, **lay)

    return _final_norm(xc, final_norm_g).reshape(Bn, S, D)
```

```python
import functools

import numpy as np
import jax
import jax.numpy as jnp
from jax import lax
from jax.experimental import pallas as pl
from jax.experimental.pallas import tpu as pltpu

F32 = jnp.float32
BF16 = jnp.bfloat16
HIGHEST = lax.Precision.HIGHEST

GRID_W = 64
HEAD_DIM = 128
ROPE_THETA = 10000.0
NORM_EPS = 1e-6
NEG_INF = -1e30
MLA_HEADS = 4
MLA_NOPE = 128
MLA_ROPE = 64
MLA_V = 128
MLA_Q_RANK = 384
MLA_KV_RANK = 256
GQA_HEADS = 4
GQA_KV_HEADS = 2
GDN_HEADS = 4
GDN_CONV = 5
GDN_CHUNK = 64
SWA_HEADS = 4
SWA_KV_HEADS = 2
SWA_WINDOW = 128
N_EXPERTS = 8
TOP_K = 2
GROUP_WIDTH = 512

ROW_BLOCK = 256
LANES = 128
SUBLANES = 8
MOE_TILE = 512
ATTN_KV_HEADS_PER_STEP = 2
VMEM_LIMIT = 56 * 1024 * 1024

P_GDN = 0
P_AB = 2048
PG_WIDTH = 2304
P_GQA = 0
P_SWA = 1024
P_MLA = 2048
PA_WIDTH = 3072


def _cparams(sem, vmem=VMEM_LIMIT):
    return pltpu.CompilerParams(dimension_semantics=sem, vmem_limit_bytes=vmem)


def _table_row(gi, nb, ncb, ctx_row):
    b = gi // nb
    r = gi - b * nb
    return jnp.where(r < ncb, ctx_row, b)


def _mods_kernel(c_ref, w_ref, b_ref, o_ref):
    c = c_ref[...]
    a = (c * jax.nn.sigmoid(c)).astype(BF16)
    o_ref[0] = jnp.dot(a, w_ref[0].astype(BF16), preferred_element_type=F32) + b_ref[0]


def _mods(cmat, w_mod, b_mod):
    L, D, N = w_mod.shape
    M = cmat.shape[0]
    tn = 1024
    return pl.pallas_call(
        _mods_kernel,
        grid=(L, N // tn),
        in_specs=[pl.BlockSpec((M, D), lambda l, j: (0, 0)),
                  pl.BlockSpec((1, D, tn), lambda l, j: (l, 0, j)),
                  pl.BlockSpec((1, 1, tn), lambda l, j: (l, 0, j))],
        out_specs=pl.BlockSpec((1, M, tn), lambda l, j: (l, 0, j)),
        out_shape=jax.ShapeDtypeStruct((L, M, N), F32),
        compiler_params=_cparams(("parallel", "parallel")),
        name="mods",
    )(cmat, w_mod, b_mod.reshape(L, 1, N))


def _norm_mod_kernel(x_ref, g_ref, tab_ref, *rest, nb, ncb, ctx_row, which, router):
    idx = _table_row(pl.program_id(0), nb, ncb, ctx_row)
    x = x_ref[...]
    y = x * lax.rsqrt(jnp.mean(x * x, axis=-1, keepdims=True) + NORM_EPS) * g_ref[...]
    shift = tab_ref[pl.ds(idx * 6 + which, 1), :]
    scale = tab_ref[pl.ds(idx * 6 + which + 1, 1), :]
    h = y * (1.0 + scale) + shift
    if router:
        r_ref, h_ref, l_ref = rest
        h_ref[...] = h
        l_ref[...] = jnp.dot(h, r_ref[...], preferred_element_type=F32, precision=HIGHEST)
    else:
        (h_ref,) = rest
        h_ref[...] = h.astype(h_ref.dtype)


def _norm_mod(x, g, tab, *, nb, ncb, ctx_row, which, router=None):
    R, D = x.shape
    kern = functools.partial(_norm_mod_kernel, nb=nb, ncb=ncb, ctx_row=ctx_row, which=which,
                             router=router is not None)
    in_specs = [pl.BlockSpec((ROW_BLOCK, D), lambda i: (i, 0)),
                pl.BlockSpec((1, D), lambda i: (0, 0)),
                pl.BlockSpec(tab.shape, lambda i: (0, 0))]
    args = [x, g.reshape(1, D), tab]
    if router is None:
        out_specs = pl.BlockSpec((ROW_BLOCK, D), lambda i: (i, 0))
        out_shape = jax.ShapeDtypeStruct((R, D), BF16)
    else:
        in_specs.append(pl.BlockSpec(router.shape, lambda i: (0, 0)))
        args.append(router)
        out_specs = [pl.BlockSpec((ROW_BLOCK, D), lambda i: (i, 0)),
                     pl.BlockSpec((ROW_BLOCK, LANES), lambda i: (i, 0))]
        out_shape = [jax.ShapeDtypeStruct((R, D), F32), jax.ShapeDtypeStruct((R, LANES), F32)]
    return pl.pallas_call(kern, grid=(R // ROW_BLOCK,), in_specs=in_specs, out_specs=out_specs,
                          out_shape=out_shape, compiler_params=_cparams(("parallel",)),
                          name="norm_mod")(*args)


def _final_norm_kernel(x_ref, g_ref, o_ref):
    x = x_ref[...]
    o_ref[...] = x * lax.rsqrt(jnp.mean(x * x, axis=-1, keepdims=True) + NORM_EPS) * g_ref[...]


def _final_norm(x, g):
    R, D = x.shape
    return pl.pallas_call(
        _final_norm_kernel, grid=(R // ROW_BLOCK,),
        in_specs=[pl.BlockSpec((ROW_BLOCK, D), lambda i: (i, 0)),
                  pl.BlockSpec((1, D), lambda i: (0, 0))],
        out_specs=pl.BlockSpec((ROW_BLOCK, D), lambda i: (i, 0)),
        out_shape=jax.ShapeDtypeStruct((R, D), F32),
        compiler_params=_cparams(("parallel",)), name="final_norm")(x, g.reshape(1, D))


def _pick_tile(n, options):
    for t in options:
        if n % t == 0:
            return t
    raise ValueError(f"no tile for {n}")


def _norm_mod_rows(x_ref, g_ref, tab_ref, h_ref, *, tm, nb, ncb, ctx_row, which):
    i = pl.program_id(0)
    for c in range(tm // ROW_BLOCK):
        idx = _table_row(i * (tm // ROW_BLOCK) + c, nb, ncb, ctx_row)
        rows = slice(c * ROW_BLOCK, (c + 1) * ROW_BLOCK)
        x = x_ref[rows, :]
        y = x * lax.rsqrt(jnp.mean(x * x, axis=-1, keepdims=True) + NORM_EPS) * g_ref[...]
        shift = tab_ref[pl.ds(idx * 6 + which, 1), :]
        scale = tab_ref[pl.ds(idx * 6 + which + 1, 1), :]
        h_ref[rows, :] = (y * (1.0 + scale) + shift).astype(h_ref.dtype)


def _in_proj_kernel(x_ref, g_ref, tab_ref, w_ref, og_ref, oa_ref, h_ref, *, n_g, **norm_args):
    j = pl.program_id(1)

    @pl.when(j == 0)
    def _():
        _norm_mod_rows(x_ref, g_ref, tab_ref, h_ref, **norm_args)

    acc = jnp.dot(h_ref[...], w_ref[...], preferred_element_type=F32)

    @pl.when(j < n_g)
    def _():
        og_ref[...] = acc

    @pl.when(j >= n_g)
    def _():
        oa_ref[...] = acc.astype(oa_ref.dtype)


def _in_proj(x, g, tab, w, *, n_cols_g, nb, ncb, ctx_row, which):
    M, D = x.shape
    N = w.shape[1]
    tm = _pick_tile(M, (1024, 512, 256))
    tn = 768
    assert n_cols_g % tn == 0 and (N - n_cols_g) % tn == 0
    n_g = n_cols_g // tn
    kern = functools.partial(_in_proj_kernel, n_g=n_g, tm=tm, nb=nb, ncb=ncb, ctx_row=ctx_row,
                             which=which)
    return pl.pallas_call(
        kern, grid=(M // tm, N // tn),
        in_specs=[pl.BlockSpec((tm, D), lambda i, j: (i, 0)),
                  pl.BlockSpec((1, D), lambda i, j: (0, 0)),
                  pl.BlockSpec(tab.shape, lambda i, j: (0, 0)),
                  pl.BlockSpec((D, tn), lambda i, j: (0, j))],
        out_specs=[pl.BlockSpec((tm, tn), lambda i, j: (i, jnp.minimum(j, n_g - 1))),
                   pl.BlockSpec((tm, tn), lambda i, j: (i, jnp.maximum(j - n_g, 0)))],
        out_shape=[jax.ShapeDtypeStruct((M, n_cols_g), F32),
                   jax.ShapeDtypeStruct((M, N - n_cols_g), BF16)],
        scratch_shapes=[pltpu.VMEM((tm, D), BF16)],
        compiler_params=_cparams(("parallel", "arbitrary")), name="in_proj")(
            x, g.reshape(1, D), tab, w)


def _norm_swiglu_kernel(x_ref, g_ref, tab_ref, w1_ref, w3_ref, o_ref, h_ref, **norm_args):
    @pl.when(pl.program_id(1) == 0)
    def _():
        _norm_mod_rows(x_ref, g_ref, tab_ref, h_ref, **norm_args)

    a = h_ref[...]
    h1 = jnp.dot(a, w1_ref[...], preferred_element_type=F32)
    h3 = jnp.dot(a, w3_ref[...], preferred_element_type=F32)
    o_ref[...] = (h1 * jax.nn.sigmoid(h1) * h3).astype(o_ref.dtype)


def _norm_swiglu(x, g, tab, w1, w3, *, nb, ncb, ctx_row, which):
    M, D = x.shape
    N = w1.shape[1]
    tm = _pick_tile(M, (1024, 512, 256))
    tn = _pick_tile(N, (512, 256, 128))
    kern = functools.partial(_norm_swiglu_kernel, tm=tm, nb=nb, ncb=ncb, ctx_row=ctx_row, which=which)
    return pl.pallas_call(
        kern, grid=(M // tm, N // tn),
        in_specs=[pl.BlockSpec((tm, D), lambda i, j: (i, 0)),
                  pl.BlockSpec((1, D), lambda i, j: (0, 0)),
                  pl.BlockSpec(tab.shape, lambda i, j: (0, 0)),
                  pl.BlockSpec((D, tn), lambda i, j: (0, j)),
                  pl.BlockSpec((D, tn), lambda i, j: (0, j))],
        out_specs=pl.BlockSpec((tm, tn), lambda i, j: (i, j)),
        out_shape=jax.ShapeDtypeStruct((M, N), BF16),
        scratch_shapes=[pltpu.VMEM((tm, D), BF16)],
        compiler_params=_cparams(("parallel", "arbitrary")), name="norm_swiglu")(
            x, g.reshape(1, D), tab, w1, w3)


def _mm_res_kernel(*refs, n_a, tm, tn, nb, ncb, ctx_row, which):
    a_refs = refs[:n_a]
    w_ref, res_ref, tab_ref, o_ref = refs[n_a:]
    i = pl.program_id(0)
    j = pl.program_id(1)
    acc = None
    k0 = 0
    for a_ref in a_refs:
        kk = a_ref.shape[1]
        part = jnp.dot(a_ref[...], w_ref[k0:k0 + kk, :], preferred_element_type=F32)
        acc = part if acc is None else acc + part
        k0 += kk
    for c in range(tm // ROW_BLOCK):
        idx = _table_row(i * (tm // ROW_BLOCK) + c, nb, ncb, ctx_row)
        gate = tab_ref[pl.ds(idx * 6 + which, 1), pl.ds(pl.multiple_of(j * tn, tn), tn)]
        rows = slice(c * ROW_BLOCK, (c + 1) * ROW_BLOCK)
        o_ref[rows, :] = res_ref[rows, :] + gate * acc[rows, :]


def _mm_res(a_list, w, res, tab, *, nb, ncb, ctx_row, which, tm, tn,
            in_row_map=None, out_rows=None):
    K, N = w.shape
    M = res.shape[0] if out_rows is None else out_rows
    rm = (lambda i: i) if in_row_map is None else in_row_map
    kern = functools.partial(_mm_res_kernel, n_a=len(a_list), tm=tm, tn=tn, nb=nb, ncb=ncb,
                             ctx_row=ctx_row, which=which)

    def a_spec(a):
        if a.shape[0] == res.shape[0]:
            return pl.BlockSpec((tm, a.shape[1]), lambda i, j: (rm(i), 0))
        assert a.shape[0] == M
        return pl.BlockSpec((tm, a.shape[1]), lambda i, j: (i, 0))

    in_specs = [a_spec(a) for a in a_list]
    in_specs += [pl.BlockSpec((K, tn), lambda i, j: (0, j)),
                 pl.BlockSpec((tm, tn), lambda i, j: (rm(i), j)),
                 pl.BlockSpec(tab.shape, lambda i, j: (0, 0))]
    return pl.pallas_call(
        kern, grid=(M // tm, N // tn), in_specs=in_specs,
        out_specs=pl.BlockSpec((tm, tn), lambda i, j: (i, j)),
        out_shape=jax.ShapeDtypeStruct((M, N), F32),
        compiler_params=_cparams(("parallel", "parallel")), name="mm_res")(*a_list, w, res, tab)


def _ones_column(rows):
    lane = lax.broadcasted_iota(jnp.int32, (rows, LANES), 1)
    return jnp.where(lane == 0, 1.0, 0.0).astype(BF16)


def _rope_lanes(x, cos, sin, rot):
    return x * cos + jnp.dot(x.astype(BF16), rot, preferred_element_type=F32) * sin


def _mean_sq(x):
    w = x.shape[1]
    return jnp.dot((x * x).astype(BF16), jnp.ones((w, w), BF16), preferred_element_type=F32) * (1.0 / w)


def _rope_tables(S, C, rot_dim):
    rows = S // GRID_W
    row = jnp.repeat(jnp.arange(rows), GRID_W).astype(F32)
    col = jnp.tile(jnp.arange(GRID_W), rows).astype(F32)
    half = rot_dim // 2
    inv_freq = ROPE_THETA ** (-jnp.arange(0, half, 2, dtype=F32) / half)
    ang_r = row[:, None] * inv_freq
    ang_c = col[:, None] * inv_freq
    ang = jnp.concatenate([ang_r, ang_r, ang_c, ang_c], axis=-1)
    cos, sin = jnp.cos(ang), jnp.sin(ang)
    quarter = rot_dim // 4
    pad = LANES - rot_dim

    def full(t, ctx_val):
        t = jnp.pad(t, ((0, 0), (0, pad)), constant_values=ctx_val)
        return jnp.concatenate([jnp.full((C, LANES), ctx_val, F32), t], axis=0)

    j = np.arange(rot_dim)
    even = (j // quarter) % 2 == 0
    rot = np.zeros((LANES, LANES), np.float32)
    rot[np.where(even, j + quarter, j - quarter), j] = np.where(even, -1.0, 1.0)
    return full(cos, 1.0), full(sin, 0.0), jnp.asarray(rot, BF16)


def _mla_prep_kernel(p_ref, qg_ref, kvg_ref, wq_ref, wkv_ref, cos_ref, sin_ref, rot_ref,
                     q_ref, k_ref, v_ref, *, scale):
    cos, sin, rot = cos_ref[...], sin_ref[...], rot_ref[...]
    cq = p_ref[:, 0:MLA_Q_RANK].astype(F32)
    qn = cq * lax.rsqrt(jnp.mean(cq * cq, axis=-1, keepdims=True) + NORM_EPS) * qg_ref[...]
    q_raw = jnp.dot(qn.astype(BF16), wq_ref[...], preferred_element_type=F32)
    ckv = p_ref[:, MLA_Q_RANK:MLA_Q_RANK + MLA_KV_RANK].astype(F32)
    kvn = ckv * lax.rsqrt(jnp.mean(ckv * ckv, axis=-1, keepdims=True) + NORM_EPS) * kvg_ref[...]
    kv = jnp.dot(kvn.astype(BF16), wkv_ref[...], preferred_element_type=F32)
    kr = p_ref[:, MLA_Q_RANK + MLA_KV_RANK:MLA_Q_RANK + MLA_KV_RANK + LANES].astype(F32)
    kr = _rope_lanes(kr, cos, sin, rot).astype(BF16)
    for h in range(MLA_HEADS):
        c0 = 2 * LANES * h
        q_ref[:, c0:c0 + LANES] = (q_raw[:, c0:c0 + LANES] * scale).astype(BF16)
        qr = _rope_lanes(q_raw[:, c0 + LANES:c0 + 2 * LANES], cos, sin, rot)
        q_ref[:, c0 + LANES:c0 + 2 * LANES] = (qr * scale).astype(BF16)
        k_ref[:, c0:c0 + LANES] = kv[:, LANES * h:LANES * (h + 1)].astype(BF16)
        k_ref[:, c0 + LANES:c0 + 2 * LANES] = kr
    ones_col = _ones_column(p_ref.shape[0])
    for h in range(MLA_HEADS):
        c0 = 2 * LANES * h
        v_ref[:, c0:c0 + LANES] = kv[:, (MLA_HEADS + h) * LANES:(MLA_HEADS + h + 1) * LANES].astype(BF16)
        v_ref[:, c0 + LANES:c0 + 2 * LANES] = ones_col


def _mla_prep(P, qg, kvg, wq_p, wkv_p, tabs, nbt):
    R = P.shape[0]
    width = MLA_HEADS * 2 * LANES
    kern = functools.partial(_mla_prep_kernel, scale=float((MLA_NOPE + MLA_ROPE) ** -0.5))
    tab_spec = pl.BlockSpec((ROW_BLOCK, LANES), lambda i: (i % nbt, 0))
    const = lambda a: pl.BlockSpec(a.shape, lambda i: (0, 0))
    return pl.pallas_call(
        kern, grid=(R // ROW_BLOCK,),
        in_specs=[pl.BlockSpec((ROW_BLOCK, 1024), lambda i: (i, P_MLA // 1024)),
                  const(qg), const(kvg), const(wq_p), const(wkv_p), tab_spec, tab_spec, const(tabs[2])],
        out_specs=[pl.BlockSpec((ROW_BLOCK, width), lambda i: (i, 0)),
                   pl.BlockSpec((ROW_BLOCK, width), lambda i: (i, 0)),
                   pl.BlockSpec((ROW_BLOCK, width), lambda i: (i, 0))],
        out_shape=[jax.ShapeDtypeStruct((R, width), BF16), jax.ShapeDtypeStruct((R, width), BF16),
                   jax.ShapeDtypeStruct((R, width), BF16)],
        compiler_params=_cparams(("parallel",)), name="mla_prep")(P, qg, kvg, wq_p, wkv_p, *tabs)


def _head_prep_kernel(p_ref, qg_ref, kg_ref, cos_ref, sin_ref, rot_ref, o_ref, *, norm, scale,
                      n_q, n_k):
    cos, sin, rot = cos_ref[...], sin_ref[...], rot_ref[...]
    for h in range(n_q + n_k):
        x = p_ref[:, LANES * h:LANES * (h + 1)].astype(F32)
        if norm:
            g = qg_ref[...] if h < n_q else kg_ref[...]
            x = x * lax.rsqrt(_mean_sq(x) + NORM_EPS) * g
        x = _rope_lanes(x, cos, sin, rot)
        if h < n_q:
            x = x * scale
        c0 = LANES * h if h < n_q else LANES * (h + 2 * n_k)
        o_ref[:, c0:c0 + LANES] = x.astype(BF16)
    v_in = LANES * (n_q + n_k)
    v_out = LANES * n_q
    ones_col = _ones_column(p_ref.shape[0])
    for h in range(n_k):
        o_ref[:, v_out + 2 * LANES * h:v_out + 2 * LANES * h + LANES] = (
            p_ref[:, v_in + LANES * h:v_in + LANES * (h + 1)])
        o_ref[:, v_out + 2 * LANES * h + LANES:v_out + 2 * LANES * (h + 1)] = ones_col


def _head_prep(P, col0, qg, kg, tabs, nbt, *, norm, n_q, n_k):
    R = P.shape[0]
    kern = functools.partial(_head_prep_kernel, norm=norm, scale=float(HEAD_DIM ** -0.5),
                             n_q=n_q, n_k=n_k)
    tab_spec = pl.BlockSpec((ROW_BLOCK, LANES), lambda i: (i % nbt, 0))
    const = lambda a: pl.BlockSpec(a.shape, lambda i: (0, 0))
    return pl.pallas_call(
        kern, grid=(R // ROW_BLOCK,),
        in_specs=[pl.BlockSpec((ROW_BLOCK, 1024), lambda i: (i, col0 // 1024)),
                  const(qg), const(kg), tab_spec, tab_spec, const(tabs[2])],
        out_specs=pl.BlockSpec((ROW_BLOCK, 1024 + n_k * LANES), lambda i: (i, 0)),
        out_shape=jax.ShapeDtypeStruct((R, 1024 + n_k * LANES), BF16),
        compiler_params=_cparams(("parallel",)), name="head_prep")(P, qg, kg, *tabs)


def _attn_kernel(q_ref, k_ref, v_ref, o_ref, *, HS, G, dq, dv, C, T, ctx_tile):
    def run(nk):
        nt = (((1,), (1,)), ((), ()))
        chains = [(hs, g) for hs in range(HS) for g in range(G)]
        s = []
        for hs, g in chains:
            q = q_ref[0, :, (hs * G + g) * dq:(hs * G + g + 1) * dq]
            k = k_ref[0, :nk, hs * dq:(hs + 1) * dq]
            s.append(lax.dot_general(q, k, nt, preferred_element_type=F32).astype(BF16))
        p = [jnp.exp(x - jnp.max(x, axis=-1, keepdims=True)) for x in s]
        for (hs, g), pc in zip(chains, p):
            o = jnp.dot(pc, v_ref[0, :nk, hs * 2 * dv:(hs + 1) * 2 * dv], preferred_element_type=F32)
            c0 = (hs * G + g) * dv
            o_ref[0, :, c0:c0 + dv] = (o[:, :dv] / o[:, dv:dv + 1]).astype(o_ref.dtype)

    if ctx_tile:
        qi = pl.program_id(2)
        pl.when(qi == 0)(lambda: run(C))
        pl.when(qi > 0)(lambda: run(T))
    else:
        run(T)


def _attention(q, k, v, *, Hk, G, dq, dv, q_col0, k_col0, v_col0, C, with_ctx):
    Bn, T, _ = q.shape
    ncb = C // ROW_BLOCK
    assert ncb == 1
    nq = T // ROW_BLOCK if with_ctx else (T - C) // ROW_BLOCK
    qoff = 0 if with_ctx else ncb
    HS = ATTN_KV_HEADS_PER_STEP
    assert Hk % HS == 0
    kern = functools.partial(_attn_kernel, HS=HS, G=G, dq=dq, dv=dv, C=C, T=T, ctx_tile=with_ctx)
    qw, kw, vw = HS * G * dq, HS * dq, HS * 2 * dv
    assert q_col0 % qw == 0 and k_col0 % kw == 0 and v_col0 % vw == 0
    return pl.pallas_call(
        kern, grid=(Bn, Hk // HS, nq),
        in_specs=[pl.BlockSpec((1, ROW_BLOCK, qw), lambda b, h, i: (b, i + qoff, q_col0 // qw + h)),
                  pl.BlockSpec((1, T, kw), lambda b, h, i: (b, 0, k_col0 // kw + h)),
                  pl.BlockSpec((1, T, vw), lambda b, h, i: (b, 0, v_col0 // vw + h))],
        out_specs=pl.BlockSpec((1, ROW_BLOCK, HS * G * dv), lambda b, h, i: (b, i, h)),
        out_shape=jax.ShapeDtypeStruct((Bn, nq * ROW_BLOCK, Hk * G * dv), BF16),
        compiler_params=_cparams(("parallel", "parallel", "parallel")), name="attention")(q, k, v)


def _swa_attention(qkv, sink, *, C, with_ctx):
    Bn, T, _ = qkv.shape
    Hq, Hk, d = SWA_HEADS, SWA_KV_HEADS, HEAD_DIM
    ncb = C // ROW_BLOCK
    assert ncb == 1 and T - C >= ROW_BLOCK + 2 * SWA_WINDOW
    nq = T // ROW_BLOCK if with_ctx else (T - C) // ROW_BLOCK
    qoff = 0 if with_ctx else ncb
    kern = functools.partial(_swa_kernel, C=C, S=T - C, ctx_tile=with_ctx, qoff=qoff)
    k_blk = (Hq * d + Hk * 2 * d) // (Hk * d)
    return pl.pallas_call(
        kern, grid=(Bn, nq),
        in_specs=[pl.BlockSpec(memory_space=pltpu.SMEM),
                  pl.BlockSpec((1, ROW_BLOCK, Hq * d), lambda b, i: (b, i + qoff, 0)),
                  pl.BlockSpec((1, T, Hk * d), lambda b, i: (b, 0, k_blk)),
                  pl.BlockSpec((1, T, Hk * 2 * d), lambda b, i: (b, 0, 1))],
        out_specs=pl.BlockSpec((1, ROW_BLOCK, Hq * d), lambda b, i: (b, i, 0)),
        out_shape=jax.ShapeDtypeStruct((Bn, nq * ROW_BLOCK, Hq * d), BF16),
        compiler_params=_cparams(("parallel", "parallel")), name="swa")(sink, qkv, qkv, qkv)


def _swa_kernel(sink_ref, q_ref, k_ref, v_ref, o_ref, *, C, S, ctx_tile, qoff):
    qi = pl.program_id(1) + qoff
    d = HEAD_DIM
    G = SWA_HEADS // SWA_KV_HEADS
    W = SWA_WINDOW
    WIN = ROW_BLOCK + 2 * W
    nt = (((1,), (1,)), ((), ()))
    heads = range(SWA_HEADS)

    def q_of(h):
        return q_ref[0, :, h * d:(h + 1) * d]

    def finish(parts):
        m16 = []
        for h in heads:
            m = jnp.full((ROW_BLOCK, 1), sink_ref[h], F32)
            for s, _ in parts[h]:
                m = jnp.maximum(m, jnp.max(s, axis=-1, keepdims=True).astype(F32))
            m16.append(m.astype(BF16))
        for h in heads:
            o = None
            for s, v in parts[h]:
                pv = jnp.dot(jnp.exp(s - m16[h]), v, preferred_element_type=F32)
                o = pv if o is None else o + pv
            den = jnp.exp(sink_ref[h] - m16[h].astype(F32)) + o[:, d:d + 1]
            o_ref[0, :, h * d:(h + 1) * d] = (o[:, :d] / den).astype(o_ref.dtype)

    def ctx_scores(h):
        kc = k_ref[0, :C, (h // G) * d:(h // G + 1) * d]
        return (lax.dot_general(q_of(h), kc, nt, preferred_element_type=F32).astype(BF16),
                v_ref[0, :C, (h // G) * 2 * d:(h // G + 1) * 2 * d])

    def run_ctx():
        finish([[ctx_scores(h)] for h in heads])

    def run_lat():
        q0 = (qi - (C // ROW_BLOCK)) * ROW_BLOCK
        ws = jnp.clip(q0 - W, 0, S - WIN)
        start = pl.multiple_of(C + ws, LANES)
        qpos = q0 + lax.broadcasted_iota(jnp.int32, (ROW_BLOCK, WIN), 0)
        kpos = ws + lax.broadcasted_iota(jnp.int32, (ROW_BLOCK, WIN), 1)
        valid = jnp.abs(kpos - qpos) <= W
        parts = []
        for h in heads:
            kw = k_ref[0, pl.ds(start, WIN), (h // G) * d:(h // G + 1) * d]
            vw = v_ref[0, pl.ds(start, WIN), (h // G) * 2 * d:(h // G + 1) * 2 * d]
            s_loc = jnp.where(valid, lax.dot_general(q_of(h), kw, nt, preferred_element_type=F32),
                              NEG_INF).astype(BF16)
            parts.append([(s_loc, vw), ctx_scores(h)])
        finish(parts)

    if ctx_tile:
        pl.when(qi == 0)(run_ctx)
        pl.when(qi > 0)(run_lat)
    else:
        run_lat()


def _gdn_prep_kernel(x_ref, w_ref, o_ref, pad_ref, *, C, T):
    j = pl.program_id(1)
    g = SUBLANES
    half = (GDN_CONV - 1) // 2
    assert half <= g
    zeros = jnp.zeros((g, LANES), F32)
    segments = ((0, C, g), (C, T - C, 2 * g + C))
    pad_ref[0:g, :] = zeros
    for lo, n, base in segments:
        pad_ref[base:base + n, :] = x_ref[0, lo:lo + n, :]
        pad_ref[base + n:base + n + g, :] = zeros
    nh = GDN_HEADS
    for lo, n, base in segments:
        acc = None
        for tap in range(GDN_CONV):
            term = pad_ref[base + tap - half:base + tap - half + n, :] * w_ref[tap:tap + 1, :]
            acc = term if acc is None else acc + term
        y = acc * jax.nn.sigmoid(acc)
        inv = lax.rsqrt(_mean_sq(y) * float(LANES) + 1e-6)
        f = jnp.where(j < 2 * nh, inv, 1.0) * jnp.where(j < nh, float(HEAD_DIM ** -0.5), 1.0)
        o_ref[0, lo:lo + n, :] = y * f


def _gdn_prep(P3, conv_w8, C):
    Bn, T, _ = P3.shape
    nblk = 3 * GDN_HEADS
    kern = functools.partial(_gdn_prep_kernel, C=C, T=T)
    return pl.pallas_call(
        kern, grid=(Bn, nblk),
        in_specs=[pl.BlockSpec((1, T, LANES), lambda b, j: (b, 0, P_GDN // LANES + j)),
                  pl.BlockSpec((8, LANES), lambda b, j: (0, j))],
        out_specs=pl.BlockSpec((1, T, LANES), lambda b, j: (b, 0, j)),
        out_shape=jax.ShapeDtypeStruct((Bn, T, nblk * LANES), F32),
        scratch_shapes=[pltpu.VMEM((T + 3 * SUBLANES, LANES), F32)],
        compiler_params=_cparams(("parallel", "parallel")), name="gdn_prep")(P3, conv_w8)


_GDN_INV_BASE = 8
GDN_BATCH_PER_STEP = 2


def _dot16(a, b):
    return jnp.dot(a, b, preferred_element_type=F32)


def _dot16_nt(a, b):
    return lax.dot_general(a, b, (((1,), (1,)), ((), ())), preferred_element_type=F32)


def _dot16_tn(a, b):
    return lax.dot_general(a, b, (((0,), (0,)), ((), ())), preferred_element_type=F32)


def _mask_dot(mask16, x):
    hi = x.astype(BF16)
    r1 = x - hi.astype(F32)
    mid = r1.astype(BF16)
    lo = (r1 - mid.astype(F32)).astype(BF16)
    w = x.shape[1]
    y = jnp.dot(mask16, jnp.concatenate([hi, mid, lo], axis=1), preferred_element_type=F32)
    return y[:, :w] + y[:, w:2 * w] + y[:, 2 * w:]


def _gdn_chunk_kernel(qf_ref, qb_ref, abf_ref, abb_ref, par_ref, of_ref, ob_ref, s_ref):
    Cc = GDN_CHUNK
    H = GDN_HEADS
    d = HEAD_DIM

    @pl.when(pl.program_id(1) == 0)
    def _():
        s_ref[...] = jnp.zeros_like(s_ref)

    ri = lax.broadcasted_iota(jnp.int32, (Cc, Cc), 0)
    ci = lax.broadcasted_iota(jnp.int32, (Cc, Cc), 1)
    eye = (ri == ci).astype(F32)
    neg_a = par_ref[0:1, :]
    dt_b = par_ref[1:2, :]
    base = _GDN_INV_BASE
    diag_blk = (ri // base) == (ci // base)
    off_blks = []
    s = base
    while s < Cc:
        off_blks.append(jnp.logical_and((ri // (2 * s)) == (ci // (2 * s)), (ri // s) != (ci // s)))
        s *= 2

    dirs = []
    for bb, direction in [(bb, direction) for bb in range(qf_ref.shape[0]) for direction in range(2)]:
        ab = (abf_ref if direction == 0 else abb_ref)[bb]
        if direction == 0:
            incl, incl_t, strict, last = ri >= ci, ri <= ci, ri > ci, Cc - 1
        else:
            incl, incl_t, strict, last = ri <= ci, ri >= ci, ri < ci, 0
        z = ab + dt_b
        sp = jnp.maximum(z, 0.0) + jnp.log1p(jnp.exp(-jnp.abs(z)))
        gl_all = neg_a * sp
        gc_all = _mask_dot(incl.astype(BF16), gl_all)
        g_tot = gc_all[last:last + 1, :]
        gc_sq = jnp.concatenate([gc_all, jnp.zeros((LANES - Cc, LANES), F32)], axis=0)
        dirs.append(dict(
            bb=bb, direction=direction,
            x_ref=qf_ref if direction == 0 else qb_ref, o_ref=of_ref if direction == 0 else ob_ref,
            incl=incl, strict=strict, gc_all=gc_all,
            gct_all=gc_sq.T[:, :Cc],
            beta_all=jax.nn.sigmoid(ab),
            eg_all=jnp.exp(gc_all), ekg_all=jnp.exp(g_tot - gc_all), egt_all=jnp.exp(g_tot)))

    units = [(dd, dd["direction"] * H + h, h) for dd in dirs for h in range(H)]
    U = range(len(units))

    k16, kb, decay, a_mat = [], [], [], []
    for dd, lane, h in units:
        k = dd["x_ref"][dd["bb"],:, (H + h) * d:(H + h + 1) * d]
        beta = dd["beta_all"][:, 2 * H + lane:2 * H + lane + 1]
        diff = dd["gc_all"][:, lane:lane + 1] - dd["gct_all"][lane:lane + 1, :]
        decay.append(jnp.where(dd["incl"], jnp.exp(jnp.where(dd["incl"], diff, 0.0)), 0.0))
        kb.append(k * beta)
        k16.append(k.astype(BF16))
    for i in U:
        a_mat.append(jnp.where(units[i][0]["strict"],
                               _dot16_nt(kb[i].astype(BF16), k16[i]) * decay[i], 0.0))

    npow = [jnp.where(diag_blk, a, 0.0).astype(BF16) for a in a_mat]
    t = [eye - jnp.where(diag_blk, a, 0.0) for a in a_mat]
    sq = 2 * base
    while sq > 4:
        npow = [_dot16(n, n).astype(BF16) for n in npow]
        t = [t[i] + _dot16(t[i].astype(BF16), npow[i]) for i in U]
        sq //= 2
    for off_blk in off_blks:
        t16 = [x.astype(BF16) for x in t]
        xo = [_dot16(t16[i], jnp.where(off_blk, a_mat[i], 0.0).astype(BF16)).astype(BF16) for i in U]
        t = [t[i] - _dot16(xo[i], t16[i]) for i in U]

    uw, intra, qg = [], [], []
    for i in U:
        dd, lane, h = units[i]
        q = dd["x_ref"][dd["bb"],:, h * d:(h + 1) * d]
        v = dd["x_ref"][dd["bb"],:, (2 * H + h) * d:(2 * H + h + 1) * d]
        beta = dd["beta_all"][:, 2 * H + lane:2 * H + lane + 1]
        eg = dd["eg_all"][:, lane:lane + 1]
        rhs = jnp.concatenate([v * beta, kb[i] * eg], axis=1).astype(BF16)
        uw.append(_dot16(t[i].astype(BF16), rhs))
        intra.append(jnp.where(dd["incl"], _dot16_nt(q.astype(BF16), k16[i]) * decay[i], 0.0)
                     .astype(BF16))
        qg.append((q * eg).astype(BF16))
    state = [s_ref[units[i][0]["bb"] * 2 * H + units[i][1]] for i in U]
    s16 = [x.astype(BF16) for x in state]
    v_new = [uw[i][:, :d] - _dot16(uw[i][:, d:].astype(BF16), s16[i]) for i in U]
    vn16 = [x.astype(BF16) for x in v_new]
    for i in U:
        dd, lane, h = units[i]
        dd["o_ref"][dd["bb"], :, h * d:(h + 1) * d] = (_dot16(qg[i], s16[i])
                                                       + _dot16(intra[i], vn16[i]))
    for i in U:
        dd, lane, h = units[i]
        k = dd["x_ref"][dd["bb"],:, (H + h) * d:(H + h + 1) * d]
        kg = (k * dd["ekg_all"][:, lane:lane + 1]).astype(BF16)
        s_ref[dd["bb"] * 2 * H + lane] = (state[i] * dd["egt_all"][:, lane:lane + 1]
                                          + _dot16_tn(kg, vn16[i]))


def _gdn_chunks(qkv, P3, par, C):
    Bn, T, _ = qkv.shape
    Cc = GDN_CHUNK
    nc = T // Cc
    ncc = C // Cc
    H = GDN_HEADS

    def bwd_chunk(s):
        return jnp.where(s < ncc, ncc - 1 - s, nc + ncc - 1 - s)

    ab_blk = P_AB // LANES
    nbb = _pick_tile(Bn, (GDN_BATCH_PER_STEP, 1))
    return pl.pallas_call(
        _gdn_chunk_kernel, grid=(Bn // nbb, nc),
        in_specs=[pl.BlockSpec((nbb, Cc, 3 * H * HEAD_DIM), lambda b, s: (b, s, 0)),
                  pl.BlockSpec((nbb, Cc, 3 * H * HEAD_DIM), lambda b, s: (b, bwd_chunk(s), 0)),
                  pl.BlockSpec((nbb, Cc, LANES), lambda b, s: (b, s, ab_blk)),
                  pl.BlockSpec((nbb, Cc, LANES), lambda b, s: (b, bwd_chunk(s), ab_blk)),
                  pl.BlockSpec((8, LANES), lambda b, s: (0, 0))],
        out_specs=[pl.BlockSpec((nbb, Cc, H * HEAD_DIM), lambda b, s: (b, s, 0)),
                   pl.BlockSpec((nbb, Cc, H * HEAD_DIM), lambda b, s: (b, bwd_chunk(s), 0))],
        out_shape=[jax.ShapeDtypeStruct((Bn, T, H * HEAD_DIM), F32),
                   jax.ShapeDtypeStruct((Bn, T, H * HEAD_DIM), F32)],
        scratch_shapes=[pltpu.VMEM((nbb * 2 * H, HEAD_DIM, HEAD_DIM), F32)],
        compiler_params=_cparams(("parallel", "arbitrary")), name="gdn_chunks")(
            qkv, qkv, P3, P3, par)


def _gdn_post_kernel(of_ref, ob_ref, z_ref, g_ref, o_ref):
    d = HEAD_DIM
    for h in range(GDN_HEADS):
        o = of_ref[:, h * d:(h + 1) * d] + ob_ref[:, h * d:(h + 1) * d]
        y = o * lax.rsqrt(_mean_sq(o) + NORM_EPS) * g_ref[...]
        z = z_ref[:, h * d:(h + 1) * d]
        o_ref[:, h * d:(h + 1) * d] = (y * (z * jax.nn.sigmoid(z))).astype(o_ref.dtype)


def _gdn_post(of, ob, P, g):
    R, W = of.shape
    zblk = (P_GDN + 3 * W) // W
    return pl.pallas_call(
        _gdn_post_kernel, grid=(R // ROW_BLOCK,),
        in_specs=[pl.BlockSpec((ROW_BLOCK, W), lambda i: (i, 0)),
                  pl.BlockSpec((ROW_BLOCK, W), lambda i: (i, 0)),
                  pl.BlockSpec((ROW_BLOCK, W), lambda i: (i, zblk)),
                  pl.BlockSpec((1, HEAD_DIM), lambda i: (0, 0))],
        out_specs=pl.BlockSpec((ROW_BLOCK, W), lambda i: (i, 0)),
        out_shape=jax.ShapeDtypeStruct((R, W), BF16),
        compiler_params=_cparams(("parallel",)), name="gdn_post")(of, ob, P, g)


def _gather_rows_kernel(tok_ref, tok_next_ref, h_hbm, o_ref, buf, sem):
    i = pl.program_id(0)
    nt = pl.num_programs(0)
    n = buf.shape[1]
    slot = i % 2

    def issue(t_ref, s):
        def body(r, c):
            pltpu.make_async_copy(h_hbm.at[pl.ds(t_ref[0, 0, r], 1), :],
                                  buf.at[s, pl.ds(r, 1), :], sem.at[s]).start()
            return c
        lax.fori_loop(0, n, body, 0, unroll=8)

    @pl.when(i == 0)
    def _():
        issue(tok_ref, 0)

    @pl.when(i + 1 < nt)
    def _():
        issue(tok_next_ref, 1 - slot)

    pltpu.make_async_copy(h_hbm.at[pl.ds(0, n), :], buf.at[slot], sem.at[slot]).wait()
    o_ref[...] = buf[slot].astype(o_ref.dtype)


def _gather_rows(h, src_tok):
    D = h.shape[1]
    n = src_tok.shape[0]
    nt = n // ROW_BLOCK
    return pl.pallas_call(
        _gather_rows_kernel, grid=(nt,),
        in_specs=[pl.BlockSpec((1, 1, ROW_BLOCK), lambda i: (i, 0, 0), memory_space=pltpu.SMEM),
                  pl.BlockSpec((1, 1, ROW_BLOCK), lambda i: (jnp.minimum(i + 1, nt - 1), 0, 0),
                               memory_space=pltpu.SMEM),
                  pl.BlockSpec(memory_space=pl.ANY)],
        out_specs=pl.BlockSpec((ROW_BLOCK, D), lambda i: (i, 0)),
        out_shape=jax.ShapeDtypeStruct((n, D), BF16),
        scratch_shapes=[pltpu.VMEM((2, ROW_BLOCK, D), F32), pltpu.SemaphoreType.DMA((2,))],
        compiler_params=_cparams(("arbitrary",)), name="moe_gather")(
            src_tok.reshape(nt, 1, ROW_BLOCK), src_tok.reshape(nt, 1, ROW_BLOCK), h)


def _gmm_kernel(te_ref, tv_ref, x_ref, w1_ref, w3_ref, w2_ref, o_ref, acc_ref):
    i = pl.program_id(0)
    f = pl.program_id(1)
    nf = pl.num_programs(1)

    @pl.when(tv_ref[i] == 1)
    def _():
        x = x_ref[...]
        h1 = jnp.dot(x, w1_ref[0], preferred_element_type=F32)
        h3 = jnp.dot(x, w3_ref[0], preferred_element_type=F32)
        hid = (h1 * jax.nn.sigmoid(h1) * h3).astype(BF16)
        part = jnp.dot(hid, w2_ref[0], preferred_element_type=F32)

        @pl.when(f == 0)
        def _():
            acc_ref[...] = part

        @pl.when(jnp.logical_and(f > 0, f < nf - 1))
        def _():
            acc_ref[...] += part

        @pl.when(f == nf - 1)
        def _():
            o_ref[...] = acc_ref[...] + part

    @pl.when(jnp.logical_and(tv_ref[i] == 0, f == nf - 1))
    def _():
        o_ref[...] = jnp.zeros_like(o_ref)


def _gmm(xs, w1, w3, w2, tile_e, tile_valid):
    n, D = xs.shape
    E, _, F = w1.shape
    tf = 1024
    nt = n // MOE_TILE
    grid_spec = pltpu.PrefetchScalarGridSpec(
        num_scalar_prefetch=2, grid=(nt, F // tf),
        in_specs=[pl.BlockSpec((MOE_TILE, D), lambda i, f, te, tv: (i, 0)),
                  pl.BlockSpec((1, D, tf), lambda i, f, te, tv: (te[i], 0, f)),
                  pl.BlockSpec((1, D, tf), lambda i, f, te, tv: (te[i], 0, f)),
                  pl.BlockSpec((1, tf, D), lambda i, f, te, tv: (te[i], f, 0))],
        out_specs=pl.BlockSpec((MOE_TILE, D), lambda i, f, te, tv: (i, 0)),
        scratch_shapes=[pltpu.VMEM((MOE_TILE, D), F32)])
    return pl.pallas_call(
        _gmm_kernel, grid_spec=grid_spec,
        out_shape=jax.ShapeDtypeStruct((n, D), F32),
        compiler_params=_cparams(("parallel", "arbitrary")), name="moe_gmm")(
            tile_e, tile_valid, xs, w1, w3, w2)


def _combine_kernel(pos_ref, pos_next_ref, ys_hbm, x_ref, eg_ref, tab_ref, o_ref, buf, sem, *, nb,
                    ncb, ctx_row, which):
    i = pl.program_id(0)
    nt = pl.num_programs(0)
    n = buf.shape[2]
    slot = i % 2
    idx = _table_row(i, nb, ncb, ctx_row)

    def issue(p_ref, s):
        def body(r, c):
            for k in range(TOP_K):
                pltpu.make_async_copy(ys_hbm.at[pl.ds(p_ref[0, k, r], 1), :],
                                      buf.at[s, k, pl.ds(r, 1), :], sem.at[s]).start()
            return c
        lax.fori_loop(0, n, body, 0, unroll=4)

    @pl.when(i == 0)
    def _():
        issue(pos_ref, 0)

    @pl.when(i + 1 < nt)
    def _():
        issue(pos_next_ref, 1 - slot)

    for k in range(TOP_K):
        pltpu.make_async_copy(ys_hbm.at[pl.ds(0, n), :], buf.at[slot, k], sem.at[slot]).wait()
    gate = tab_ref[pl.ds(idx * 6 + which, 1), :]
    eg = eg_ref[...]
    y = eg[:, 0:1] * buf[slot, 0]
    for k in range(1, TOP_K):
        y = y + eg[:, k:k + 1] * buf[slot, k]
    o_ref[...] = x_ref[...] + gate * y


def _combine(ys, pos, egate, x, tab, *, nb, ncb, ctx_row, which):
    R, D = x.shape
    nt = R // ROW_BLOCK
    pos3 = pos.reshape(nt, ROW_BLOCK, TOP_K).transpose(0, 2, 1)
    kern = functools.partial(_combine_kernel, nb=nb, ncb=ncb, ctx_row=ctx_row, which=which)
    return pl.pallas_call(
        kern, grid=(nt,),
        in_specs=[pl.BlockSpec((1, TOP_K, ROW_BLOCK), lambda i: (i, 0, 0), memory_space=pltpu.SMEM),
                  pl.BlockSpec((1, TOP_K, ROW_BLOCK), lambda i: (jnp.minimum(i + 1, nt - 1), 0, 0),
                               memory_space=pltpu.SMEM),
                  pl.BlockSpec(memory_space=pl.ANY),
                  pl.BlockSpec((ROW_BLOCK, D), lambda i: (i, 0)),
                  pl.BlockSpec((ROW_BLOCK, LANES), lambda i: (i, 0)),
                  pl.BlockSpec(tab.shape, lambda i: (0, 0))],
        out_specs=pl.BlockSpec((ROW_BLOCK, D), lambda i: (i, 0)),
        out_shape=jax.ShapeDtypeStruct((R, D), F32),
        scratch_shapes=[pltpu.VMEM((2, TOP_K, ROW_BLOCK, D), F32), pltpu.SemaphoreType.DMA((2,))],
        compiler_params=_cparams(("arbitrary",)), name="moe_combine")(pos3, pos3, ys, x, egate, tab)


def _route(logits):
    R = logits.shape[0]
    E = N_EXPERTS
    top_v, top_i = lax.top_k(logits, TOP_K)
    gate = jax.nn.softmax(top_v, axis=-1)
    onehot = jnp.sum((top_i[:, :, None] == jnp.arange(E)[None, None, :]).astype(jnp.int32), axis=1)
    before = jnp.cumsum(onehot, axis=0) - onehot
    sizes = jnp.sum(onehot, axis=0)
    padded = ((sizes + MOE_TILE - 1) // MOE_TILE) * MOE_TILE
    ends = jnp.cumsum(padded)
    off = ends - padded
    pos = off[top_i] + jnp.take_along_axis(before, top_i, axis=1)
    nt = (TOP_K * R) // MOE_TILE + E
    n = nt * MOE_TILE
    flat = pos.reshape(-1)
    src_tok = jnp.zeros((n,), jnp.int32).at[flat].set(jnp.repeat(jnp.arange(R, dtype=jnp.int32), TOP_K))
    egate = jnp.pad(gate, ((0, 0), (0, LANES - TOP_K)))
    tile_start = jnp.arange(nt, dtype=jnp.int32) * MOE_TILE
    tile_valid = (tile_start < ends[-1]).astype(jnp.int32)
    tile_e = jnp.searchsorted(ends, tile_start, side="right").astype(jnp.int32)
    last_e = jnp.max(jnp.where(tile_valid == 1, tile_e, 0))
    tile_e = jnp.where(tile_valid == 1, tile_e, last_e)
    return pos.astype(jnp.int32), src_tok, egate, tile_e, tile_valid


def _prep_w_in(w):
    D = w.shape[0]
    z = lambda n: jnp.zeros((D, n), w.dtype)
    w_g = jnp.concatenate([w[:, 1728:3776], w[:, 3776:3792], z(PG_WIDTH - 2048 - 16)], axis=1)
    w_a = jnp.concatenate([w[:, 704:1728], w[:, 3792:4816], w[:, 0:704], z(PA_WIDTH - 2048 - 704)],
                          axis=1)
    assert w_g.shape[1] == PG_WIDTH and w_a.shape[1] == PA_WIDTH
    return jnp.concatenate([w_g, w_a], axis=1).astype(BF16)


def _prep_w_uq(w):
    r = w.shape[0]
    w = w.reshape(r, MLA_HEADS, MLA_NOPE + MLA_ROPE)
    w = jnp.pad(w, ((0, 0), (0, 0), (0, 2 * LANES - (MLA_NOPE + MLA_ROPE))))
    return w.reshape(r, MLA_HEADS * 2 * LANES).astype(BF16)


def _prep_w_ukv(w):
    r = w.shape[0]
    w = w.reshape(r, MLA_HEADS, MLA_NOPE + MLA_V)
    return jnp.concatenate([w[:, :, :MLA_NOPE].reshape(r, -1), w[:, :, MLA_NOPE:].reshape(r, -1)],
                           axis=1).astype(BF16)


def kernel(x, c, ctx, c_ctx, norm1_g, norm2_g, w_mod, b_mod, w_in, mla_q_norm_g, mla_kv_norm_g,
           mla_w_uq, mla_w_ukv, gqa_q_norm_g, gqa_k_norm_g, gdn_conv_w, gdn_a_log, gdn_dt_bias,
           gdn_norm_g, swa_sink, w_out, ffn_w1, ffn_w3, ffn_w2, moe_router, moe_w1, moe_w3, moe_w2,
           final_norm_g):
    Bn, S, D = x.shape
    C = ctx.shape[1]
    T = C + S
    R = Bn * T
    depth = w_in.shape[0]
    nbt = T // ROW_BLOCK
    ncb = C // ROW_BLOCK
    nbl = S // ROW_BLOCK

    m_rows = ((Bn + 1 + 7) // 8) * 8
    cmat = jnp.zeros((m_rows, D), F32).at[:Bn].set(c).at[Bn].set(c_ctx)
    mods = _mods(cmat, w_mod, b_mod).reshape(depth, m_rows * 6, D)

    rope_mla = _rope_tables(S, C, MLA_ROPE)
    rope_head = _rope_tables(S, C, HEAD_DIM)

    xc = jnp.concatenate([ctx, x], axis=1).reshape(R, D)
    lay = dict(nb=nbt, ncb=ncb, ctx_row=Bn)

    for layer in range(depth):
        with_ctx = layer < depth - 1
        tab = mods[layer]
        Pg, Pa = _in_proj(xc, norm1_g[layer], tab, _prep_w_in(w_in[layer]), n_cols_g=PG_WIDTH,
                          which=0, **lay)
        Pg3 = Pg.reshape(Bn, T, PG_WIDTH)

        qa, ka, va = _mla_prep(Pa, mla_q_norm_g[layer].reshape(1, -1), mla_kv_norm_g[layer].reshape(1, -1),
                               _prep_w_uq(mla_w_uq[layer]), _prep_w_ukv(mla_w_ukv[layer]), rope_mla, nbt)
        ya = _attention(qa.reshape(Bn, T, -1), ka.reshape(Bn, T, -1), va.reshape(Bn, T, -1),
                        Hk=MLA_HEADS, G=1, dq=2 * LANES, dv=MLA_V, q_col0=0, k_col0=0, v_col0=0,
                        C=C, with_ctx=with_ctx)
        qkv_b = _head_prep(Pa, P_GQA, gqa_q_norm_g[layer].reshape(1, -1), gqa_k_norm_g[layer].reshape(1, -1),
                           rope_head, nbt, norm=True, n_q=GQA_HEADS, n_k=GQA_KV_HEADS).reshape(Bn, T, -1)
        Gb = GQA_HEADS // GQA_KV_HEADS
        yb = _attention(qkv_b, qkv_b, qkv_b, Hk=GQA_KV_HEADS, G=Gb, dq=HEAD_DIM, dv=HEAD_DIM,
                        q_col0=0, v_col0=GQA_HEADS * HEAD_DIM,
                        k_col0=(GQA_HEADS + 2 * GQA_KV_HEADS) * HEAD_DIM, C=C, with_ctx=with_ctx)
        conv_w8 = jnp.pad(gdn_conv_w[layer], ((0, 8 - GDN_CONV), (0, 0)))
        qkv_c = _gdn_prep(Pg3, conv_w8, C)
        par = jnp.zeros((8, LANES), F32)
        par = par.at[0, :2 * GDN_HEADS].set(-jnp.exp(gdn_a_log[layer].reshape(-1)))
        par = par.at[1, :2 * GDN_HEADS].set(gdn_dt_bias[layer].reshape(-1))
        of, ob = _gdn_chunks(qkv_c, Pg3, par, C)
        yc = _gdn_post(of.reshape(R, -1), ob.reshape(R, -1), Pg, gdn_norm_g[layer].reshape(1, -1))
        qkv_d = _head_prep(Pa, P_SWA, gqa_q_norm_g[layer].reshape(1, -1), gqa_k_norm_g[layer].reshape(1, -1),
                           rope_head, nbt, norm=False, n_q=SWA_HEADS, n_k=SWA_KV_HEADS).reshape(Bn, T, -1)
        yd = _swa_attention(qkv_d, swa_sink[layer], C=C, with_ctx=with_ctx)

        mixed = [ya.reshape(-1, GROUP_WIDTH), yb.reshape(-1, GROUP_WIDTH), yc,
                 yd.reshape(-1, GROUP_WIDTH)]
        w_o = w_out[layer].astype(BF16)
        if with_ctx:
            xc = _mm_res(mixed, w_o, xc, tab, which=2, tm=_pick_tile(R, (1024, 512, 256)),
                         tn=_pick_tile(D, (1024, 512)), **lay)
        else:
            lay = dict(nb=nbl, ncb=0, ctx_row=Bn)
            xc = _mm_res(mixed, w_o, xc, tab, which=2, tm=ROW_BLOCK, tn=D, **lay,
                         in_row_map=lambda i: (i // nbl) * nbt + ncb + i % nbl, out_rows=Bn * S)

        i2 = layer // 2
        if layer % 2 == 0:
            hid = _norm_swiglu(xc, norm2_g[layer], tab, ffn_w1[i2].astype(BF16),
                               ffn_w3[i2].astype(BF16), which=3, **lay)
            xc = _mm_res([hid], ffn_w2[i2].astype(BF16), xc, tab, which=5, tm=_pick_tile(xc.shape[0], (1024, 512)), tn=512, **lay)
        else:
            router_p = jnp.pad(moe_router[i2], ((0, 0), (0, LANES - N_EXPERTS)))
            h2, logits = _norm_mod(xc, norm2_g[layer], tab, which=3, router=router_p, **lay)
            pos, src_tok, egate, tile_e, tile_valid = _route(logits[:, :N_EXPERTS])
            xs = _gather_rows(h2, src_tok)
            ys = _gmm(xs, moe_w1[i2].astype(BF16), moe_w3[i2].astype(BF16), moe_w2[i2].astype(BF16),
                      tile_e, tile_valid)
            xc = _combine(ys, pos, egate, xc, tab, which=5, **lay)

    return _final_norm(xc, final_norm_g).reshape(Bn, S, D)
```

```python
import functools

import numpy as np
import jax
import jax.numpy as jnp
from jax import lax
from jax.experimental import pallas as pl
from jax.experimental.pallas import tpu as pltpu

F32 = jnp.float32
BF16 = jnp.bfloat16
HIGHEST = lax.Precision.HIGHEST

GRID_W = 64
HEAD_DIM = 128
ROPE_THETA = 10000.0
NORM_EPS = 1e-6
NEG_INF = -1e30
MLA_HEADS = 4
MLA_NOPE = 128
MLA_ROPE = 64
MLA_V = 128
MLA_Q_RANK = 384
MLA_KV_RANK = 256
GQA_HEADS = 4
GQA_KV_HEADS = 2
GDN_HEADS = 4
GDN_CONV = 5
GDN_CHUNK = 64
SWA_HEADS = 4
SWA_KV_HEADS = 2
SWA_WINDOW = 128
N_EXPERTS = 8
TOP_K = 2
GROUP_WIDTH = 512

ROW_BLOCK = 256
LANES = 128
SUBLANES = 8
MOE_TILE = 512
ATTN_CHAINS_PER_STEP = 4
VMEM_LIMIT = 56 * 1024 * 1024

P_GDN = 0
P_AB = 2048
PG_WIDTH = 2304
P_GQA = 0
P_SWA = 1024
P_MLA = 2048
PA_WIDTH = 3072


def _cparams(sem, vmem=VMEM_LIMIT):
    return pltpu.CompilerParams(dimension_semantics=sem, vmem_limit_bytes=vmem)


def _table_row(gi, nb, ncb, ctx_row):
    b = gi // nb
    r = gi - b * nb
    return jnp.where(r < ncb, ctx_row, b)


def _mods_kernel(c_ref, w_ref, b_ref, o_ref):
    c = c_ref[...]
    a = (c * jax.nn.sigmoid(c)).astype(BF16)
    o_ref[0] = jnp.dot(a, w_ref[0].astype(BF16), preferred_element_type=F32) + b_ref[0]


def _mods(cmat, w_mod, b_mod):
    L, D, N = w_mod.shape
    M = cmat.shape[0]
    tn = 1024
    return pl.pallas_call(
        _mods_kernel,
        grid=(L, N // tn),
        in_specs=[pl.BlockSpec((M, D), lambda l, j: (0, 0)),
                  pl.BlockSpec((1, D, tn), lambda l, j: (l, 0, j)),
                  pl.BlockSpec((1, 1, tn), lambda l, j: (l, 0, j))],
        out_specs=pl.BlockSpec((1, M, tn), lambda l, j: (l, 0, j)),
        out_shape=jax.ShapeDtypeStruct((L, M, N), F32),
        compiler_params=_cparams(("parallel", "parallel")),
        name="mods",
    )(cmat, w_mod, b_mod.reshape(L, 1, N))


def _norm_mod_kernel(x_ref, g_ref, tab_ref, *rest, nb, ncb, ctx_row, which, router):
    idx = _table_row(pl.program_id(0), nb, ncb, ctx_row)
    x = x_ref[...]
    y = x * lax.rsqrt(jnp.mean(x * x, axis=-1, keepdims=True) + NORM_EPS) * g_ref[...]
    shift = tab_ref[pl.ds(idx * 6 + which, 1), :]
    scale = tab_ref[pl.ds(idx * 6 + which + 1, 1), :]
    h = y * (1.0 + scale) + shift
    if router:
        r_ref, h_ref, l_ref = rest
        h_ref[...] = h
        l_ref[...] = jnp.dot(h, r_ref[...], preferred_element_type=F32, precision=HIGHEST)
    else:
        (h_ref,) = rest
        h_ref[...] = h.astype(h_ref.dtype)


def _norm_mod(x, g, tab, *, nb, ncb, ctx_row, which, router=None):
    R, D = x.shape
    kern = functools.partial(_norm_mod_kernel, nb=nb, ncb=ncb, ctx_row=ctx_row, which=which,
                             router=router is not None)
    in_specs = [pl.BlockSpec((ROW_BLOCK, D), lambda i: (i, 0)),
                pl.BlockSpec((1, D), lambda i: (0, 0)),
                pl.BlockSpec(tab.shape, lambda i: (0, 0))]
    args = [x, g.reshape(1, D), tab]
    if router is None:
        out_specs = pl.BlockSpec((ROW_BLOCK, D), lambda i: (i, 0))
        out_shape = jax.ShapeDtypeStruct((R, D), BF16)
    else:
        in_specs.append(pl.BlockSpec(router.shape, lambda i: (0, 0)))
        args.append(router)
        out_specs = [pl.BlockSpec((ROW_BLOCK, D), lambda i: (i, 0)),
                     pl.BlockSpec((ROW_BLOCK, LANES), lambda i: (i, 0))]
        out_shape = [jax.ShapeDtypeStruct((R, D), F32), jax.ShapeDtypeStruct((R, LANES), F32)]
    return pl.pallas_call(kern, grid=(R // ROW_BLOCK,), in_specs=in_specs, out_specs=out_specs,
                          out_shape=out_shape, compiler_params=_cparams(("parallel",)),
                          name="norm_mod")(*args)


def _final_norm_kernel(x_ref, g_ref, o_ref):
    x = x_ref[...]
    o_ref[...] = x * lax.rsqrt(jnp.mean(x * x, axis=-1, keepdims=True) + NORM_EPS) * g_ref[...]


def _final_norm(x, g):
    R, D = x.shape
    return pl.pallas_call(
        _final_norm_kernel, grid=(R // ROW_BLOCK,),
        in_specs=[pl.BlockSpec((ROW_BLOCK, D), lambda i: (i, 0)),
                  pl.BlockSpec((1, D), lambda i: (0, 0))],
        out_specs=pl.BlockSpec((ROW_BLOCK, D), lambda i: (i, 0)),
        out_shape=jax.ShapeDtypeStruct((R, D), F32),
        compiler_params=_cparams(("parallel",)), name="final_norm")(x, g.reshape(1, D))


def _pick_tile(n, options):
    for t in options:
        if n % t == 0:
            return t
    raise ValueError(f"no tile for {n}")


def _norm_mod_rows(x_ref, g_ref, tab_ref, h_ref, *, tm, nb, ncb, ctx_row, which):
    i = pl.program_id(0)
    for c in range(tm // ROW_BLOCK):
        idx = _table_row(i * (tm // ROW_BLOCK) + c, nb, ncb, ctx_row)
        rows = slice(c * ROW_BLOCK, (c + 1) * ROW_BLOCK)
        x = x_ref[rows, :]
        y = x * lax.rsqrt(jnp.mean(x * x, axis=-1, keepdims=True) + NORM_EPS) * g_ref[...]
        shift = tab_ref[pl.ds(idx * 6 + which, 1), :]
        scale = tab_ref[pl.ds(idx * 6 + which + 1, 1), :]
        h_ref[rows, :] = (y * (1.0 + scale) + shift).astype(h_ref.dtype)


def _in_proj_kernel(x_ref, g_ref, tab_ref, w_ref, og_ref, oa_ref, h_ref, *, n_g, **norm_args):
    j = pl.program_id(1)

    @pl.when(j == 0)
    def _():
        _norm_mod_rows(x_ref, g_ref, tab_ref, h_ref, **norm_args)

    acc = jnp.dot(h_ref[...], w_ref[...], preferred_element_type=F32)

    @pl.when(j < n_g)
    def _():
        og_ref[...] = acc

    @pl.when(j >= n_g)
    def _():
        oa_ref[...] = acc.astype(oa_ref.dtype)


def _in_proj(x, g, tab, w, *, n_cols_g, nb, ncb, ctx_row, which):
    M, D = x.shape
    N = w.shape[1]
    tm = _pick_tile(M, (1024, 512, 256))
    tn = 768
    assert n_cols_g % tn == 0 and (N - n_cols_g) % tn == 0
    n_g = n_cols_g // tn
    kern = functools.partial(_in_proj_kernel, n_g=n_g, tm=tm, nb=nb, ncb=ncb, ctx_row=ctx_row,
                             which=which)
    return pl.pallas_call(
        kern, grid=(M // tm, N // tn),
        in_specs=[pl.BlockSpec((tm, D), lambda i, j: (i, 0)),
                  pl.BlockSpec((1, D), lambda i, j: (0, 0)),
                  pl.BlockSpec(tab.shape, lambda i, j: (0, 0)),
                  pl.BlockSpec((D, tn), lambda i, j: (0, j))],
        out_specs=[pl.BlockSpec((tm, tn), lambda i, j: (i, jnp.minimum(j, n_g - 1))),
                   pl.BlockSpec((tm, tn), lambda i, j: (i, jnp.maximum(j - n_g, 0)))],
        out_shape=[jax.ShapeDtypeStruct((M, n_cols_g), F32),
                   jax.ShapeDtypeStruct((M, N - n_cols_g), BF16)],
        scratch_shapes=[pltpu.VMEM((tm, D), BF16)],
        compiler_params=_cparams(("parallel", "arbitrary")), name="in_proj")(
            x, g.reshape(1, D), tab, w)


def _norm_swiglu_kernel(x_ref, g_ref, tab_ref, w1_ref, w3_ref, o_ref, h_ref, **norm_args):
    @pl.when(pl.program_id(1) == 0)
    def _():
        _norm_mod_rows(x_ref, g_ref, tab_ref, h_ref, **norm_args)

    a = h_ref[...]
    h1 = jnp.dot(a, w1_ref[...], preferred_element_type=F32)
    h3 = jnp.dot(a, w3_ref[...], preferred_element_type=F32)
    o_ref[...] = (h1 * jax.nn.sigmoid(h1) * h3).astype(o_ref.dtype)


def _norm_swiglu(x, g, tab, w1, w3, *, nb, ncb, ctx_row, which):
    M, D = x.shape
    N = w1.shape[1]
    tm = _pick_tile(M, (1024, 512, 256))
    tn = _pick_tile(N, (512, 256, 128))
    kern = functools.partial(_norm_swiglu_kernel, tm=tm, nb=nb, ncb=ncb, ctx_row=ctx_row, which=which)
    return pl.pallas_call(
        kern, grid=(M // tm, N // tn),
        in_specs=[pl.BlockSpec((tm, D), lambda i, j: (i, 0)),
                  pl.BlockSpec((1, D), lambda i, j: (0, 0)),
                  pl.BlockSpec(tab.shape, lambda i, j: (0, 0)),
                  pl.BlockSpec((D, tn), lambda i, j: (0, j)),
                  pl.BlockSpec((D, tn), lambda i, j: (0, j))],
        out_specs=pl.BlockSpec((tm, tn), lambda i, j: (i, j)),
        out_shape=jax.ShapeDtypeStruct((M, N), BF16),
        scratch_shapes=[pltpu.VMEM((tm, D), BF16)],
        compiler_params=_cparams(("parallel", "arbitrary")), name="norm_swiglu")(
            x, g.reshape(1, D), tab, w1, w3)


def _mm_res_kernel(*refs, n_a, tm, tn, nb, ncb, ctx_row, which):
    a_refs = refs[:n_a]
    w_ref, res_ref, tab_ref, o_ref = refs[n_a:]
    i = pl.program_id(0)
    j = pl.program_id(1)
    a = a_refs[0][...] if n_a == 1 else jnp.concatenate([r[...] for r in a_refs], axis=1)
    acc = jnp.dot(a, w_ref[...], preferred_element_type=F32)
    for c in range(tm // ROW_BLOCK):
        idx = _table_row(i * (tm // ROW_BLOCK) + c, nb, ncb, ctx_row)
        gate = tab_ref[pl.ds(idx * 6 + which, 1), pl.ds(pl.multiple_of(j * tn, tn), tn)]
        rows = slice(c * ROW_BLOCK, (c + 1) * ROW_BLOCK)
        o_ref[rows, :] = res_ref[rows, :] + gate * acc[rows, :]


def _mm_res(a_list, w, res, tab, *, nb, ncb, ctx_row, which, tm, tn,
            in_row_map=None, out_rows=None):
    K, N = w.shape
    M = res.shape[0] if out_rows is None else out_rows
    rm = (lambda i: i) if in_row_map is None else in_row_map
    kern = functools.partial(_mm_res_kernel, n_a=len(a_list), tm=tm, tn=tn, nb=nb, ncb=ncb,
                             ctx_row=ctx_row, which=which)

    def a_spec(a):
        if a.shape[0] == res.shape[0]:
            return pl.BlockSpec((tm, a.shape[1]), lambda i, j: (rm(i), 0))
        assert a.shape[0] == M
        return pl.BlockSpec((tm, a.shape[1]), lambda i, j: (i, 0))

    in_specs = [a_spec(a) for a in a_list]
    in_specs += [pl.BlockSpec((K, tn), lambda i, j: (0, j)),
                 pl.BlockSpec((tm, tn), lambda i, j: (rm(i), j)),
                 pl.BlockSpec(tab.shape, lambda i, j: (0, 0))]
    return pl.pallas_call(
        kern, grid=(M // tm, N // tn), in_specs=in_specs,
        out_specs=pl.BlockSpec((tm, tn), lambda i, j: (i, j)),
        out_shape=jax.ShapeDtypeStruct((M, N), F32),
        compiler_params=_cparams(("parallel", "parallel")), name="mm_res")(*a_list, w, res, tab)


def _ones_column(rows):
    lane = lax.broadcasted_iota(jnp.int32, (rows, LANES), 1)
    return jnp.where(lane == 0, 1.0, 0.0).astype(BF16)


def _rope_lanes(x, cos, sin, rot):
    return x * cos + jnp.dot(x.astype(BF16), rot, preferred_element_type=F32) * sin


def _mean_sq(x):
    w = x.shape[1]
    return jnp.dot((x * x).astype(BF16), jnp.ones((w, w), BF16), preferred_element_type=F32) * (1.0 / w)


def _rope_tables(S, C, rot_dim):
    rows = S // GRID_W
    row = jnp.repeat(jnp.arange(rows), GRID_W).astype(F32)
    col = jnp.tile(jnp.arange(GRID_W), rows).astype(F32)
    half = rot_dim // 2
    inv_freq = ROPE_THETA ** (-jnp.arange(0, half, 2, dtype=F32) / half)
    ang_r = row[:, None] * inv_freq
    ang_c = col[:, None] * inv_freq
    ang = jnp.concatenate([ang_r, ang_r, ang_c, ang_c], axis=-1)
    cos, sin = jnp.cos(ang), jnp.sin(ang)
    quarter = rot_dim // 4
    pad = LANES - rot_dim

    def full(t, ctx_val):
        t = jnp.pad(t, ((0, 0), (0, pad)), constant_values=ctx_val)
        return jnp.concatenate([jnp.full((C, LANES), ctx_val, F32), t], axis=0)

    j = np.arange(rot_dim)
    even = (j // quarter) % 2 == 0
    rot = np.zeros((LANES, LANES), np.float32)
    rot[np.where(even, j + quarter, j - quarter), j] = np.where(even, -1.0, 1.0)
    return full(cos, 1.0), full(sin, 0.0), jnp.asarray(rot, BF16)


def _mla_prep_kernel(p_ref, qg_ref, kvg_ref, wq_ref, wkv_ref, cos_ref, sin_ref, rot_ref,
                     q_ref, k_ref, v_ref, *, scale):
    cos, sin, rot = cos_ref[...], sin_ref[...], rot_ref[...]
    cq = p_ref[:, 0:MLA_Q_RANK].astype(F32)
    qn = cq * lax.rsqrt(jnp.mean(cq * cq, axis=-1, keepdims=True) + NORM_EPS) * qg_ref[...]
    q_raw = jnp.dot(qn.astype(BF16), wq_ref[...], preferred_element_type=F32)
    ckv = p_ref[:, MLA_Q_RANK:MLA_Q_RANK + MLA_KV_RANK].astype(F32)
    kvn = ckv * lax.rsqrt(jnp.mean(ckv * ckv, axis=-1, keepdims=True) + NORM_EPS) * kvg_ref[...]
    kv = jnp.dot(kvn.astype(BF16), wkv_ref[...], preferred_element_type=F32)
    kr = p_ref[:, MLA_Q_RANK + MLA_KV_RANK:MLA_Q_RANK + MLA_KV_RANK + LANES].astype(F32)
    kr = _rope_lanes(kr, cos, sin, rot).astype(BF16)
    for h in range(MLA_HEADS):
        c0 = 2 * LANES * h
        q_ref[:, c0:c0 + LANES] = (q_raw[:, c0:c0 + LANES] * scale).astype(BF16)
        qr = _rope_lanes(q_raw[:, c0 + LANES:c0 + 2 * LANES], cos, sin, rot)
        q_ref[:, c0 + LANES:c0 + 2 * LANES] = (qr * scale).astype(BF16)
        k_ref[:, c0:c0 + LANES] = kv[:, LANES * h:LANES * (h + 1)].astype(BF16)
        k_ref[:, c0 + LANES:c0 + 2 * LANES] = kr
    ones_col = _ones_column(p_ref.shape[0])
    for h in range(MLA_HEADS):
        c0 = 2 * LANES * h
        v_ref[:, c0:c0 + LANES] = kv[:, (MLA_HEADS + h) * LANES:(MLA_HEADS + h + 1) * LANES].astype(BF16)
        v_ref[:, c0 + LANES:c0 + 2 * LANES] = ones_col


def _mla_prep(P, qg, kvg, wq_p, wkv_p, tabs, nbt):
    R = P.shape[0]
    width = MLA_HEADS * 2 * LANES
    kern = functools.partial(_mla_prep_kernel, scale=float((MLA_NOPE + MLA_ROPE) ** -0.5))
    tab_spec = pl.BlockSpec((ROW_BLOCK, LANES), lambda i: (i % nbt, 0))
    const = lambda a: pl.BlockSpec(a.shape, lambda i: (0, 0))
    return pl.pallas_call(
        kern, grid=(R // ROW_BLOCK,),
        in_specs=[pl.BlockSpec((ROW_BLOCK, 1024), lambda i: (i, P_MLA // 1024)),
                  const(qg), const(kvg), const(wq_p), const(wkv_p), tab_spec, tab_spec, const(tabs[2])],
        out_specs=[pl.BlockSpec((ROW_BLOCK, width), lambda i: (i, 0)),
                   pl.BlockSpec((ROW_BLOCK, width), lambda i: (i, 0)),
                   pl.BlockSpec((ROW_BLOCK, width), lambda i: (i, 0))],
        out_shape=[jax.ShapeDtypeStruct((R, width), BF16), jax.ShapeDtypeStruct((R, width), BF16),
                   jax.ShapeDtypeStruct((R, width), BF16)],
        compiler_params=_cparams(("parallel",)), name="mla_prep")(P, qg, kvg, wq_p, wkv_p, *tabs)


def _head_prep_kernel(p_ref, qg_ref, kg_ref, cos_ref, sin_ref, rot_ref, o_ref, *, norm, scale,
                      n_q, n_k):
    cos, sin, rot = cos_ref[...], sin_ref[...], rot_ref[...]
    for h in range(n_q + n_k):
        x = p_ref[:, LANES * h:LANES * (h + 1)].astype(F32)
        if norm:
            g = qg_ref[...] if h < n_q else kg_ref[...]
            x = x * lax.rsqrt(_mean_sq(x) + NORM_EPS) * g
        x = _rope_lanes(x, cos, sin, rot)
        if h < n_q:
            x = x * scale
        c0 = LANES * h if h < n_q else LANES * (h + 2 * n_k)
        o_ref[:, c0:c0 + LANES] = x.astype(BF16)
    v_in = LANES * (n_q + n_k)
    v_out = LANES * n_q
    ones_col = _ones_column(p_ref.shape[0])
    for h in range(n_k):
        o_ref[:, v_out + 2 * LANES * h:v_out + 2 * LANES * h + LANES] = (
            p_ref[:, v_in + LANES * h:v_in + LANES * (h + 1)])
        o_ref[:, v_out + 2 * LANES * h + LANES:v_out + 2 * LANES * (h + 1)] = ones_col


def _head_prep(P, col0, qg, kg, tabs, nbt, *, norm, n_q, n_k):
    R = P.shape[0]
    kern = functools.partial(_head_prep_kernel, norm=norm, scale=float(HEAD_DIM ** -0.5),
                             n_q=n_q, n_k=n_k)
    tab_spec = pl.BlockSpec((ROW_BLOCK, LANES), lambda i: (i % nbt, 0))
    const = lambda a: pl.BlockSpec(a.shape, lambda i: (0, 0))
    return pl.pallas_call(
        kern, grid=(R // ROW_BLOCK,),
        in_specs=[pl.BlockSpec((ROW_BLOCK, 1024), lambda i: (i, col0 // 1024)),
                  const(qg), const(kg), tab_spec, tab_spec, const(tabs[2])],
        out_specs=pl.BlockSpec((ROW_BLOCK, 1024 + n_k * LANES), lambda i: (i, 0)),
        out_shape=jax.ShapeDtypeStruct((R, 1024 + n_k * LANES), BF16),
        compiler_params=_cparams(("parallel",)), name="head_prep")(P, qg, kg, *tabs)


def _attn_kernel(q_ref, k_ref, v_ref, o_ref, *, HS, G, dq, dv, C, T, ctx_tile):
    def run(nk):
        nt = (((1,), (1,)), ((), ()))
        chains = [(hs, g) for hs in range(HS) for g in range(G)]
        s = []
        for hs, g in chains:
            q = q_ref[0, :, (hs * G + g) * dq:(hs * G + g + 1) * dq]
            k = k_ref[0, :nk, hs * dq:(hs + 1) * dq]
            s.append(lax.dot_general(q, k, nt, preferred_element_type=F32).astype(BF16))
        p = [jnp.exp(x - jnp.max(x, axis=-1, keepdims=True)) for x in s]
        for (hs, g), pc in zip(chains, p):
            o = jnp.dot(pc, v_ref[0, :nk, hs * 2 * dv:(hs + 1) * 2 * dv], preferred_element_type=F32)
            c0 = (hs * G + g) * dv
            o_ref[0, :, c0:c0 + dv] = (o[:, :dv] / o[:, dv:dv + 1]).astype(o_ref.dtype)

    if ctx_tile:
        qi = pl.program_id(2)
        pl.when(qi == 0)(lambda: run(C))
        pl.when(qi > 0)(lambda: run(T))
    else:
        run(T)


def _attention(q, k, v, *, Hk, G, dq, dv, q_col0, k_col0, v_col0, C, with_ctx):
    HS = ATTN_CHAINS_PER_STEP // G
    Bn, T, _ = q.shape
    ncb = C // ROW_BLOCK
    assert ncb == 1
    nq = T // ROW_BLOCK if with_ctx else (T - C) // ROW_BLOCK
    qoff = 0 if with_ctx else ncb
    assert Hk % HS == 0
    kern = functools.partial(_attn_kernel, HS=HS, G=G, dq=dq, dv=dv, C=C, T=T, ctx_tile=with_ctx)
    qw, kw, vw = HS * G * dq, HS * dq, HS * 2 * dv
    assert q_col0 % qw == 0 and k_col0 % kw == 0 and v_col0 % vw == 0
    return pl.pallas_call(
        kern, grid=(Bn, Hk // HS, nq),
        in_specs=[pl.BlockSpec((1, ROW_BLOCK, qw), lambda b, h, i: (b, i + qoff, q_col0 // qw + h)),
                  pl.BlockSpec((1, T, kw), lambda b, h, i: (b, 0, k_col0 // kw + h)),
                  pl.BlockSpec((1, T, vw), lambda b, h, i: (b, 0, v_col0 // vw + h))],
        out_specs=pl.BlockSpec((1, ROW_BLOCK, HS * G * dv), lambda b, h, i: (b, i, h)),
        out_shape=jax.ShapeDtypeStruct((Bn, nq * ROW_BLOCK, Hk * G * dv), BF16),
        compiler_params=_cparams(("parallel", "parallel", "parallel")), name="attention")(q, k, v)


def _swa_attention(qkv, sink, *, C, with_ctx):
    Bn, T, _ = qkv.shape
    Hq, Hk, d = SWA_HEADS, SWA_KV_HEADS, HEAD_DIM
    ncb = C // ROW_BLOCK
    assert ncb == 1 and T - C >= ROW_BLOCK + 2 * SWA_WINDOW
    nq = T // ROW_BLOCK if with_ctx else (T - C) // ROW_BLOCK
    qoff = 0 if with_ctx else ncb
    kern = functools.partial(_swa_kernel, C=C, S=T - C, ctx_tile=with_ctx, qoff=qoff)
    k_blk = (Hq * d + Hk * 2 * d) // (Hk * d)
    return pl.pallas_call(
        kern, grid=(Bn, nq),
        in_specs=[pl.BlockSpec(memory_space=pltpu.SMEM),
                  pl.BlockSpec((1, ROW_BLOCK, Hq * d), lambda b, i: (b, i + qoff, 0)),
                  pl.BlockSpec((1, T, Hk * d), lambda b, i: (b, 0, k_blk)),
                  pl.BlockSpec((1, T, Hk * 2 * d), lambda b, i: (b, 0, 1))],
        out_specs=pl.BlockSpec((1, ROW_BLOCK, Hq * d), lambda b, i: (b, i, 0)),
        out_shape=jax.ShapeDtypeStruct((Bn, nq * ROW_BLOCK, Hq * d), BF16),
        compiler_params=_cparams(("parallel", "parallel")), name="swa")(sink, qkv, qkv, qkv)


def _swa_kernel(sink_ref, q_ref, k_ref, v_ref, o_ref, *, C, S, ctx_tile, qoff):
    qi = pl.program_id(1) + qoff
    d = HEAD_DIM
    G = SWA_HEADS // SWA_KV_HEADS
    W = SWA_WINDOW
    WIN = ROW_BLOCK + 2 * W
    nt = (((1,), (1,)), ((), ()))
    heads = range(SWA_HEADS)

    def q_of(h):
        return q_ref[0, :, h * d:(h + 1) * d]

    def finish(parts):
        m16 = []
        for h in heads:
            m = jnp.full((ROW_BLOCK, 1), sink_ref[h], F32)
            for s, _ in parts[h]:
                m = jnp.maximum(m, jnp.max(s, axis=-1, keepdims=True).astype(F32))
            m16.append(m.astype(BF16))
        for h in heads:
            o = None
            for s, v in parts[h]:
                pv = jnp.dot(jnp.exp(s - m16[h]), v, preferred_element_type=F32)
                o = pv if o is None else o + pv
            den = jnp.exp(sink_ref[h] - m16[h].astype(F32)) + o[:, d:d + 1]
            o_ref[0, :, h * d:(h + 1) * d] = (o[:, :d] / den).astype(o_ref.dtype)

    def ctx_scores(h):
        kc = k_ref[0, :C, (h // G) * d:(h // G + 1) * d]
        return (lax.dot_general(q_of(h), kc, nt, preferred_element_type=F32).astype(BF16),
                v_ref[0, :C, (h // G) * 2 * d:(h // G + 1) * 2 * d])

    def run_ctx():
        finish([[ctx_scores(h)] for h in heads])

    def run_lat():
        q0 = (qi - (C // ROW_BLOCK)) * ROW_BLOCK
        ws = jnp.clip(q0 - W, 0, S - WIN)
        start = pl.multiple_of(C + ws, LANES)
        qpos = q0 + lax.broadcasted_iota(jnp.int32, (ROW_BLOCK, WIN), 0)
        kpos = ws + lax.broadcasted_iota(jnp.int32, (ROW_BLOCK, WIN), 1)
        valid = jnp.abs(kpos - qpos) <= W
        parts = []
        for h in heads:
            kw = k_ref[0, pl.ds(start, WIN), (h // G) * d:(h // G + 1) * d]
            vw = v_ref[0, pl.ds(start, WIN), (h // G) * 2 * d:(h // G + 1) * 2 * d]
            s_loc = jnp.where(valid, lax.dot_general(q_of(h), kw, nt, preferred_element_type=F32),
                              NEG_INF).astype(BF16)
            parts.append([(s_loc, vw), ctx_scores(h)])
        finish(parts)

    if ctx_tile:
        pl.when(qi == 0)(run_ctx)
        pl.when(qi > 0)(run_lat)
    else:
        run_lat()


def _gdn_prep_kernel(x_ref, w_ref, o_ref, pad_ref, *, C, T):
    j = pl.program_id(1)
    g = SUBLANES
    half = (GDN_CONV - 1) // 2
    assert half <= g
    zeros = jnp.zeros((g, LANES), F32)
    segments = ((0, C, g), (C, T - C, 2 * g + C))
    pad_ref[0:g, :] = zeros
    for lo, n, base in segments:
        pad_ref[base:base + n, :] = x_ref[0, lo:lo + n, :]
        pad_ref[base + n:base + n + g, :] = zeros
    nh = GDN_HEADS
    for lo, n, base in segments:
        acc = None
        for tap in range(GDN_CONV):
            term = pad_ref[base + tap - half:base + tap - half + n, :] * w_ref[tap:tap + 1, :]
            acc = term if acc is None else acc + term
        y = acc * jax.nn.sigmoid(acc)
        inv = lax.rsqrt(_mean_sq(y) * float(LANES) + 1e-6)
        f = jnp.where(j < 2 * nh, inv, 1.0) * jnp.where(j < nh, float(HEAD_DIM ** -0.5), 1.0)
        o_ref[0, lo:lo + n, :] = y * f


def _gdn_prep(P3, conv_w8, C):
    Bn, T, _ = P3.shape
    nblk = 3 * GDN_HEADS
    kern = functools.partial(_gdn_prep_kernel, C=C, T=T)
    return pl.pallas_call(
        kern, grid=(Bn, nblk),
        in_specs=[pl.BlockSpec((1, T, LANES), lambda b, j: (b, 0, P_GDN // LANES + j)),
                  pl.BlockSpec((8, LANES), lambda b, j: (0, j))],
        out_specs=pl.BlockSpec((1, T, LANES), lambda b, j: (b, 0, j)),
        out_shape=jax.ShapeDtypeStruct((Bn, T, nblk * LANES), F32),
        scratch_shapes=[pltpu.VMEM((T + 3 * SUBLANES, LANES), F32)],
        compiler_params=_cparams(("parallel", "parallel")), name="gdn_prep")(P3, conv_w8)


_GDN_INV_BASE = 8
GDN_BATCH_PER_STEP = 4


def _dot16(a, b):
    return jnp.dot(a, b, preferred_element_type=F32)


def _dot16_nt(a, b):
    return lax.dot_general(a, b, (((1,), (1,)), ((), ())), preferred_element_type=F32)


def _dot16_tn(a, b):
    return lax.dot_general(a, b, (((0,), (0,)), ((), ())), preferred_element_type=F32)


def _mask_dot(mask16, x):
    hi = x.astype(BF16)
    r1 = x - hi.astype(F32)
    mid = r1.astype(BF16)
    lo = (r1 - mid.astype(F32)).astype(BF16)
    w = x.shape[1]
    y = jnp.dot(mask16, jnp.concatenate([hi, mid, lo], axis=1), preferred_element_type=F32)
    return y[:, :w] + y[:, w:2 * w] + y[:, 2 * w:]


def _gdn_chunk_kernel(qf_ref, qb_ref, abf_ref, abb_ref, par_ref, of_ref, ob_ref, s_ref):
    Cc = GDN_CHUNK
    H = GDN_HEADS
    d = HEAD_DIM

    @pl.when(pl.program_id(1) == 0)
    def _():
        s_ref[...] = jnp.zeros_like(s_ref)

    ri = lax.broadcasted_iota(jnp.int32, (Cc, Cc), 0)
    ci = lax.broadcasted_iota(jnp.int32, (Cc, Cc), 1)
    eye = (ri == ci).astype(F32)
    neg_a = par_ref[0:1, :]
    dt_b = par_ref[1:2, :]
    base = _GDN_INV_BASE
    diag_blk = (ri // base) == (ci // base)
    off_blks = []
    s = base
    while s < Cc:
        off_blks.append(jnp.logical_and((ri // (2 * s)) == (ci // (2 * s)), (ri // s) != (ci // s)))
        s *= 2

    dirs = []
    for bb, direction in [(bb, direction) for bb in range(qf_ref.shape[0]) for direction in range(2)]:
        ab = (abf_ref if direction == 0 else abb_ref)[bb]
        if direction == 0:
            incl, incl_t, strict, last = ri >= ci, ri <= ci, ri > ci, Cc - 1
        else:
            incl, incl_t, strict, last = ri <= ci, ri >= ci, ri < ci, 0
        z = ab + dt_b
        sp = jnp.maximum(z, 0.0) + jnp.log1p(jnp.exp(-jnp.abs(z)))
        gl_all = neg_a * sp
        gc_all = _mask_dot(incl.astype(BF16), gl_all)
        g_tot = gc_all[last:last + 1, :]
        gc_sq = jnp.concatenate([gc_all, jnp.zeros((LANES - Cc, LANES), F32)], axis=0)
        dirs.append(dict(
            bb=bb, direction=direction,
            x_ref=qf_ref if direction == 0 else qb_ref, o_ref=of_ref if direction == 0 else ob_ref,
            incl=incl, strict=strict, gc_all=gc_all,
            gct_all=gc_sq.T[:, :Cc],
            beta_all=jax.nn.sigmoid(ab),
            eg_all=jnp.exp(gc_all), ekg_all=jnp.exp(g_tot - gc_all), egt_all=jnp.exp(g_tot)))

    units = [(dd, dd["direction"] * H + h, h) for dd in dirs for h in range(H)]
    U = range(len(units))

    k16, kb, decay, a_mat = [], [], [], []
    for dd, lane, h in units:
        k = dd["x_ref"][dd["bb"],:, (H + h) * d:(H + h + 1) * d]
        beta = dd["beta_all"][:, 2 * H + lane:2 * H + lane + 1]
        diff = dd["gc_all"][:, lane:lane + 1] - dd["gct_all"][lane:lane + 1, :]
        decay.append(jnp.where(dd["incl"], jnp.exp(jnp.where(dd["incl"], diff, 0.0)), 0.0))
        kb.append(k * beta)
        k16.append(k.astype(BF16))
    for i in U:
        a_mat.append(jnp.where(units[i][0]["strict"],
                               _dot16_nt(kb[i].astype(BF16), k16[i]) * decay[i], 0.0))

    npow = [jnp.where(diag_blk, a, 0.0).astype(BF16) for a in a_mat]
    t = [eye - jnp.where(diag_blk, a, 0.0) for a in a_mat]
    sq = 2 * base
    while sq > 4:
        npow = [_dot16(n, n).astype(BF16) for n in npow]
        t = [t[i] + _dot16(t[i].astype(BF16), npow[i]) for i in U]
        sq //= 2
    for off_blk in off_blks:
        t16 = [x.astype(BF16) for x in t]
        xo = [_dot16(t16[i], jnp.where(off_blk, a_mat[i], 0.0).astype(BF16)).astype(BF16) for i in U]
        t = [t[i] - _dot16(xo[i], t16[i]) for i in U]

    uw, intra, qg = [], [], []
    for i in U:
        dd, lane, h = units[i]
        q = dd["x_ref"][dd["bb"],:, h * d:(h + 1) * d]
        v = dd["x_ref"][dd["bb"],:, (2 * H + h) * d:(2 * H + h + 1) * d]
        beta = dd["beta_all"][:, 2 * H + lane:2 * H + lane + 1]
        eg = dd["eg_all"][:, lane:lane + 1]
        rhs = jnp.concatenate([v * beta, kb[i] * eg], axis=1).astype(BF16)
        uw.append(_dot16(t[i].astype(BF16), rhs))
        intra.append(jnp.where(dd["incl"], _dot16_nt(q.astype(BF16), k16[i]) * decay[i], 0.0)
                     .astype(BF16))
        qg.append((q * eg).astype(BF16))
    state = [s_ref[units[i][0]["bb"] * 2 * H + units[i][1]] for i in U]
    s16 = [x.astype(BF16) for x in state]
    v_new = [uw[i][:, :d] - _dot16(uw[i][:, d:].astype(BF16), s16[i]) for i in U]
    vn16 = [x.astype(BF16) for x in v_new]
    for i in U:
        dd, lane, h = units[i]
        dd["o_ref"][dd["bb"], :, h * d:(h + 1) * d] = (_dot16(qg[i], s16[i])
                                                       + _dot16(intra[i], vn16[i]))
    for i in U:
        dd, lane, h = units[i]
        k = dd["x_ref"][dd["bb"],:, (H + h) * d:(H + h + 1) * d]
        kg = (k * dd["ekg_all"][:, lane:lane + 1]).astype(BF16)
        s_ref[dd["bb"] * 2 * H + lane] = (state[i] * dd["egt_all"][:, lane:lane + 1]
                                          + _dot16_tn(kg, vn16[i]))


def _gdn_chunks(qkv, P3, par, C):
    Bn, T, _ = qkv.shape
    Cc = GDN_CHUNK
    nc = T // Cc
    ncc = C // Cc
    H = GDN_HEADS

    def bwd_chunk(s):
        return jnp.where(s < ncc, ncc - 1 - s, nc + ncc - 1 - s)

    ab_blk = P_AB // LANES
    nbb = _pick_tile(Bn, (GDN_BATCH_PER_STEP, 1))
    return pl.pallas_call(
        _gdn_chunk_kernel, grid=(Bn // nbb, nc),
        in_specs=[pl.BlockSpec((nbb, Cc, 3 * H * HEAD_DIM), lambda b, s: (b, s, 0)),
                  pl.BlockSpec((nbb, Cc, 3 * H * HEAD_DIM), lambda b, s: (b, bwd_chunk(s), 0)),
                  pl.BlockSpec((nbb, Cc, LANES), lambda b, s: (b, s, ab_blk)),
                  pl.BlockSpec((nbb, Cc, LANES), lambda b, s: (b, bwd_chunk(s), ab_blk)),
                  pl.BlockSpec((8, LANES), lambda b, s: (0, 0))],
        out_specs=[pl.BlockSpec((nbb, Cc, H * HEAD_DIM), lambda b, s: (b, s, 0)),
                   pl.BlockSpec((nbb, Cc, H * HEAD_DIM), lambda b, s: (b, bwd_chunk(s), 0))],
        out_shape=[jax.ShapeDtypeStruct((Bn, T, H * HEAD_DIM), F32),
                   jax.ShapeDtypeStruct((Bn, T, H * HEAD_DIM), F32)],
        scratch_shapes=[pltpu.VMEM((nbb * 2 * H, HEAD_DIM, HEAD_DIM), F32)],
        compiler_params=_cparams(("parallel", "arbitrary")), name="gdn_chunks")(
            qkv, qkv, P3, P3, par)


def _gdn_post_kernel(of_ref, ob_ref, z_ref, g_ref, o_ref):
    d = HEAD_DIM
    for h in range(GDN_HEADS):
        o = of_ref[:, h * d:(h + 1) * d] + ob_ref[:, h * d:(h + 1) * d]
        y = o * lax.rsqrt(_mean_sq(o) + NORM_EPS) * g_ref[...]
        z = z_ref[:, h * d:(h + 1) * d]
        o_ref[:, h * d:(h + 1) * d] = (y * (z * jax.nn.sigmoid(z))).astype(o_ref.dtype)


def _gdn_post(of, ob, P, g):
    R, W = of.shape
    zblk = (P_GDN + 3 * W) // W
    return pl.pallas_call(
        _gdn_post_kernel, grid=(R // ROW_BLOCK,),
        in_specs=[pl.BlockSpec((ROW_BLOCK, W), lambda i: (i, 0)),
                  pl.BlockSpec((ROW_BLOCK, W), lambda i: (i, 0)),
                  pl.BlockSpec((ROW_BLOCK, W), lambda i: (i, zblk)),
                  pl.BlockSpec((1, HEAD_DIM), lambda i: (0, 0))],
        out_specs=pl.BlockSpec((ROW_BLOCK, W), lambda i: (i, 0)),
        out_shape=jax.ShapeDtypeStruct((R, W), BF16),
        compiler_params=_cparams(("parallel",)), name="gdn_post")(of, ob, P, g)


def _gather_rows_kernel(tok_ref, tok_next_ref, h_hbm, o_ref, buf, sem):
    i = pl.program_id(0)
    nt = pl.num_programs(0)
    n = buf.shape[1]
    slot = i % 2

    def issue(t_ref, s):
        def body(r, c):
            pltpu.make_async_copy(h_hbm.at[pl.ds(t_ref[0, 0, r], 1), :],
                                  buf.at[s, pl.ds(r, 1), :], sem.at[s]).start()
            return c
        lax.fori_loop(0, n, body, 0, unroll=8)

    @pl.when(i == 0)
    def _():
        issue(tok_ref, 0)

    @pl.when(i + 1 < nt)
    def _():
        issue(tok_next_ref, 1 - slot)

    pltpu.make_async_copy(h_hbm.at[pl.ds(0, n), :], buf.at[slot], sem.at[slot]).wait()
    o_ref[...] = buf[slot].astype(o_ref.dtype)


def _gather_rows(h, src_tok):
    D = h.shape[1]
    n = src_tok.shape[0]
    nt = n // ROW_BLOCK
    return pl.pallas_call(
        _gather_rows_kernel, grid=(nt,),
        in_specs=[pl.BlockSpec((1, 1, ROW_BLOCK), lambda i: (i, 0, 0), memory_space=pltpu.SMEM),
                  pl.BlockSpec((1, 1, ROW_BLOCK), lambda i: (jnp.minimum(i + 1, nt - 1), 0, 0),
                               memory_space=pltpu.SMEM),
                  pl.BlockSpec(memory_space=pl.ANY)],
        out_specs=pl.BlockSpec((ROW_BLOCK, D), lambda i: (i, 0)),
        out_shape=jax.ShapeDtypeStruct((n, D), BF16),
        scratch_shapes=[pltpu.VMEM((2, ROW_BLOCK, D), F32), pltpu.SemaphoreType.DMA((2,))],
        compiler_params=_cparams(("arbitrary",)), name="moe_gather")(
            src_tok.reshape(nt, 1, ROW_BLOCK), src_tok.reshape(nt, 1, ROW_BLOCK), h)


def _gmm_kernel(te_ref, tv_ref, x_ref, w1_ref, w3_ref, w2_ref, o_ref):
    i = pl.program_id(0)
    f = pl.program_id(1)

    @pl.when(f == 0)
    def _():
        o_ref[...] = jnp.zeros_like(o_ref)

    @pl.when(tv_ref[i] == 1)
    def _():
        x = x_ref[...]
        h1 = jnp.dot(x, w1_ref[0], preferred_element_type=F32)
        h3 = jnp.dot(x, w3_ref[0], preferred_element_type=F32)
        hid = (h1 * jax.nn.sigmoid(h1) * h3).astype(BF16)
        o_ref[...] += jnp.dot(hid, w2_ref[0], preferred_element_type=F32)


def _gmm(xs, w1, w3, w2, tile_e, tile_valid):
    n, D = xs.shape
    E, _, F = w1.shape
    tf = 1024
    nt = n // MOE_TILE
    grid_spec = pltpu.PrefetchScalarGridSpec(
        num_scalar_prefetch=2, grid=(nt, F // tf),
        in_specs=[pl.BlockSpec((MOE_TILE, D), lambda i, f, te, tv: (i, 0)),
                  pl.BlockSpec((1, D, tf), lambda i, f, te, tv: (te[i], 0, f)),
                  pl.BlockSpec((1, D, tf), lambda i, f, te, tv: (te[i], 0, f)),
                  pl.BlockSpec((1, tf, D), lambda i, f, te, tv: (te[i], f, 0))],
        out_specs=pl.BlockSpec((MOE_TILE, D), lambda i, f, te, tv: (i, 0)))
    return pl.pallas_call(
        _gmm_kernel, grid_spec=grid_spec,
        out_shape=jax.ShapeDtypeStruct((n, D), F32),
        compiler_params=_cparams(("parallel", "arbitrary")), name="moe_gmm")(
            tile_e, tile_valid, xs, w1, w3, w2)


def _combine_kernel(pos_ref, pos_next_ref, ys_hbm, x_ref, eg_ref, tab_ref, o_ref, buf, sem, *, nb,
                    ncb, ctx_row, which):
    i = pl.program_id(0)
    nt = pl.num_programs(0)
    n = buf.shape[2]
    slot = i % 2
    idx = _table_row(i, nb, ncb, ctx_row)

    def issue(p_ref, s):
        def body(r, c):
            for k in range(TOP_K):
                pltpu.make_async_copy(ys_hbm.at[pl.ds(p_ref[0, k, r], 1), :],
                                      buf.at[s, k, pl.ds(r, 1), :], sem.at[s]).start()
            return c
        lax.fori_loop(0, n, body, 0, unroll=4)

    @pl.when(i == 0)
    def _():
        issue(pos_ref, 0)

    @pl.when(i + 1 < nt)
    def _():
        issue(pos_next_ref, 1 - slot)

    for k in range(TOP_K):
        pltpu.make_async_copy(ys_hbm.at[pl.ds(0, n), :], buf.at[slot, k], sem.at[slot]).wait()
    gate = tab_ref[pl.ds(idx * 6 + which, 1), :]
    eg = eg_ref[...]
    y = eg[:, 0:1] * buf[slot, 0]
    for k in range(1, TOP_K):
        y = y + eg[:, k:k + 1] * buf[slot, k]
    o_ref[...] = x_ref[...] + gate * y


def _combine(ys, pos, egate, x, tab, *, nb, ncb, ctx_row, which):
    R, D = x.shape
    nt = R // ROW_BLOCK
    pos3 = pos.reshape(nt, ROW_BLOCK, TOP_K).transpose(0, 2, 1)
    kern = functools.partial(_combine_kernel, nb=nb, ncb=ncb, ctx_row=ctx_row, which=which)
    return pl.pallas_call(
        kern, grid=(nt,),
        in_specs=[pl.BlockSpec((1, TOP_K, ROW_BLOCK), lambda i: (i, 0, 0), memory_space=pltpu.SMEM),
                  pl.BlockSpec((1, TOP_K, ROW_BLOCK), lambda i: (jnp.minimum(i + 1, nt - 1), 0, 0),
                               memory_space=pltpu.SMEM),
                  pl.BlockSpec(memory_space=pl.ANY),
                  pl.BlockSpec((ROW_BLOCK, D), lambda i: (i, 0)),
                  pl.BlockSpec((ROW_BLOCK, LANES), lambda i: (i, 0)),
                  pl.BlockSpec(tab.shape, lambda i: (0, 0))],
        out_specs=pl.BlockSpec((ROW_BLOCK, D), lambda i: (i, 0)),
        out_shape=jax.ShapeDtypeStruct((R, D), F32),
        scratch_shapes=[pltpu.VMEM((2, TOP_K, ROW_BLOCK, D), F32), pltpu.SemaphoreType.DMA((2,))],
        compiler_params=_cparams(("arbitrary",)), name="moe_combine")(pos3, pos3, ys, x, egate, tab)


def _route(logits):
    R = logits.shape[0]
    E = N_EXPERTS
    top_v, top_i = lax.top_k(logits, TOP_K)
    gate = jax.nn.softmax(top_v, axis=-1)
    onehot = jnp.sum((top_i[:, :, None] == jnp.arange(E)[None, None, :]).astype(jnp.int32), axis=1)
    before = jnp.cumsum(onehot, axis=0) - onehot
    sizes = jnp.sum(onehot, axis=0)
    padded = ((sizes + MOE_TILE - 1) // MOE_TILE) * MOE_TILE
    ends = jnp.cumsum(padded)
    off = ends - padded
    pos = off[top_i] + jnp.take_along_axis(before, top_i, axis=1)
    nt = (TOP_K * R) // MOE_TILE + E
    n = nt * MOE_TILE
    flat = pos.reshape(-1)
    src_tok = jnp.zeros((n,), jnp.int32).at[flat].set(jnp.repeat(jnp.arange(R, dtype=jnp.int32), TOP_K))
    egate = jnp.pad(gate, ((0, 0), (0, LANES - TOP_K)))
    tile_start = jnp.arange(nt, dtype=jnp.int32) * MOE_TILE
    tile_valid = (tile_start < ends[-1]).astype(jnp.int32)
    tile_e = jnp.searchsorted(ends, tile_start, side="right").astype(jnp.int32)
    last_e = jnp.max(jnp.where(tile_valid == 1, tile_e, 0))
    tile_e = jnp.where(tile_valid == 1, tile_e, last_e)
    return pos.astype(jnp.int32), src_tok, egate, tile_e, tile_valid


def _prep_w_in(w):
    D = w.shape[0]
    z = lambda n: jnp.zeros((D, n), w.dtype)
    w_g = jnp.concatenate([w[:, 1728:3776], w[:, 3776:3792], z(PG_WIDTH - 2048 - 16)], axis=1)
    w_a = jnp.concatenate([w[:, 704:1728], w[:, 3792:4816], w[:, 0:704], z(PA_WIDTH - 2048 - 704)],
                          axis=1)
    assert w_g.shape[1] == PG_WIDTH and w_a.shape[1] == PA_WIDTH
    return jnp.concatenate([w_g, w_a], axis=1).astype(BF16)


def _prep_w_uq(w):
    r = w.shape[0]
    w = w.reshape(r, MLA_HEADS, MLA_NOPE + MLA_ROPE)
    w = jnp.pad(w, ((0, 0), (0, 0), (0, 2 * LANES - (MLA_NOPE + MLA_ROPE))))
    return w.reshape(r, MLA_HEADS * 2 * LANES).astype(BF16)


def _prep_w_ukv(w):
    r = w.shape[0]
    w = w.reshape(r, MLA_HEADS, MLA_NOPE + MLA_V)
    return jnp.concatenate([w[:, :, :MLA_NOPE].reshape(r, -1), w[:, :, MLA_NOPE:].reshape(r, -1)],
                           axis=1).astype(BF16)


def kernel(x, c, ctx, c_ctx, norm1_g, norm2_g, w_mod, b_mod, w_in, mla_q_norm_g, mla_kv_norm_g,
           mla_w_uq, mla_w_ukv, gqa_q_norm_g, gqa_k_norm_g, gdn_conv_w, gdn_a_log, gdn_dt_bias,
           gdn_norm_g, swa_sink, w_out, ffn_w1, ffn_w3, ffn_w2, moe_router, moe_w1, moe_w3, moe_w2,
           final_norm_g):
    Bn, S, D = x.shape
    C = ctx.shape[1]
    T = C + S
    R = Bn * T
    depth = w_in.shape[0]
    nbt = T // ROW_BLOCK
    ncb = C // ROW_BLOCK
    nbl = S // ROW_BLOCK

    m_rows = ((Bn + 1 + 7) // 8) * 8
    cmat = jnp.zeros((m_rows, D), F32).at[:Bn].set(c).at[Bn].set(c_ctx)
    mods = _mods(cmat, w_mod, b_mod).reshape(depth, m_rows * 6, D)

    rope_mla = _rope_tables(S, C, MLA_ROPE)
    rope_head = _rope_tables(S, C, HEAD_DIM)

    xc = jnp.concatenate([ctx, x], axis=1).reshape(R, D)
    lay = dict(nb=nbt, ncb=ncb, ctx_row=Bn)

    for layer in range(depth):
        with_ctx = layer < depth - 1
        tab = mods[layer]
        Pg, Pa = _in_proj(xc, norm1_g[layer], tab, _prep_w_in(w_in[layer]), n_cols_g=PG_WIDTH,
                          which=0, **lay)
        Pg3 = Pg.reshape(Bn, T, PG_WIDTH)

        qa, ka, va = _mla_prep(Pa, mla_q_norm_g[layer].reshape(1, -1), mla_kv_norm_g[layer].reshape(1, -1),
                               _prep_w_uq(mla_w_uq[layer]), _prep_w_ukv(mla_w_ukv[layer]), rope_mla, nbt)
        ya = _attention(qa.reshape(Bn, T, -1), ka.reshape(Bn, T, -1), va.reshape(Bn, T, -1),
                        Hk=MLA_HEADS, G=1, dq=2 * LANES, dv=MLA_V, q_col0=0, k_col0=0, v_col0=0,
                        C=C, with_ctx=with_ctx)
        qkv_b = _head_prep(Pa, P_GQA, gqa_q_norm_g[layer].reshape(1, -1), gqa_k_norm_g[layer].reshape(1, -1),
                           rope_head, nbt, norm=True, n_q=GQA_HEADS, n_k=GQA_KV_HEADS).reshape(Bn, T, -1)
        Gb = GQA_HEADS // GQA_KV_HEADS
        yb = _attention(qkv_b, qkv_b, qkv_b, Hk=GQA_KV_HEADS, G=Gb, dq=HEAD_DIM, dv=HEAD_DIM,
                        q_col0=0, v_col0=GQA_HEADS * HEAD_DIM,
                        k_col0=(GQA_HEADS + 2 * GQA_KV_HEADS) * HEAD_DIM, C=C, with_ctx=with_ctx)
        conv_w8 = jnp.pad(gdn_conv_w[layer], ((0, 8 - GDN_CONV), (0, 0)))
        qkv_c = _gdn_prep(Pg3, conv_w8, C)
        par = jnp.zeros((8, LANES), F32)
        par = par.at[0, :2 * GDN_HEADS].set(-jnp.exp(gdn_a_log[layer].reshape(-1)))
        par = par.at[1, :2 * GDN_HEADS].set(gdn_dt_bias[layer].reshape(-1))
        of, ob = _gdn_chunks(qkv_c, Pg3, par, C)
        yc = _gdn_post(of.reshape(R, -1), ob.reshape(R, -1), Pg, gdn_norm_g[layer].reshape(1, -1))
        qkv_d = _head_prep(Pa, P_SWA, gqa_q_norm_g[layer].reshape(1, -1), gqa_k_norm_g[layer].reshape(1, -1),
                           rope_head, nbt, norm=False, n_q=SWA_HEADS, n_k=SWA_KV_HEADS).reshape(Bn, T, -1)
        yd = _swa_attention(qkv_d, swa_sink[layer], C=C, with_ctx=with_ctx)

        mixed = [ya.reshape(-1, GROUP_WIDTH), yb.reshape(-1, GROUP_WIDTH), yc,
                 yd.reshape(-1, GROUP_WIDTH)]
        w_o = w_out[layer].astype(BF16)
        if with_ctx:
            xc = _mm_res(mixed, w_o, xc, tab, which=2, tm=_pick_tile(R, (1024, 512, 256)),
                         tn=_pick_tile(D, (1024, 512)), **lay)
        else:
            lay = dict(nb=nbl, ncb=0, ctx_row=Bn)
            xc = _mm_res(mixed, w_o, xc, tab, which=2, tm=ROW_BLOCK, tn=D, **lay,
                         in_row_map=lambda i: (i // nbl) * nbt + ncb + i % nbl, out_rows=Bn * S)

        i2 = layer // 2
        if layer % 2 == 0:
            hid = _norm_swiglu(xc, norm2_g[layer], tab, ffn_w1[i2].astype(BF16),
                               ffn_w3[i2].astype(BF16), which=3, **lay)
            xc = _mm_res([hid], ffn_w2[i2].astype(BF16), xc, tab, which=5, tm=_pick_tile(xc.shape[0], (1024, 512)), tn=512, **lay)
        else:
            router_p = jnp.pad(moe_router[i2], ((0, 0), (0, LANES - N_EXPERTS)))
            h2, logits = _norm_mod(xc, norm2_g[layer], tab, which=3, router=router_p, **lay)
            pos, src_tok, egate, tile_e, tile_valid = _route(logits[:, :N_EXPERTS])
            xs = _gather_rows(h2, src_tok)
            ys = _gmm(xs, moe_w1[i2].astype(BF16), moe_w3[i2].astype(BF16), moe_w2[i2].astype(BF16),
                      tile_e, tile_valid)
            xc = _combine(ys, pos, egate, xc, tab, which=5, **lay)

    return _final_norm(xc, final_norm_g).reshape(Bn, S, D)
```

```python
import functools

import numpy as np
import jax
import jax.numpy as jnp
from jax import lax
from jax.experimental import pallas as pl
from jax.experimental.pallas import tpu as pltpu

F32 = jnp.float32
BF16 = jnp.bfloat16
HIGHEST = lax.Precision.HIGHEST

GRID_W = 64
HEAD_DIM = 128
ROPE_THETA = 10000.0
NORM_EPS = 1e-6
NEG_INF = -1e30
MLA_HEADS = 4
MLA_NOPE = 128
MLA_ROPE = 64
MLA_V = 128
MLA_Q_RANK = 384
MLA_KV_RANK = 256
GQA_HEADS = 4
GQA_KV_HEADS = 2
GDN_HEADS = 4
GDN_CONV = 5
GDN_CHUNK = 64
SWA_HEADS = 4
SWA_KV_HEADS = 2
SWA_WINDOW = 128
N_EXPERTS = 8
TOP_K = 2
GROUP_WIDTH = 512

ROW_BLOCK = 256
LANES = 128
SUBLANES = 8
MOE_TILE = 512
ATTN_CHAINS_PER_STEP = 4
VMEM_LIMIT = 56 * 1024 * 1024

P_GDN = 0
PG_WIDTH = 2048
P_GQA = 0
P_SWA = 1024
P_MLA = 2048
PA_WIDTH = 3072


def _cparams(sem, vmem=VMEM_LIMIT):
    return pltpu.CompilerParams(dimension_semantics=sem, vmem_limit_bytes=vmem)


def _table_row(gi, nb, ncb, ctx_row):
    b = gi // nb
    r = gi - b * nb
    return jnp.where(r < ncb, ctx_row, b)


def _mods_kernel(c_ref, w_ref, b_ref, o_ref):
    c = c_ref[...]
    a = (c * jax.nn.sigmoid(c)).astype(BF16)
    o_ref[0] = jnp.dot(a, w_ref[0].astype(BF16), preferred_element_type=F32) + b_ref[0]


def _mods(cmat, w_mod, b_mod):
    L, D, N = w_mod.shape
    M = cmat.shape[0]
    tn = 1024
    return pl.pallas_call(
        _mods_kernel,
        grid=(L, N // tn),
        in_specs=[pl.BlockSpec((M, D), lambda l, j: (0, 0)),
                  pl.BlockSpec((1, D, tn), lambda l, j: (l, 0, j)),
                  pl.BlockSpec((1, 1, tn), lambda l, j: (l, 0, j))],
        out_specs=pl.BlockSpec((1, M, tn), lambda l, j: (l, 0, j)),
        out_shape=jax.ShapeDtypeStruct((L, M, N), F32),
        compiler_params=_cparams(("parallel", "parallel")),
        name="mods",
    )(cmat, w_mod, b_mod.reshape(L, 1, N))


def _norm_router_kernel(x_ref, g_ref, tab_ref, r_ref, h_ref, l_ref, *, nb, ncb, ctx_row, which):
    idx = _table_row(pl.program_id(0), nb, ncb, ctx_row)
    x = x_ref[...]
    y = x * lax.rsqrt(jnp.mean(x * x, axis=-1, keepdims=True) + NORM_EPS) * g_ref[...]
    shift = tab_ref[pl.ds(idx * 6 + which, 1), :]
    scale = tab_ref[pl.ds(idx * 6 + which + 1, 1), :]
    h = y * (1.0 + scale) + shift
    h_ref[...] = h
    l_ref[...] = jnp.dot(h, r_ref[...], preferred_element_type=F32, precision=HIGHEST)


def _norm_router(x, g, tab, router, *, nb, ncb, ctx_row, which):
    R, D = x.shape
    kern = functools.partial(_norm_router_kernel, nb=nb, ncb=ncb, ctx_row=ctx_row, which=which)
    return pl.pallas_call(
        kern, grid=(R // ROW_BLOCK,),
        in_specs=[pl.BlockSpec((ROW_BLOCK, D), lambda i: (i, 0)),
                  pl.BlockSpec((1, D), lambda i: (0, 0)),
                  pl.BlockSpec(tab.shape, lambda i: (0, 0)),
                  pl.BlockSpec(router.shape, lambda i: (0, 0))],
        out_specs=[pl.BlockSpec((ROW_BLOCK, D), lambda i: (i, 0)),
                   pl.BlockSpec((ROW_BLOCK, LANES), lambda i: (i, 0))],
        out_shape=[jax.ShapeDtypeStruct((R, D), F32), jax.ShapeDtypeStruct((R, LANES), F32)],
        compiler_params=_cparams(("parallel",)), name="norm_router")(
            x, g.reshape(1, D), tab, router)


def _final_norm_kernel(x_ref, g_ref, o_ref):
    x = x_ref[...]
    o_ref[...] = x * lax.rsqrt(jnp.mean(x * x, axis=-1, keepdims=True) + NORM_EPS) * g_ref[...]


def _final_norm(x, g):
    R, D = x.shape
    return pl.pallas_call(
        _final_norm_kernel, grid=(R // ROW_BLOCK,),
        in_specs=[pl.BlockSpec((ROW_BLOCK, D), lambda i: (i, 0)),
                  pl.BlockSpec((1, D), lambda i: (0, 0))],
        out_specs=pl.BlockSpec((ROW_BLOCK, D), lambda i: (i, 0)),
        out_shape=jax.ShapeDtypeStruct((R, D), F32),
        compiler_params=_cparams(("parallel",)), name="final_norm")(x, g.reshape(1, D))


def _pick_tile(n, options):
    for t in options:
        if n % t == 0:
            return t
    raise ValueError(f"no tile for {n}")


def _norm_mod_rows(x_ref, g_ref, tab_ref, h_ref, *, tm, nb, ncb, ctx_row, which):
    i = pl.program_id(0)
    for c in range(tm // ROW_BLOCK):
        idx = _table_row(i * (tm // ROW_BLOCK) + c, nb, ncb, ctx_row)
        rows = slice(c * ROW_BLOCK, (c + 1) * ROW_BLOCK)
        x = x_ref[rows, :]
        y = x * lax.rsqrt(jnp.mean(x * x, axis=-1, keepdims=True) + NORM_EPS) * g_ref[...]
        shift = tab_ref[pl.ds(idx * 6 + which, 1), :]
        scale = tab_ref[pl.ds(idx * 6 + which + 1, 1), :]
        h_ref[rows, :] = (y * (1.0 + scale) + shift).astype(h_ref.dtype)


def _in_proj_kernel(x_ref, g_ref, tab_ref, w_ref, wab_ref, og_ref, oa_ref, oab_ref, h_ref, *, n_g,
                    **norm_args):
    j = pl.program_id(1)

    @pl.when(j == 0)
    def _():
        _norm_mod_rows(x_ref, g_ref, tab_ref, h_ref, **norm_args)
        oab_ref[...] = jnp.dot(h_ref[...], wab_ref[...], preferred_element_type=F32)

    acc = jnp.dot(h_ref[...], w_ref[...], preferred_element_type=F32)

    @pl.when(j < n_g)
    def _():
        og_ref[...] = acc

    @pl.when(j >= n_g)
    def _():
        oa_ref[...] = acc.astype(oa_ref.dtype)


def _in_proj(x, g, tab, w, w_ab, *, n_cols_g, nb, ncb, ctx_row, which):
    M, D = x.shape
    N = w.shape[1]
    tm = _pick_tile(M, (1024, 512, 256))
    tn = 1024
    assert n_cols_g % tn == 0 and (N - n_cols_g) % tn == 0
    n_g = n_cols_g // tn
    kern = functools.partial(_in_proj_kernel, n_g=n_g, tm=tm, nb=nb, ncb=ncb, ctx_row=ctx_row,
                             which=which)
    return pl.pallas_call(
        kern, grid=(M // tm, N // tn),
        in_specs=[pl.BlockSpec((tm, D), lambda i, j: (i, 0)),
                  pl.BlockSpec((1, D), lambda i, j: (0, 0)),
                  pl.BlockSpec(tab.shape, lambda i, j: (0, 0)),
                  pl.BlockSpec((D, tn), lambda i, j: (0, j)),
                  pl.BlockSpec(w_ab.shape, lambda i, j: (0, 0))],
        out_specs=[pl.BlockSpec((tm, tn), lambda i, j: (i, jnp.minimum(j, n_g - 1))),
                   pl.BlockSpec((tm, tn), lambda i, j: (i, jnp.maximum(j - n_g, 0))),
                   pl.BlockSpec((tm, w_ab.shape[1]), lambda i, j: (i, 0))],
        out_shape=[jax.ShapeDtypeStruct((M, n_cols_g), F32),
                   jax.ShapeDtypeStruct((M, N - n_cols_g), BF16),
                   jax.ShapeDtypeStruct((M, w_ab.shape[1]), F32)],
        scratch_shapes=[pltpu.VMEM((tm, D), BF16)],
        compiler_params=_cparams(("parallel", "arbitrary")), name="in_proj")(
            x, g.reshape(1, D), tab, w, w_ab)


def _norm_swiglu_kernel(x_ref, g_ref, tab_ref, w1_ref, w3_ref, o_ref, h_ref, **norm_args):
    @pl.when(pl.program_id(1) == 0)
    def _():
        _norm_mod_rows(x_ref, g_ref, tab_ref, h_ref, **norm_args)

    a = h_ref[...]
    h1 = jnp.dot(a, w1_ref[...], preferred_element_type=F32)
    h3 = jnp.dot(a, w3_ref[...], preferred_element_type=F32)
    o_ref[...] = (h1 * jax.nn.sigmoid(h1) * h3).astype(o_ref.dtype)


def _norm_swiglu(x, g, tab, w1, w3, *, nb, ncb, ctx_row, which):
    M, D = x.shape
    N = w1.shape[1]
    tm = _pick_tile(M, (1024, 512, 256))
    tn = _pick_tile(N, (512, 256, 128))
    kern = functools.partial(_norm_swiglu_kernel, tm=tm, nb=nb, ncb=ncb, ctx_row=ctx_row, which=which)
    return pl.pallas_call(
        kern, grid=(M // tm, N // tn),
        in_specs=[pl.BlockSpec((tm, D), lambda i, j: (i, 0)),
                  pl.BlockSpec((1, D), lambda i, j: (0, 0)),
                  pl.BlockSpec(tab.shape, lambda i, j: (0, 0)),
                  pl.BlockSpec((D, tn), lambda i, j: (0, j)),
                  pl.BlockSpec((D, tn), lambda i, j: (0, j))],
        out_specs=pl.BlockSpec((tm, tn), lambda i, j: (i, j)),
        out_shape=jax.ShapeDtypeStruct((M, N), BF16),
        scratch_shapes=[pltpu.VMEM((tm, D), BF16)],
        compiler_params=_cparams(("parallel", "arbitrary")), name="norm_swiglu")(
            x, g.reshape(1, D), tab, w1, w3)


def _mm_res_kernel(*refs, n_a, tm, tn, nb, ncb, ctx_row, which):
    a_refs = refs[:n_a]
    w_ref, res_ref, tab_ref, o_ref = refs[n_a:]
    i = pl.program_id(0)
    j = pl.program_id(1)
    a = a_refs[0][...] if n_a == 1 else jnp.concatenate([r[...] for r in a_refs], axis=1)
    acc = jnp.dot(a, w_ref[...], preferred_element_type=F32)
    for c in range(tm // ROW_BLOCK):
        idx = _table_row(i * (tm // ROW_BLOCK) + c, nb, ncb, ctx_row)
        gate = tab_ref[pl.ds(idx * 6 + which, 1), pl.ds(pl.multiple_of(j * tn, tn), tn)]
        rows = slice(c * ROW_BLOCK, (c + 1) * ROW_BLOCK)
        o_ref[rows, :] = res_ref[rows, :] + gate * acc[rows, :]


def _mm_res(a_list, w, res, tab, *, nb, ncb, ctx_row, which, tm, tn,
            in_row_map=None, out_rows=None):
    K, N = w.shape
    M = res.shape[0] if out_rows is None else out_rows
    rm = (lambda i: i) if in_row_map is None else in_row_map
    kern = functools.partial(_mm_res_kernel, n_a=len(a_list), tm=tm, tn=tn, nb=nb, ncb=ncb,
                             ctx_row=ctx_row, which=which)

    def a_spec(a):
        if a.shape[0] == res.shape[0]:
            return pl.BlockSpec((tm, a.shape[1]), lambda i, j: (rm(i), 0))
        assert a.shape[0] == M
        return pl.BlockSpec((tm, a.shape[1]), lambda i, j: (i, 0))

    in_specs = [a_spec(a) for a in a_list]
    in_specs += [pl.BlockSpec((K, tn), lambda i, j: (0, j)),
                 pl.BlockSpec((tm, tn), lambda i, j: (rm(i), j)),
                 pl.BlockSpec(tab.shape, lambda i, j: (0, 0))]
    return pl.pallas_call(
        kern, grid=(M // tm, N // tn), in_specs=in_specs,
        out_specs=pl.BlockSpec((tm, tn), lambda i, j: (i, j)),
        out_shape=jax.ShapeDtypeStruct((M, N), F32),
        compiler_params=_cparams(("parallel", "parallel")), name="mm_res")(*a_list, w, res, tab)


def _ones_column(rows):
    lane = lax.broadcasted_iota(jnp.int32, (rows, LANES), 1)
    return jnp.where(lane == 0, 1.0, 0.0).astype(BF16)


def _rope_lanes(x, cos, sin, rot):
    return x * cos + jnp.dot(x.astype(BF16), rot, preferred_element_type=F32) * sin


def _mean_sq(x):
    w = x.shape[1]
    return jnp.dot((x * x).astype(BF16), jnp.ones((w, w), BF16), preferred_element_type=F32) * (1.0 / w)


def _rope_tables(S, C, rot_dim):
    rows = S // GRID_W
    row = jnp.repeat(jnp.arange(rows), GRID_W).astype(F32)
    col = jnp.tile(jnp.arange(GRID_W), rows).astype(F32)
    half = rot_dim // 2
    inv_freq = ROPE_THETA ** (-jnp.arange(0, half, 2, dtype=F32) / half)
    ang_r = row[:, None] * inv_freq
    ang_c = col[:, None] * inv_freq
    ang = jnp.concatenate([ang_r, ang_r, ang_c, ang_c], axis=-1)
    cos, sin = jnp.cos(ang), jnp.sin(ang)
    quarter = rot_dim // 4
    pad = LANES - rot_dim

    def full(t, ctx_val):
        t = jnp.pad(t, ((0, 0), (0, pad)), constant_values=ctx_val)
        return jnp.concatenate([jnp.full((C, LANES), ctx_val, F32), t], axis=0)

    j = np.arange(rot_dim)
    even = (j // quarter) % 2 == 0
    rot = np.zeros((LANES, LANES), np.float32)
    rot[np.where(even, j + quarter, j - quarter), j] = np.where(even, -1.0, 1.0)
    return full(cos, 1.0), full(sin, 0.0), jnp.asarray(rot, BF16)


def _mla_prep_kernel(p_ref, qg_ref, kvg_ref, wq_ref, wkv_ref, cos_ref, sin_ref, rot_ref,
                     q_ref, k_ref, v_ref, *, scale):
    cos, sin, rot = cos_ref[...], sin_ref[...], rot_ref[...]
    cq = p_ref[:, 0:MLA_Q_RANK].astype(F32)
    qn = cq * lax.rsqrt(jnp.mean(cq * cq, axis=-1, keepdims=True) + NORM_EPS) * qg_ref[...]
    q_raw = jnp.dot(qn.astype(BF16), wq_ref[...], preferred_element_type=F32)
    ckv = p_ref[:, MLA_Q_RANK:MLA_Q_RANK + MLA_KV_RANK].astype(F32)
    kvn = ckv * lax.rsqrt(jnp.mean(ckv * ckv, axis=-1, keepdims=True) + NORM_EPS) * kvg_ref[...]
    kv = jnp.dot(kvn.astype(BF16), wkv_ref[...], preferred_element_type=F32)
    kr = p_ref[:, MLA_Q_RANK + MLA_KV_RANK:MLA_Q_RANK + MLA_KV_RANK + LANES].astype(F32)
    kr = _rope_lanes(kr, cos, sin, rot).astype(BF16)
    for h in range(MLA_HEADS):
        c0 = 2 * LANES * h
        q_ref[:, c0:c0 + LANES] = (q_raw[:, c0:c0 + LANES] * scale).astype(BF16)
        qr = _rope_lanes(q_raw[:, c0 + LANES:c0 + 2 * LANES], cos, sin, rot)
        q_ref[:, c0 + LANES:c0 + 2 * LANES] = (qr * scale).astype(BF16)
        k_ref[:, c0:c0 + LANES] = kv[:, LANES * h:LANES * (h + 1)].astype(BF16)
        k_ref[:, c0 + LANES:c0 + 2 * LANES] = kr
    ones_col = _ones_column(p_ref.shape[0])
    for h in range(MLA_HEADS):
        c0 = 2 * LANES * h
        v_ref[:, c0:c0 + LANES] = kv[:, (MLA_HEADS + h) * LANES:(MLA_HEADS + h + 1) * LANES].astype(BF16)
        v_ref[:, c0 + LANES:c0 + 2 * LANES] = ones_col


def _mla_prep(P, qg, kvg, wq_p, wkv_p, tabs, nbt):
    R = P.shape[0]
    width = MLA_HEADS * 2 * LANES
    kern = functools.partial(_mla_prep_kernel, scale=float((MLA_NOPE + MLA_ROPE) ** -0.5))
    tab_spec = pl.BlockSpec((ROW_BLOCK, LANES), lambda i: (i % nbt, 0))
    const = lambda a: pl.BlockSpec(a.shape, lambda i: (0, 0))
    return pl.pallas_call(
        kern, grid=(R // ROW_BLOCK,),
        in_specs=[pl.BlockSpec((ROW_BLOCK, 1024), lambda i: (i, P_MLA // 1024)),
                  const(qg), const(kvg), const(wq_p), const(wkv_p), tab_spec, tab_spec, const(tabs[2])],
        out_specs=[pl.BlockSpec((ROW_BLOCK, width), lambda i: (i, 0)),
                   pl.BlockSpec((ROW_BLOCK, width), lambda i: (i, 0)),
                   pl.BlockSpec((ROW_BLOCK, width), lambda i: (i, 0))],
        out_shape=[jax.ShapeDtypeStruct((R, width), BF16), jax.ShapeDtypeStruct((R, width), BF16),
                   jax.ShapeDtypeStruct((R, width), BF16)],
        compiler_params=_cparams(("parallel",)), name="mla_prep")(P, qg, kvg, wq_p, wkv_p, *tabs)


def _head_prep_kernel(p_ref, qg_ref, kg_ref, cos_ref, sin_ref, rot_ref, o_ref, *, norm, scale,
                      n_q, n_k):
    cos, sin, rot = cos_ref[...], sin_ref[...], rot_ref[...]
    for h in range(n_q + n_k):
        x = p_ref[:, LANES * h:LANES * (h + 1)].astype(F32)
        if norm:
            g = qg_ref[...] if h < n_q else kg_ref[...]
            x = x * lax.rsqrt(_mean_sq(x) + NORM_EPS) * g
        x = _rope_lanes(x, cos, sin, rot)
        if h < n_q:
            x = x * scale
        c0 = LANES * h if h < n_q else LANES * (h + 2 * n_k)
        o_ref[:, c0:c0 + LANES] = x.astype(BF16)
    v_in = LANES * (n_q + n_k)
    v_out = LANES * n_q
    ones_col = _ones_column(p_ref.shape[0])
    for h in range(n_k):
        o_ref[:, v_out + 2 * LANES * h:v_out + 2 * LANES * h + LANES] = (
            p_ref[:, v_in + LANES * h:v_in + LANES * (h + 1)])
        o_ref[:, v_out + 2 * LANES * h + LANES:v_out + 2 * LANES * (h + 1)] = ones_col


def _head_prep(P, col0, qg, kg, tabs, nbt, *, norm, n_q, n_k):
    R = P.shape[0]
    kern = functools.partial(_head_prep_kernel, norm=norm, scale=float(HEAD_DIM ** -0.5),
                             n_q=n_q, n_k=n_k)
    tab_spec = pl.BlockSpec((ROW_BLOCK, LANES), lambda i: (i % nbt, 0))
    const = lambda a: pl.BlockSpec(a.shape, lambda i: (0, 0))
    return pl.pallas_call(
        kern, grid=(R // ROW_BLOCK,),
        in_specs=[pl.BlockSpec((ROW_BLOCK, 1024), lambda i: (i, col0 // 1024)),
                  const(qg), const(kg), tab_spec, tab_spec, const(tabs[2])],
        out_specs=pl.BlockSpec((ROW_BLOCK, 1024 + n_k * LANES), lambda i: (i, 0)),
        out_shape=jax.ShapeDtypeStruct((R, 1024 + n_k * LANES), BF16),
        compiler_params=_cparams(("parallel",)), name="head_prep")(P, qg, kg, *tabs)


def _attn_kernel(q_ref, k_ref, v_ref, o_ref, *, HS, G, dq, dv, C, T, ctx_tile):
    def run(nk):
        nt = (((1,), (1,)), ((), ()))
        chains = [(hs, g) for hs in range(HS) for g in range(G)]
        s = []
        for hs, g in chains:
            q = q_ref[0, :, (hs * G + g) * dq:(hs * G + g + 1) * dq]
            k = k_ref[0, :nk, hs * dq:(hs + 1) * dq]
            s.append(lax.dot_general(q, k, nt, preferred_element_type=F32).astype(BF16))
        p = [jnp.exp(x - jnp.max(x, axis=-1, keepdims=True)) for x in s]
        for (hs, g), pc in zip(chains, p):
            o = jnp.dot(pc, v_ref[0, :nk, hs * 2 * dv:(hs + 1) * 2 * dv], preferred_element_type=F32)
            c0 = (hs * G + g) * dv
            o_ref[0, :, c0:c0 + dv] = (o[:, :dv] / o[:, dv:dv + 1]).astype(o_ref.dtype)

    if ctx_tile:
        qi = pl.program_id(2)
        pl.when(qi == 0)(lambda: run(C))
        pl.when(qi > 0)(lambda: run(T))
    else:
        run(T)


def _attention(q, k, v, *, Hk, G, dq, dv, q_col0, k_col0, v_col0, C, with_ctx):
    HS = ATTN_CHAINS_PER_STEP // G
    Bn, T, _ = q.shape
    ncb = C // ROW_BLOCK
    assert ncb == 1
    nq = T // ROW_BLOCK if with_ctx else (T - C) // ROW_BLOCK
    qoff = 0 if with_ctx else ncb
    assert Hk % HS == 0
    kern = functools.partial(_attn_kernel, HS=HS, G=G, dq=dq, dv=dv, C=C, T=T, ctx_tile=with_ctx)
    qw, kw, vw = HS * G * dq, HS * dq, HS * 2 * dv
    assert q_col0 % qw == 0 and k_col0 % kw == 0 and v_col0 % vw == 0
    return pl.pallas_call(
        kern, grid=(Bn, Hk // HS, nq),
        in_specs=[pl.BlockSpec((1, ROW_BLOCK, qw), lambda b, h, i: (b, i + qoff, q_col0 // qw + h)),
                  pl.BlockSpec((1, T, kw), lambda b, h, i: (b, 0, k_col0 // kw + h)),
                  pl.BlockSpec((1, T, vw), lambda b, h, i: (b, 0, v_col0 // vw + h))],
        out_specs=pl.BlockSpec((1, ROW_BLOCK, HS * G * dv), lambda b, h, i: (b, i, h)),
        out_shape=jax.ShapeDtypeStruct((Bn, nq * ROW_BLOCK, Hk * G * dv), BF16),
        compiler_params=_cparams(("parallel", "parallel", "parallel")), name="attention")(q, k, v)


def _swa_attention(qkv, sink, *, C, with_ctx):
    Bn, T, _ = qkv.shape
    Hq, Hk, d = SWA_HEADS, SWA_KV_HEADS, HEAD_DIM
    ncb = C // ROW_BLOCK
    assert ncb == 1 and T - C >= ROW_BLOCK + 2 * SWA_WINDOW
    nq = T // ROW_BLOCK if with_ctx else (T - C) // ROW_BLOCK
    qoff = 0 if with_ctx else ncb
    kern = functools.partial(_swa_kernel, C=C, S=T - C, ctx_tile=with_ctx, qoff=qoff)
    k_blk = (Hq * d + Hk * 2 * d) // (Hk * d)
    return pl.pallas_call(
        kern, grid=(Bn, nq),
        in_specs=[pl.BlockSpec(memory_space=pltpu.SMEM),
                  pl.BlockSpec((1, ROW_BLOCK, Hq * d), lambda b, i: (b, i + qoff, 0)),
                  pl.BlockSpec((1, T, Hk * d), lambda b, i: (b, 0, k_blk)),
                  pl.BlockSpec((1, T, Hk * 2 * d), lambda b, i: (b, 0, 1))],
        out_specs=pl.BlockSpec((1, ROW_BLOCK, Hq * d), lambda b, i: (b, i, 0)),
        out_shape=jax.ShapeDtypeStruct((Bn, nq * ROW_BLOCK, Hq * d), BF16),
        compiler_params=_cparams(("parallel", "parallel")), name="swa")(sink, qkv, qkv, qkv)


def _swa_kernel(sink_ref, q_ref, k_ref, v_ref, o_ref, *, C, S, ctx_tile, qoff):
    qi = pl.program_id(1) + qoff
    d = HEAD_DIM
    G = SWA_HEADS // SWA_KV_HEADS
    W = SWA_WINDOW
    WIN = ROW_BLOCK + 2 * W
    nt = (((1,), (1,)), ((), ()))
    heads = range(SWA_HEADS)

    def q_of(h):
        return q_ref[0, :, h * d:(h + 1) * d]

    def finish(parts):
        m16 = []
        for h in heads:
            m = jnp.full((ROW_BLOCK, 1), sink_ref[h], F32)
            for s, _ in parts[h]:
                m = jnp.maximum(m, jnp.max(s, axis=-1, keepdims=True).astype(F32))
            m16.append(m.astype(BF16))
        for h in heads:
            o = None
            for s, v in parts[h]:
                pv = jnp.dot(jnp.exp(s - m16[h]), v, preferred_element_type=F32)
                o = pv if o is None else o + pv
            den = jnp.exp(sink_ref[h] - m16[h].astype(F32)) + o[:, d:d + 1]
            o_ref[0, :, h * d:(h + 1) * d] = (o[:, :d] / den).astype(o_ref.dtype)

    def ctx_scores(h):
        kc = k_ref[0, :C, (h // G) * d:(h // G + 1) * d]
        return (lax.dot_general(q_of(h), kc, nt, preferred_element_type=F32).astype(BF16),
                v_ref[0, :C, (h // G) * 2 * d:(h // G + 1) * 2 * d])

    def run_ctx():
        finish([[ctx_scores(h)] for h in heads])

    def run_lat():
        q0 = (qi - (C // ROW_BLOCK)) * ROW_BLOCK
        ws = jnp.clip(q0 - W, 0, S - WIN)
        start = pl.multiple_of(C + ws, LANES)
        qpos = q0 + lax.broadcasted_iota(jnp.int32, (ROW_BLOCK, WIN), 0)
        kpos = ws + lax.broadcasted_iota(jnp.int32, (ROW_BLOCK, WIN), 1)
        valid = jnp.abs(kpos - qpos) <= W
        parts = []
        for h in heads:
            kw = k_ref[0, pl.ds(start, WIN), (h // G) * d:(h // G + 1) * d]
            vw = v_ref[0, pl.ds(start, WIN), (h // G) * 2 * d:(h // G + 1) * 2 * d]
            s_loc = jnp.where(valid, lax.dot_general(q_of(h), kw, nt, preferred_element_type=F32),
                              NEG_INF).astype(BF16)
            parts.append([(s_loc, vw), ctx_scores(h)])
        finish(parts)

    if ctx_tile:
        pl.when(qi == 0)(run_ctx)
        pl.when(qi > 0)(run_lat)
    else:
        run_lat()


def _gdn_prep_kernel(x_ref, w_ref, o_ref, pad_ref, *, C, T):
    j = pl.program_id(1)
    g = SUBLANES
    half = (GDN_CONV - 1) // 2
    assert half <= g
    zeros = jnp.zeros((g, LANES), F32)
    segments = ((0, C, g), (C, T - C, 2 * g + C))
    pad_ref[0:g, :] = zeros
    for lo, n, base in segments:
        pad_ref[base:base + n, :] = x_ref[0, lo:lo + n, :]
        pad_ref[base + n:base + n + g, :] = zeros
    nh = GDN_HEADS
    for lo, n, base in segments:
        acc = None
        for tap in range(GDN_CONV):
            term = pad_ref[base + tap - half:base + tap - half + n, :] * w_ref[tap:tap + 1, :]
            acc = term if acc is None else acc + term
        y = acc * jax.nn.sigmoid(acc)
        inv = lax.rsqrt(_mean_sq(y) * float(LANES) + 1e-6)
        f = jnp.where(j < 2 * nh, inv, 1.0) * jnp.where(j < nh, float(HEAD_DIM ** -0.5), 1.0)
        o_ref[0, lo:lo + n, :] = y * f


def _gdn_prep(P3, conv_w8, C):
    Bn, T, _ = P3.shape
    nblk = 3 * GDN_HEADS
    kern = functools.partial(_gdn_prep_kernel, C=C, T=T)
    return pl.pallas_call(
        kern, grid=(Bn, nblk),
        in_specs=[pl.BlockSpec((1, T, LANES), lambda b, j: (b, 0, P_GDN // LANES + j)),
                  pl.BlockSpec((8, LANES), lambda b, j: (0, j))],
        out_specs=pl.BlockSpec((1, T, LANES), lambda b, j: (b, 0, j)),
        out_shape=jax.ShapeDtypeStruct((Bn, T, nblk * LANES), F32),
        scratch_shapes=[pltpu.VMEM((T + 3 * SUBLANES, LANES), F32)],
        compiler_params=_cparams(("parallel", "parallel")), name="gdn_prep")(P3, conv_w8)


_GDN_INV_BASE = 8
GDN_BATCH_PER_STEP = 4


def _dot16(a, b):
    return jnp.dot(a, b, preferred_element_type=F32)


def _dot16_nt(a, b):
    return lax.dot_general(a, b, (((1,), (1,)), ((), ())), preferred_element_type=F32)


def _dot16_tn(a, b):
    return lax.dot_general(a, b, (((0,), (0,)), ((), ())), preferred_element_type=F32)


def _mask_dot(mask16, x):
    hi = x.astype(BF16)
    r1 = x - hi.astype(F32)
    mid = r1.astype(BF16)
    lo = (r1 - mid.astype(F32)).astype(BF16)
    w = x.shape[1]
    y = jnp.dot(mask16, jnp.concatenate([hi, mid, lo], axis=1), preferred_element_type=F32)
    return y[:, :w] + y[:, w:2 * w] + y[:, 2 * w:]


def _gdn_chunk_kernel(qf_ref, qb_ref, abf_ref, abb_ref, par_ref, of_ref, ob_ref, s_ref):
    Cc = GDN_CHUNK
    H = GDN_HEADS
    d = HEAD_DIM

    @pl.when(pl.program_id(1) == 0)
    def _():
        s_ref[...] = jnp.zeros_like(s_ref)

    ri = lax.broadcasted_iota(jnp.int32, (Cc, Cc), 0)
    ci = lax.broadcasted_iota(jnp.int32, (Cc, Cc), 1)
    eye = (ri == ci).astype(F32)
    neg_a = par_ref[0:1, :]
    dt_b = par_ref[1:2, :]
    base = _GDN_INV_BASE
    diag_blk = (ri // base) == (ci // base)
    off_blks = []
    s = base
    while s < Cc:
        off_blks.append(jnp.logical_and((ri // (2 * s)) == (ci // (2 * s)), (ri // s) != (ci // s)))
        s *= 2

    dirs = []
    for bb, direction in [(bb, direction) for bb in range(qf_ref.shape[0]) for direction in range(2)]:
        ab = (abf_ref if direction == 0 else abb_ref)[bb]
        if direction == 0:
            incl, incl_t, strict, last = ri >= ci, ri <= ci, ri > ci, Cc - 1
        else:
            incl, incl_t, strict, last = ri <= ci, ri >= ci, ri < ci, 0
        z = ab + dt_b
        sp = jnp.maximum(z, 0.0) + jnp.log1p(jnp.exp(-jnp.abs(z)))
        gl_all = neg_a * sp
        gc_all = _mask_dot(incl.astype(BF16), gl_all)
        g_tot = gc_all[last:last + 1, :]
        gc_sq = jnp.concatenate([gc_all, jnp.zeros((LANES - Cc, LANES), F32)], axis=0)
        dirs.append(dict(
            bb=bb, direction=direction,
            x_ref=qf_ref if direction == 0 else qb_ref, o_ref=of_ref if direction == 0 else ob_ref,
            incl=incl, strict=strict, gc_all=gc_all,
            gct_all=gc_sq.T[:, :Cc],
            beta_all=jax.nn.sigmoid(ab),
            eg_all=jnp.exp(gc_all), ekg_all=jnp.exp(g_tot - gc_all), egt_all=jnp.exp(g_tot)))

    units = [(dd, dd["direction"] * H + h, h) for dd in dirs for h in range(H)]
    U = range(len(units))

    k16, kb, decay, a_mat = [], [], [], []
    for dd, lane, h in units:
        k = dd["x_ref"][dd["bb"],:, (H + h) * d:(H + h + 1) * d]
        beta = dd["beta_all"][:, 2 * H + lane:2 * H + lane + 1]
        diff = dd["gc_all"][:, lane:lane + 1] - dd["gct_all"][lane:lane + 1, :]
        decay.append(jnp.where(dd["incl"], jnp.exp(jnp.where(dd["incl"], diff, 0.0)), 0.0))
        kb.append(k * beta)
        k16.append(k.astype(BF16))
    for i in U:
        a_mat.append(jnp.where(units[i][0]["strict"],
                               _dot16_nt(kb[i].astype(BF16), k16[i]) * decay[i], 0.0))

    npow = [jnp.where(diag_blk, a, 0.0).astype(BF16) for a in a_mat]
    t = [eye - jnp.where(diag_blk, a, 0.0) for a in a_mat]
    sq = 2 * base
    while sq > 4:
        npow = [_dot16(n, n).astype(BF16) for n in npow]
        t = [t[i] + _dot16(t[i].astype(BF16), npow[i]) for i in U]
        sq //= 2
    for off_blk in off_blks:
        t16 = [x.astype(BF16) for x in t]
        xo = [_dot16(t16[i], jnp.where(off_blk, a_mat[i], 0.0).astype(BF16)).astype(BF16) for i in U]
        t = [t[i] - _dot16(xo[i], t16[i]) for i in U]

    uw, intra, qg = [], [], []
    for i in U:
        dd, lane, h = units[i]
        q = dd["x_ref"][dd["bb"],:, h * d:(h + 1) * d]
        v = dd["x_ref"][dd["bb"],:, (2 * H + h) * d:(2 * H + h + 1) * d]
        beta = dd["beta_all"][:, 2 * H + lane:2 * H + lane + 1]
        eg = dd["eg_all"][:, lane:lane + 1]
        rhs = jnp.concatenate([v * beta, kb[i] * eg], axis=1).astype(BF16)
        uw.append(_dot16(t[i].astype(BF16), rhs))
        intra.append(jnp.where(dd["incl"], _dot16_nt(q.astype(BF16), k16[i]) * decay[i], 0.0)
                     .astype(BF16))
        qg.append((q * eg).astype(BF16))
    state = [s_ref[units[i][0]["bb"] * 2 * H + units[i][1]] for i in U]
    s16 = [x.astype(BF16) for x in state]
    v_new = [uw[i][:, :d] - _dot16(uw[i][:, d:].astype(BF16), s16[i]) for i in U]
    vn16 = [x.astype(BF16) for x in v_new]
    for i in U:
        dd, lane, h = units[i]
        dd["o_ref"][dd["bb"], :, h * d:(h + 1) * d] = (_dot16(qg[i], s16[i])
                                                       + _dot16(intra[i], vn16[i]))
    for i in U:
        dd, lane, h = units[i]
        k = dd["x_ref"][dd["bb"],:, (H + h) * d:(H + h + 1) * d]
        kg = (k * dd["ekg_all"][:, lane:lane + 1]).astype(BF16)
        s_ref[dd["bb"] * 2 * H + lane] = (state[i] * dd["egt_all"][:, lane:lane + 1]
                                          + _dot16_tn(kg, vn16[i]))


def _gdn_chunks(qkv, P3, par, C):
    Bn, T, _ = qkv.shape
    Cc = GDN_CHUNK
    nc = T // Cc
    ncc = C // Cc
    H = GDN_HEADS

    def bwd_chunk(s):
        return jnp.where(s < ncc, ncc - 1 - s, nc + ncc - 1 - s)

    ab_blk = 0
    nbb = _pick_tile(Bn, (GDN_BATCH_PER_STEP, 1))
    return pl.pallas_call(
        _gdn_chunk_kernel, grid=(Bn // nbb, nc),
        in_specs=[pl.BlockSpec((nbb, Cc, 3 * H * HEAD_DIM), lambda b, s: (b, s, 0)),
                  pl.BlockSpec((nbb, Cc, 3 * H * HEAD_DIM), lambda b, s: (b, bwd_chunk(s), 0)),
                  pl.BlockSpec((nbb, Cc, LANES), lambda b, s: (b, s, ab_blk)),
                  pl.BlockSpec((nbb, Cc, LANES), lambda b, s: (b, bwd_chunk(s), ab_blk)),
                  pl.BlockSpec((8, LANES), lambda b, s: (0, 0))],
        out_specs=[pl.BlockSpec((nbb, Cc, H * HEAD_DIM), lambda b, s: (b, s, 0)),
                   pl.BlockSpec((nbb, Cc, H * HEAD_DIM), lambda b, s: (b, bwd_chunk(s), 0))],
        out_shape=[jax.ShapeDtypeStruct((Bn, T, H * HEAD_DIM), F32),
                   jax.ShapeDtypeStruct((Bn, T, H * HEAD_DIM), F32)],
        scratch_shapes=[pltpu.VMEM((nbb * 2 * H, HEAD_DIM, HEAD_DIM), F32)],
        compiler_params=_cparams(("parallel", "arbitrary")), name="gdn_chunks")(
            qkv, qkv, P3, P3, par)


def _gdn_post_kernel(of_ref, ob_ref, z_ref, g_ref, o_ref):
    d = HEAD_DIM
    for h in range(GDN_HEADS):
        o = of_ref[:, h * d:(h + 1) * d] + ob_ref[:, h * d:(h + 1) * d]
        y = o * lax.rsqrt(_mean_sq(o) + NORM_EPS) * g_ref[...]
        z = z_ref[:, h * d:(h + 1) * d]
        o_ref[:, h * d:(h + 1) * d] = (y * (z * jax.nn.sigmoid(z))).astype(o_ref.dtype)


def _gdn_post(of, ob, P, g):
    R, W = of.shape
    zblk = (P_GDN + 3 * W) // W
    return pl.pallas_call(
        _gdn_post_kernel, grid=(R // ROW_BLOCK,),
        in_specs=[pl.BlockSpec((ROW_BLOCK, W), lambda i: (i, 0)),
                  pl.BlockSpec((ROW_BLOCK, W), lambda i: (i, 0)),
                  pl.BlockSpec((ROW_BLOCK, W), lambda i: (i, zblk)),
                  pl.BlockSpec((1, HEAD_DIM), lambda i: (0, 0))],
        out_specs=pl.BlockSpec((ROW_BLOCK, W), lambda i: (i, 0)),
        out_shape=jax.ShapeDtypeStruct((R, W), BF16),
        compiler_params=_cparams(("parallel",)), name="gdn_post")(of, ob, P, g)


def _gather_rows_kernel(tok_ref, tok_next_ref, h_hbm, o_ref, buf, sem):
    i = pl.program_id(0)
    nt = pl.num_programs(0)
    n = buf.shape[1]
    slot = i % 2

    def issue(t_ref, s):
        def body(it, c):
            for p in range(2):
                r = 2 * it + p
                pltpu.make_async_copy(h_hbm.at[pl.ds(t_ref[0, 0, r], 1), :],
                                      buf.at[s, pl.ds(r, 1), :], sem.at[s]).start(priority=p)
            return c
        lax.fori_loop(0, n // 2, body, 0, unroll=4)

    @pl.when(i == 0)
    def _():
        issue(tok_ref, 0)

    @pl.when(i + 1 < nt)
    def _():
        issue(tok_next_ref, 1 - slot)

    pltpu.make_async_copy(h_hbm.at[pl.ds(0, n), :], buf.at[slot], sem.at[slot]).wait()
    o_ref[...] = buf[slot].astype(o_ref.dtype)


def _gather_rows(h, src_tok):
    D = h.shape[1]
    n = src_tok.shape[0]
    nt = n // ROW_BLOCK
    return pl.pallas_call(
        _gather_rows_kernel, grid=(nt,),
        in_specs=[pl.BlockSpec((1, 1, ROW_BLOCK), lambda i: (i, 0, 0), memory_space=pltpu.SMEM),
                  pl.BlockSpec((1, 1, ROW_BLOCK), lambda i: (jnp.minimum(i + 1, nt - 1), 0, 0),
                               memory_space=pltpu.SMEM),
                  pl.BlockSpec(memory_space=pl.ANY)],
        out_specs=pl.BlockSpec((ROW_BLOCK, D), lambda i: (i, 0)),
        out_shape=jax.ShapeDtypeStruct((n, D), BF16),
        scratch_shapes=[pltpu.VMEM((2, ROW_BLOCK, D), F32), pltpu.SemaphoreType.DMA((2,))],
        compiler_params=_cparams(("arbitrary",)), name="moe_gather")(
            src_tok.reshape(nt, 1, ROW_BLOCK), src_tok.reshape(nt, 1, ROW_BLOCK), h)


def _gmm_kernel(te_ref, tv_ref, x_ref, w1_ref, w3_ref, w2_ref, o_ref):
    i = pl.program_id(0)
    f = pl.program_id(1)

    @pl.when(f == 0)
    def _():
        o_ref[...] = jnp.zeros_like(o_ref)

    @pl.when(tv_ref[i] == 1)
    def _():
        x = x_ref[...]
        h1 = jnp.dot(x, w1_ref[0], preferred_element_type=F32)
        h3 = jnp.dot(x, w3_ref[0], preferred_element_type=F32)
        hid = (h1 * jax.nn.sigmoid(h1) * h3).astype(BF16)
        o_ref[...] += jnp.dot(hid, w2_ref[0], preferred_element_type=F32)


def _gmm(xs, w1, w3, w2, tile_e, tile_valid):
    n, D = xs.shape
    E, _, F = w1.shape
    tf = 1024
    nt = n // MOE_TILE
    grid_spec = pltpu.PrefetchScalarGridSpec(
        num_scalar_prefetch=2, grid=(nt, F // tf),
        in_specs=[pl.BlockSpec((MOE_TILE, D), lambda i, f, te, tv: (i, 0)),
                  pl.BlockSpec((1, D, tf), lambda i, f, te, tv: (te[i], 0, f)),
                  pl.BlockSpec((1, D, tf), lambda i, f, te, tv: (te[i], 0, f)),
                  pl.BlockSpec((1, tf, D), lambda i, f, te, tv: (te[i], f, 0))],
        out_specs=pl.BlockSpec((MOE_TILE, D), lambda i, f, te, tv: (i, 0)))
    return pl.pallas_call(
        _gmm_kernel, grid_spec=grid_spec,
        out_shape=jax.ShapeDtypeStruct((n, D), F32),
        compiler_params=_cparams(("parallel", "arbitrary")), name="moe_gmm")(
            tile_e, tile_valid, xs, w1, w3, w2)


def _combine_kernel(pos_ref, pos_next_ref, ys_hbm, x_ref, eg_ref, tab_ref, o_ref, buf, sem, *, nb,
                    ncb, ctx_row, which):
    i = pl.program_id(0)
    nt = pl.num_programs(0)
    n = buf.shape[2]
    slot = i % 2
    idx = _table_row(i, nb, ncb, ctx_row)

    def issue(p_ref, s):
        def body(r, c):
            for k in range(TOP_K):
                pltpu.make_async_copy(ys_hbm.at[pl.ds(p_ref[0, k, r], 1), :],
                                      buf.at[s, k, pl.ds(r, 1), :], sem.at[s]).start(priority=k % 2)
            return c
        lax.fori_loop(0, n, body, 0, unroll=4)

    @pl.when(i == 0)
    def _():
        issue(pos_ref, 0)

    @pl.when(i + 1 < nt)
    def _():
        issue(pos_next_ref, 1 - slot)

    for k in range(TOP_K):
        pltpu.make_async_copy(ys_hbm.at[pl.ds(0, n), :], buf.at[slot, k], sem.at[slot]).wait()
    gate = tab_ref[pl.ds(idx * 6 + which, 1), :]
    eg = eg_ref[...]
    y = eg[:, 0:1] * buf[slot, 0]
    for k in range(1, TOP_K):
        y = y + eg[:, k:k + 1] * buf[slot, k]
    o_ref[...] = x_ref[...] + gate * y


def _combine(ys, pos, egate, x, tab, *, nb, ncb, ctx_row, which):
    R, D = x.shape
    nt = R // ROW_BLOCK
    pos3 = pos.reshape(nt, ROW_BLOCK, TOP_K).transpose(0, 2, 1)
    kern = functools.partial(_combine_kernel, nb=nb, ncb=ncb, ctx_row=ctx_row, which=which)
    return pl.pallas_call(
        kern, grid=(nt,),
        in_specs=[pl.BlockSpec((1, TOP_K, ROW_BLOCK), lambda i: (i, 0, 0), memory_space=pltpu.SMEM),
                  pl.BlockSpec((1, TOP_K, ROW_BLOCK), lambda i: (jnp.minimum(i + 1, nt - 1), 0, 0),
                               memory_space=pltpu.SMEM),
                  pl.BlockSpec(memory_space=pl.ANY),
                  pl.BlockSpec((ROW_BLOCK, D), lambda i: (i, 0)),
                  pl.BlockSpec((ROW_BLOCK, LANES), lambda i: (i, 0)),
                  pl.BlockSpec(tab.shape, lambda i: (0, 0))],
        out_specs=pl.BlockSpec((ROW_BLOCK, D), lambda i: (i, 0)),
        out_shape=jax.ShapeDtypeStruct((R, D), F32),
        scratch_shapes=[pltpu.VMEM((2, TOP_K, ROW_BLOCK, D), F32), pltpu.SemaphoreType.DMA((2,))],
        compiler_params=_cparams(("arbitrary",)), name="moe_combine")(pos3, pos3, ys, x, egate, tab)


def _route(logits):
    R = logits.shape[0]
    E = N_EXPERTS
    top_v, top_i = lax.top_k(logits, TOP_K)
    gate = jax.nn.softmax(top_v, axis=-1)
    onehot = jnp.sum((top_i[:, :, None] == jnp.arange(E)[None, None, :]).astype(jnp.int32), axis=1)
    before = jnp.cumsum(onehot, axis=0) - onehot
    sizes = jnp.sum(onehot, axis=0)
    padded = ((sizes + MOE_TILE - 1) // MOE_TILE) * MOE_TILE
    ends = jnp.cumsum(padded)
    off = ends - padded
    pos = off[top_i] + jnp.take_along_axis(before, top_i, axis=1)
    nt = (TOP_K * R) // MOE_TILE + E
    n = nt * MOE_TILE
    flat = pos.reshape(-1)
    src_tok = jnp.zeros((n,), jnp.int32).at[flat].set(jnp.repeat(jnp.arange(R, dtype=jnp.int32), TOP_K))
    egate = jnp.pad(gate, ((0, 0), (0, LANES - TOP_K)))
    tile_start = jnp.arange(nt, dtype=jnp.int32) * MOE_TILE
    tile_valid = (tile_start < ends[-1]).astype(jnp.int32)
    tile_e = jnp.searchsorted(ends, tile_start, side="right").astype(jnp.int32)
    last_e = jnp.max(jnp.where(tile_valid == 1, tile_e, 0))
    tile_e = jnp.where(tile_valid == 1, tile_e, last_e)
    return pos.astype(jnp.int32), src_tok, egate, tile_e, tile_valid


def _prep_w_in(w):
    D = w.shape[0]
    z = lambda n: jnp.zeros((D, n), w.dtype)
    w_g = w[:, 1728:3776]
    w_a = jnp.concatenate([w[:, 704:1728], w[:, 3792:4816], w[:, 0:704], z(PA_WIDTH - 2048 - 704)],
                          axis=1)
    w_ab = jnp.concatenate([w[:, 3776:3792], z(LANES - 16)], axis=1)
    assert w_g.shape[1] == PG_WIDTH and w_a.shape[1] == PA_WIDTH
    return jnp.concatenate([w_g, w_a], axis=1).astype(BF16), w_ab.astype(BF16)


def _prep_w_uq(w):
    r = w.shape[0]
    w = w.reshape(r, MLA_HEADS, MLA_NOPE + MLA_ROPE)
    w = jnp.pad(w, ((0, 0), (0, 0), (0, 2 * LANES - (MLA_NOPE + MLA_ROPE))))
    return w.reshape(r, MLA_HEADS * 2 * LANES).astype(BF16)


def _prep_w_ukv(w):
    r = w.shape[0]
    w = w.reshape(r, MLA_HEADS, MLA_NOPE + MLA_V)
    return jnp.concatenate([w[:, :, :MLA_NOPE].reshape(r, -1), w[:, :, MLA_NOPE:].reshape(r, -1)],
                           axis=1).astype(BF16)


def kernel(x, c, ctx, c_ctx, norm1_g, norm2_g, w_mod, b_mod, w_in, mla_q_norm_g, mla_kv_norm_g,
           mla_w_uq, mla_w_ukv, gqa_q_norm_g, gqa_k_norm_g, gdn_conv_w, gdn_a_log, gdn_dt_bias,
           gdn_norm_g, swa_sink, w_out, ffn_w1, ffn_w3, ffn_w2, moe_router, moe_w1, moe_w3, moe_w2,
           final_norm_g):
    Bn, S, D = x.shape
    C = ctx.shape[1]
    T = C + S
    R = Bn * T
    depth = w_in.shape[0]
    nbt = T // ROW_BLOCK
    ncb = C // ROW_BLOCK
    nbl = S // ROW_BLOCK

    m_rows = ((Bn + 1 + 7) // 8) * 8
    cmat = jnp.zeros((m_rows, D), F32).at[:Bn].set(c).at[Bn].set(c_ctx)
    mods = _mods(cmat, w_mod, b_mod).reshape(depth, m_rows * 6, D)

    rope_mla = _rope_tables(S, C, MLA_ROPE)
    rope_head = _rope_tables(S, C, HEAD_DIM)

    xc = jnp.concatenate([ctx, x], axis=1).reshape(R, D)
    lay = dict(nb=nbt, ncb=ncb, ctx_row=Bn)

    for layer in range(depth):
        with_ctx = layer < depth - 1
        tab = mods[layer]
        Pg, Pa, Pab = _in_proj(xc, norm1_g[layer], tab, *_prep_w_in(w_in[layer]), n_cols_g=PG_WIDTH,
                          which=0, **lay)
        Pg3 = Pg.reshape(Bn, T, PG_WIDTH)

        qa, ka, va = _mla_prep(Pa, mla_q_norm_g[layer].reshape(1, -1), mla_kv_norm_g[layer].reshape(1, -1),
                               _prep_w_uq(mla_w_uq[layer]), _prep_w_ukv(mla_w_ukv[layer]), rope_mla, nbt)
        ya = _attention(qa.reshape(Bn, T, -1), ka.reshape(Bn, T, -1), va.reshape(Bn, T, -1),
                        Hk=MLA_HEADS, G=1, dq=2 * LANES, dv=MLA_V, q_col0=0, k_col0=0, v_col0=0,
                        C=C, with_ctx=with_ctx)
        qkv_b = _head_prep(Pa, P_GQA, gqa_q_norm_g[layer].reshape(1, -1), gqa_k_norm_g[layer].reshape(1, -1),
                           rope_head, nbt, norm=True, n_q=GQA_HEADS, n_k=GQA_KV_HEADS).reshape(Bn, T, -1)
        Gb = GQA_HEADS // GQA_KV_HEADS
        yb = _attention(qkv_b, qkv_b, qkv_b, Hk=GQA_KV_HEADS, G=Gb, dq=HEAD_DIM, dv=HEAD_DIM,
                        q_col0=0, v_col0=GQA_HEADS * HEAD_DIM,
                        k_col0=(GQA_HEADS + 2 * GQA_KV_HEADS) * HEAD_DIM, C=C, with_ctx=with_ctx)
        conv_w8 = jnp.pad(gdn_conv_w[layer], ((0, 8 - GDN_CONV), (0, 0)))
        qkv_c = _gdn_prep(Pg3, conv_w8, C)
        par = jnp.zeros((8, LANES), F32)
        par = par.at[0, :2 * GDN_HEADS].set(-jnp.exp(gdn_a_log[layer].reshape(-1)))
        par = par.at[1, :2 * GDN_HEADS].set(gdn_dt_bias[layer].reshape(-1))
        of, ob = _gdn_chunks(qkv_c, Pab.reshape(Bn, T, LANES), par, C)
        yc = _gdn_post(of.reshape(R, -1), ob.reshape(R, -1), Pg, gdn_norm_g[layer].reshape(1, -1))
        qkv_d = _head_prep(Pa, P_SWA, gqa_q_norm_g[layer].reshape(1, -1), gqa_k_norm_g[layer].reshape(1, -1),
                           rope_head, nbt, norm=False, n_q=SWA_HEADS, n_k=SWA_KV_HEADS).reshape(Bn, T, -1)
        yd = _swa_attention(qkv_d, swa_sink[layer], C=C, with_ctx=with_ctx)

        mixed = [ya.reshape(-1, GROUP_WIDTH), yb.reshape(-1, GROUP_WIDTH), yc,
                 yd.reshape(-1, GROUP_WIDTH)]
        w_o = w_out[layer].astype(BF16)
        if with_ctx:
            xc = _mm_res(mixed, w_o, xc, tab, which=2, tm=_pick_tile(R, (1024, 512, 256)),
                         tn=_pick_tile(D, (1024, 512)), **lay)
        else:
            lay = dict(nb=nbl, ncb=0, ctx_row=Bn)
            xc = _mm_res(mixed, w_o, xc, tab, which=2, tm=ROW_BLOCK, tn=D, **lay,
                         in_row_map=lambda i: (i // nbl) * nbt + ncb + i % nbl, out_rows=Bn * S)

        i2 = layer // 2
        if layer % 2 == 0:
            hid = _norm_swiglu(xc, norm2_g[layer], tab, ffn_w1[i2].astype(BF16),
                               ffn_w3[i2].astype(BF16), which=3, **lay)
            xc = _mm_res([hid], ffn_w2[i2].astype(BF16), xc, tab, which=5, tm=_pick_tile(xc.shape[0], (1024, 512)), tn=512, **lay)
        else:
            router_p = jnp.pad(moe_router[i2], ((0, 0), (0, LANES - N_EXPERTS)))
            h2, logits = _norm_router(xc, norm2_g[layer], tab, router_p, which=3, **lay)
            pos, src_tok, egate, tile_e, tile_valid = _route(logits[:, :N_EXPERTS])
            xs = _gather_rows(h2, src_tok)
            ys = _gmm(xs, moe_w1[i2].astype(BF16), moe_w3[i2].astype(BF16), moe_w2[i2].astype(BF16),
                      tile_e, tile_valid)
            xc = _combine(ys, pos, egate, xc, tab, which=5, **lay)

    return _final_norm(xc, final_norm_g).reshape(Bn, S, D)
```

```python
import functools

import numpy as np
import jax
import jax.numpy as jnp
from jax import lax
from jax.experimental import pallas as pl
from jax.experimental.pallas import tpu as pltpu

F32 = jnp.float32
BF16 = jnp.bfloat16
HIGHEST = lax.Precision.HIGHEST

GRID_W = 64
HEAD_DIM = 128
ROPE_THETA = 10000.0
NORM_EPS = 1e-6
NEG_INF = -1e30
MLA_HEADS = 4
MLA_NOPE = 128
MLA_ROPE = 64
MLA_V = 128
MLA_Q_RANK = 384
MLA_KV_RANK = 256
GQA_HEADS = 4
GQA_KV_HEADS = 2
GDN_HEADS = 4
GDN_CONV = 5
GDN_CHUNK = 64
SWA_HEADS = 4
SWA_KV_HEADS = 2
SWA_WINDOW = 128
N_EXPERTS = 8
TOP_K = 2
GROUP_WIDTH = 512

ROW_BLOCK = 256
LANES = 128
SUBLANES = 8
MOE_TILE = 512
ATTN_CHAINS_PER_STEP = 4
VMEM_LIMIT = 56 * 1024 * 1024

P_GDN = 0
PG_WIDTH = 2048
P_GQA = 0
P_SWA = 1024
P_MLA = 2048
PA_WIDTH = 3072


def _cparams(sem, vmem=VMEM_LIMIT):
    return pltpu.CompilerParams(dimension_semantics=sem, vmem_limit_bytes=vmem)


def _table_row(gi, nb, ncb, ctx_row):
    b = gi // nb
    r = gi - b * nb
    return jnp.where(r < ncb, ctx_row, b)


def _mods_kernel(c_ref, w_ref, b_ref, o_ref):
    c = c_ref[...]
    a = (c * jax.nn.sigmoid(c)).astype(BF16)
    o_ref[0] = jnp.dot(a, w_ref[0].astype(BF16), preferred_element_type=F32) + b_ref[0]


def _mods(cmat, w_mod, b_mod):
    L, D, N = w_mod.shape
    M = cmat.shape[0]
    tn = 1024
    return pl.pallas_call(
        _mods_kernel,
        grid=(L, N // tn),
        in_specs=[pl.BlockSpec((M, D), lambda l, j: (0, 0)),
                  pl.BlockSpec((1, D, tn), lambda l, j: (l, 0, j)),
                  pl.BlockSpec((1, 1, tn), lambda l, j: (l, 0, j))],
        out_specs=pl.BlockSpec((1, M, tn), lambda l, j: (l, 0, j)),
        out_shape=jax.ShapeDtypeStruct((L, M, N), F32),
        compiler_params=_cparams(("parallel", "parallel")),
        name="mods",
    )(cmat, w_mod, b_mod.reshape(L, 1, N))


def _norm_router_kernel(x_ref, g_ref, tab_ref, r_ref, h_ref, l_ref, *, nb, ncb, ctx_row, which):
    idx = _table_row(pl.program_id(0), nb, ncb, ctx_row)
    x = x_ref[...]
    y = x * lax.rsqrt(jnp.mean(x * x, axis=-1, keepdims=True) + NORM_EPS) * g_ref[...]
    shift = tab_ref[pl.ds(idx * 6 + which, 1), :]
    scale = tab_ref[pl.ds(idx * 6 + which + 1, 1), :]
    h = y * (1.0 + scale) + shift
    h_ref[...] = h
    l_ref[...] = jnp.dot(h, r_ref[...], preferred_element_type=F32, precision=HIGHEST)


def _norm_router(x, g, tab, router, *, nb, ncb, ctx_row, which):
    R, D = x.shape
    kern = functools.partial(_norm_router_kernel, nb=nb, ncb=ncb, ctx_row=ctx_row, which=which)
    return pl.pallas_call(
        kern, grid=(R // ROW_BLOCK,),
        in_specs=[pl.BlockSpec((ROW_BLOCK, D), lambda i: (i, 0)),
                  pl.BlockSpec((1, D), lambda i: (0, 0)),
                  pl.BlockSpec(tab.shape, lambda i: (0, 0)),
                  pl.BlockSpec(router.shape, lambda i: (0, 0))],
        out_specs=[pl.BlockSpec((ROW_BLOCK, D), lambda i: (i, 0)),
                   pl.BlockSpec((ROW_BLOCK, LANES), lambda i: (i, 0))],
        out_shape=[jax.ShapeDtypeStruct((R, D), F32), jax.ShapeDtypeStruct((R, LANES), F32)],
        compiler_params=_cparams(("parallel",)), name="norm_router")(
            x, g.reshape(1, D), tab, router)


def _final_norm_kernel(x_ref, g_ref, o_ref):
    x = x_ref[...]
    o_ref[...] = x * lax.rsqrt(jnp.mean(x * x, axis=-1, keepdims=True) + NORM_EPS) * g_ref[...]


def _final_norm(x, g):
    R, D = x.shape
    return pl.pallas_call(
        _final_norm_kernel, grid=(R // ROW_BLOCK,),
        in_specs=[pl.BlockSpec((ROW_BLOCK, D), lambda i: (i, 0)),
                  pl.BlockSpec((1, D), lambda i: (0, 0))],
        out_specs=pl.BlockSpec((ROW_BLOCK, D), lambda i: (i, 0)),
        out_shape=jax.ShapeDtypeStruct((R, D), F32),
        compiler_params=_cparams(("parallel",)), name="final_norm")(x, g.reshape(1, D))


def _pick_tile(n, options):
    for t in options:
        if n % t == 0:
            return t
    raise ValueError(f"no tile for {n}")


def _norm_mod_rows(x_ref, g_ref, tab_ref, h_ref, *, tm, nb, ncb, ctx_row, which):
    i = pl.program_id(0)
    for c in range(tm // ROW_BLOCK):
        idx = _table_row(i * (tm // ROW_BLOCK) + c, nb, ncb, ctx_row)
        rows = slice(c * ROW_BLOCK, (c + 1) * ROW_BLOCK)
        x = x_ref[rows, :]
        y = x * lax.rsqrt(jnp.mean(x * x, axis=-1, keepdims=True) + NORM_EPS) * g_ref[...]
        shift = tab_ref[pl.ds(idx * 6 + which, 1), :]
        scale = tab_ref[pl.ds(idx * 6 + which + 1, 1), :]
        h_ref[rows, :] = (y * (1.0 + scale) + shift).astype(h_ref.dtype)


def _in_proj_kernel(x_ref, g_ref, tab_ref, w_ref, wab_ref, og_ref, oa_ref, oab_ref, h_ref, *, n_g,
                    **norm_args):
    j = pl.program_id(1)

    @pl.when(j == 0)
    def _():
        _norm_mod_rows(x_ref, g_ref, tab_ref, h_ref, **norm_args)
        oab_ref[...] = jnp.dot(h_ref[...], wab_ref[...], preferred_element_type=F32)

    acc = jnp.dot(h_ref[...], w_ref[...], preferred_element_type=F32)

    @pl.when(j < n_g)
    def _():
        og_ref[...] = acc

    @pl.when(j >= n_g)
    def _():
        oa_ref[...] = acc.astype(oa_ref.dtype)


def _in_proj(x, g, tab, w, w_ab, *, n_cols_g, nb, ncb, ctx_row, which):
    M, D = x.shape
    N = w.shape[1]
    tm = _pick_tile(M, (1024, 512, 256))
    tn = 1024
    assert n_cols_g % tn == 0 and (N - n_cols_g) % tn == 0
    n_g = n_cols_g // tn
    kern = functools.partial(_in_proj_kernel, n_g=n_g, tm=tm, nb=nb, ncb=ncb, ctx_row=ctx_row,
                             which=which)
    return pl.pallas_call(
        kern, grid=(M // tm, N // tn),
        in_specs=[pl.BlockSpec((tm, D), lambda i, j: (i, 0)),
                  pl.BlockSpec((1, D), lambda i, j: (0, 0)),
                  pl.BlockSpec(tab.shape, lambda i, j: (0, 0)),
                  pl.BlockSpec((D, tn), lambda i, j: (0, j)),
                  pl.BlockSpec(w_ab.shape, lambda i, j: (0, 0))],
        out_specs=[pl.BlockSpec((tm, tn), lambda i, j: (i, jnp.minimum(j, n_g - 1))),
                   pl.BlockSpec((tm, tn), lambda i, j: (i, jnp.maximum(j - n_g, 0))),
                   pl.BlockSpec((tm, w_ab.shape[1]), lambda i, j: (i, 0))],
        out_shape=[jax.ShapeDtypeStruct((M, n_cols_g), F32),
                   jax.ShapeDtypeStruct((M, N - n_cols_g), BF16),
                   jax.ShapeDtypeStruct((M, w_ab.shape[1]), F32)],
        scratch_shapes=[pltpu.VMEM((tm, D), BF16)],
        compiler_params=_cparams(("parallel", "arbitrary")), name="in_proj")(
            x, g.reshape(1, D), tab, w, w_ab)


def _norm_swiglu_kernel(x_ref, g_ref, tab_ref, w1_ref, w3_ref, o_ref, h_ref, **norm_args):
    @pl.when(pl.program_id(1) == 0)
    def _():
        _norm_mod_rows(x_ref, g_ref, tab_ref, h_ref, **norm_args)

    a = h_ref[...]
    h1 = jnp.dot(a, w1_ref[...], preferred_element_type=F32)
    h3 = jnp.dot(a, w3_ref[...], preferred_element_type=F32)
    o_ref[...] = (h1 * jax.nn.sigmoid(h1) * h3).astype(o_ref.dtype)


def _norm_swiglu(x, g, tab, w1, w3, *, nb, ncb, ctx_row, which):
    M, D = x.shape
    N = w1.shape[1]
    tm = _pick_tile(M, (1024, 512, 256))
    tn = _pick_tile(N, (512, 256, 128))
    kern = functools.partial(_norm_swiglu_kernel, tm=tm, nb=nb, ncb=ncb, ctx_row=ctx_row, which=which)
    return pl.pallas_call(
        kern, grid=(M // tm, N // tn),
        in_specs=[pl.BlockSpec((tm, D), lambda i, j: (i, 0)),
                  pl.BlockSpec((1, D), lambda i, j: (0, 0)),
                  pl.BlockSpec(tab.shape, lambda i, j: (0, 0)),
                  pl.BlockSpec((D, tn), lambda i, j: (0, j)),
                  pl.BlockSpec((D, tn), lambda i, j: (0, j))],
        out_specs=pl.BlockSpec((tm, tn), lambda i, j: (i, j)),
        out_shape=jax.ShapeDtypeStruct((M, N), BF16),
        scratch_shapes=[pltpu.VMEM((tm, D), BF16)],
        compiler_params=_cparams(("parallel", "arbitrary")), name="norm_swiglu")(
            x, g.reshape(1, D), tab, w1, w3)


def _mm_res_kernel(*refs, n_a, tm, tn, nb, ncb, ctx_row, which):
    a_refs = refs[:n_a]
    w_ref, res_ref, tab_ref, o_ref = refs[n_a:]
    i = pl.program_id(0)
    j = pl.program_id(1)
    a = a_refs[0][...] if n_a == 1 else jnp.concatenate([r[...] for r in a_refs], axis=1)
    acc = jnp.dot(a, w_ref[...], preferred_element_type=F32)
    for c in range(tm // ROW_BLOCK):
        idx = _table_row(i * (tm // ROW_BLOCK) + c, nb, ncb, ctx_row)
        gate = tab_ref[pl.ds(idx * 6 + which, 1), pl.ds(pl.multiple_of(j * tn, tn), tn)]
        rows = slice(c * ROW_BLOCK, (c + 1) * ROW_BLOCK)
        o_ref[rows, :] = res_ref[rows, :] + gate * acc[rows, :]


def _mm_res(a_list, w, res, tab, *, nb, ncb, ctx_row, which, tm, tn,
            in_row_map=None, out_rows=None):
    K, N = w.shape
    M = res.shape[0] if out_rows is None else out_rows
    rm = (lambda i: i) if in_row_map is None else in_row_map
    kern = functools.partial(_mm_res_kernel, n_a=len(a_list), tm=tm, tn=tn, nb=nb, ncb=ncb,
                             ctx_row=ctx_row, which=which)

    def a_spec(a):
        if a.shape[0] == res.shape[0]:
            return pl.BlockSpec((tm, a.shape[1]), lambda i, j: (rm(i), 0))
        assert a.shape[0] == M
        return pl.BlockSpec((tm, a.shape[1]), lambda i, j: (i, 0))

    in_specs = [a_spec(a) for a in a_list]
    in_specs += [pl.BlockSpec((K, tn), lambda i, j: (0, j)),
                 pl.BlockSpec((tm, tn), lambda i, j: (rm(i), j)),
                 pl.BlockSpec(tab.shape, lambda i, j: (0, 0))]
    return pl.pallas_call(
        kern, grid=(M // tm, N // tn), in_specs=in_specs,
        out_specs=pl.BlockSpec((tm, tn), lambda i, j: (i, j)),
        out_shape=jax.ShapeDtypeStruct((M, N), F32),
        compiler_params=_cparams(("parallel", "parallel")), name="mm_res")(*a_list, w, res, tab)


def _ones_column(rows):
    lane = lax.broadcasted_iota(jnp.int32, (rows, LANES), 1)
    return jnp.where(lane == 0, 1.0, 0.0).astype(BF16)


def _rope_lanes(x, cos, sin, rot):
    return x * cos + jnp.dot(x.astype(BF16), rot, preferred_element_type=F32) * sin


def _mean_sq(x):
    w = x.shape[1]
    return jnp.dot((x * x).astype(BF16), jnp.ones((w, w), BF16), preferred_element_type=F32) * (1.0 / w)


def _rope_tables(S, C, rot_dim):
    rows = S // GRID_W
    row = jnp.repeat(jnp.arange(rows), GRID_W).astype(F32)
    col = jnp.tile(jnp.arange(GRID_W), rows).astype(F32)
    half = rot_dim // 2
    inv_freq = ROPE_THETA ** (-jnp.arange(0, half, 2, dtype=F32) / half)
    ang_r = row[:, None] * inv_freq
    ang_c = col[:, None] * inv_freq
    ang = jnp.concatenate([ang_r, ang_r, ang_c, ang_c], axis=-1)
    cos, sin = jnp.cos(ang), jnp.sin(ang)
    quarter = rot_dim // 4
    pad = LANES - rot_dim

    def full(t, ctx_val):
        t = jnp.pad(t, ((0, 0), (0, pad)), constant_values=ctx_val)
        return jnp.concatenate([jnp.full((C, LANES), ctx_val, F32), t], axis=0)

    j = np.arange(rot_dim)
    even = (j // quarter) % 2 == 0
    rot = np.zeros((LANES, LANES), np.float32)
    rot[np.where(even, j + quarter, j - quarter), j] = np.where(even, -1.0, 1.0)
    return full(cos, 1.0), full(sin, 0.0), jnp.asarray(rot, BF16)


def _mla_prep_kernel(p_ref, qg_ref, kvg_ref, wq_ref, wkv_ref, cos_ref, sin_ref, rot_ref,
                     q_ref, k_ref, v_ref, *, scale):
    cos, sin, rot = cos_ref[...], sin_ref[...], rot_ref[...]
    cq = p_ref[:, 0:MLA_Q_RANK].astype(F32)
    qn = cq * lax.rsqrt(jnp.mean(cq * cq, axis=-1, keepdims=True) + NORM_EPS) * qg_ref[...]
    q_raw = jnp.dot(qn.astype(BF16), wq_ref[...], preferred_element_type=F32)
    ckv = p_ref[:, MLA_Q_RANK:MLA_Q_RANK + MLA_KV_RANK].astype(F32)
    kvn = ckv * lax.rsqrt(jnp.mean(ckv * ckv, axis=-1, keepdims=True) + NORM_EPS) * kvg_ref[...]
    kv = jnp.dot(kvn.astype(BF16), wkv_ref[...], preferred_element_type=F32)
    kr = p_ref[:, MLA_Q_RANK + MLA_KV_RANK:MLA_Q_RANK + MLA_KV_RANK + LANES].astype(F32)
    kr = _rope_lanes(kr, cos, sin, rot).astype(BF16)
    for h in range(MLA_HEADS):
        c0 = 2 * LANES * h
        q_ref[:, c0:c0 + LANES] = (q_raw[:, c0:c0 + LANES] * scale).astype(BF16)
        qr = _rope_lanes(q_raw[:, c0 + LANES:c0 + 2 * LANES], cos, sin, rot)
        q_ref[:, c0 + LANES:c0 + 2 * LANES] = (qr * scale).astype(BF16)
        k_ref[:, c0:c0 + LANES] = kv[:, LANES * h:LANES * (h + 1)].astype(BF16)
        k_ref[:, c0 + LANES:c0 + 2 * LANES] = kr
    ones_col = _ones_column(p_ref.shape[0])
    for h in range(MLA_HEADS):
        c0 = 2 * LANES * h
        v_ref[:, c0:c0 + LANES] = kv[:, (MLA_HEADS + h) * LANES:(MLA_HEADS + h + 1) * LANES].astype(BF16)
        v_ref[:, c0 + LANES:c0 + 2 * LANES] = ones_col


def _mla_prep(P, qg, kvg, wq_p, wkv_p, tabs, nbt):
    R = P.shape[0]
    width = MLA_HEADS * 2 * LANES
    kern = functools.partial(_mla_prep_kernel, scale=float((MLA_NOPE + MLA_ROPE) ** -0.5))
    tab_spec = pl.BlockSpec((ROW_BLOCK, LANES), lambda i: (i % nbt, 0))
    const = lambda a: pl.BlockSpec(a.shape, lambda i: (0, 0))
    return pl.pallas_call(
        kern, grid=(R // ROW_BLOCK,),
        in_specs=[pl.BlockSpec((ROW_BLOCK, 1024), lambda i: (i, P_MLA // 1024)),
                  const(qg), const(kvg), const(wq_p), const(wkv_p), tab_spec, tab_spec, const(tabs[2])],
        out_specs=[pl.BlockSpec((ROW_BLOCK, width), lambda i: (i, 0)),
                   pl.BlockSpec((ROW_BLOCK, width), lambda i: (i, 0)),
                   pl.BlockSpec((ROW_BLOCK, width), lambda i: (i, 0))],
        out_shape=[jax.ShapeDtypeStruct((R, width), BF16), jax.ShapeDtypeStruct((R, width), BF16),
                   jax.ShapeDtypeStruct((R, width), BF16)],
        compiler_params=_cparams(("parallel",)), name="mla_prep")(P, qg, kvg, wq_p, wkv_p, *tabs)


def _head_prep_kernel(p_ref, qg_ref, kg_ref, cos_ref, sin_ref, rot_ref, o_ref, *, norm, scale,
                      n_q, n_k):
    cos, sin, rot = cos_ref[...], sin_ref[...], rot_ref[...]
    for h in range(n_q + n_k):
        x = p_ref[:, LANES * h:LANES * (h + 1)].astype(F32)
        if norm:
            g = qg_ref[...] if h < n_q else kg_ref[...]
            x = x * lax.rsqrt(_mean_sq(x) + NORM_EPS) * g
        x = _rope_lanes(x, cos, sin, rot)
        if h < n_q:
            x = x * scale
        c0 = LANES * h if h < n_q else LANES * (h + 2 * n_k)
        o_ref[:, c0:c0 + LANES] = x.astype(BF16)
    v_in = LANES * (n_q + n_k)
    v_out = LANES * n_q
    ones_col = _ones_column(p_ref.shape[0])
    for h in range(n_k):
        o_ref[:, v_out + 2 * LANES * h:v_out + 2 * LANES * h + LANES] = (
            p_ref[:, v_in + LANES * h:v_in + LANES * (h + 1)])
        o_ref[:, v_out + 2 * LANES * h + LANES:v_out + 2 * LANES * (h + 1)] = ones_col


def _head_prep(P, col0, qg, kg, tabs, nbt, *, norm, n_q, n_k):
    R = P.shape[0]
    kern = functools.partial(_head_prep_kernel, norm=norm, scale=float(HEAD_DIM ** -0.5),
                             n_q=n_q, n_k=n_k)
    tab_spec = pl.BlockSpec((ROW_BLOCK, LANES), lambda i: (i % nbt, 0))
    const = lambda a: pl.BlockSpec(a.shape, lambda i: (0, 0))
    return pl.pallas_call(
        kern, grid=(R // ROW_BLOCK,),
        in_specs=[pl.BlockSpec((ROW_BLOCK, 1024), lambda i: (i, col0 // 1024)),
                  const(qg), const(kg), tab_spec, tab_spec, const(tabs[2])],
        out_specs=pl.BlockSpec((ROW_BLOCK, 1024 + n_k * LANES), lambda i: (i, 0)),
        out_shape=jax.ShapeDtypeStruct((R, 1024 + n_k * LANES), BF16),
        compiler_params=_cparams(("parallel",)), name="head_prep")(P, qg, kg, *tabs)


def _attn_kernel(q_ref, k_ref, v_ref, o_ref, *, HS, G, dq, dv, C, T, ctx_tile):
    def run(nk):
        nt = (((1,), (1,)), ((), ()))
        chains = [(hs, g) for hs in range(HS) for g in range(G)]
        s = []
        for hs, g in chains:
            q = q_ref[0, :, (hs * G + g) * dq:(hs * G + g + 1) * dq]
            k = k_ref[0, :nk, hs * dq:(hs + 1) * dq]
            s.append(lax.dot_general(q, k, nt, preferred_element_type=F32).astype(BF16))
        p = [jnp.exp(x - jnp.max(x, axis=-1, keepdims=True)) for x in s]
        for (hs, g), pc in zip(chains, p):
            o = jnp.dot(pc, v_ref[0, :nk, hs * 2 * dv:(hs + 1) * 2 * dv], preferred_element_type=F32)
            c0 = (hs * G + g) * dv
            o_ref[0, :, c0:c0 + dv] = (o[:, :dv] / o[:, dv:dv + 1]).astype(o_ref.dtype)

    if ctx_tile:
        qi = pl.program_id(2)
        pl.when(qi == 0)(lambda: run(C))
        pl.when(qi > 0)(lambda: run(T))
    else:
        run(T)


def _attention(q, k, v, *, Hk, G, dq, dv, q_col0, k_col0, v_col0, C, with_ctx):
    HS = ATTN_CHAINS_PER_STEP // G
    Bn, T, _ = q.shape
    ncb = C // ROW_BLOCK
    assert ncb == 1
    nq = T // ROW_BLOCK if with_ctx else (T - C) // ROW_BLOCK
    qoff = 0 if with_ctx else ncb
    assert Hk % HS == 0
    kern = functools.partial(_attn_kernel, HS=HS, G=G, dq=dq, dv=dv, C=C, T=T, ctx_tile=with_ctx)
    qw, kw, vw = HS * G * dq, HS * dq, HS * 2 * dv
    assert q_col0 % qw == 0 and k_col0 % kw == 0 and v_col0 % vw == 0
    return pl.pallas_call(
        kern, grid=(Bn, Hk // HS, nq),
        in_specs=[pl.BlockSpec((1, ROW_BLOCK, qw), lambda b, h, i: (b, i + qoff, q_col0 // qw + h)),
                  pl.BlockSpec((1, T, kw), lambda b, h, i: (b, 0, k_col0 // kw + h)),
                  pl.BlockSpec((1, T, vw), lambda b, h, i: (b, 0, v_col0 // vw + h))],
        out_specs=pl.BlockSpec((1, ROW_BLOCK, HS * G * dv), lambda b, h, i: (b, i, h)),
        out_shape=jax.ShapeDtypeStruct((Bn, nq * ROW_BLOCK, Hk * G * dv), BF16),
        compiler_params=_cparams(("parallel", "parallel", "parallel")), name="attention")(q, k, v)


def _swa_attention(qkv, sink, *, C, with_ctx):
    Bn, T, _ = qkv.shape
    Hq, Hk, d = SWA_HEADS, SWA_KV_HEADS, HEAD_DIM
    ncb = C // ROW_BLOCK
    assert ncb == 1 and T - C >= ROW_BLOCK + 2 * SWA_WINDOW
    nq = T // ROW_BLOCK if with_ctx else (T - C) // ROW_BLOCK
    qoff = 0 if with_ctx else ncb
    kern = functools.partial(_swa_kernel, C=C, S=T - C, ctx_tile=with_ctx, qoff=qoff)
    k_blk = (Hq * d + Hk * 2 * d) // (Hk * d)
    return pl.pallas_call(
        kern, grid=(Bn, nq),
        in_specs=[pl.BlockSpec(memory_space=pltpu.SMEM),
                  pl.BlockSpec((1, ROW_BLOCK, Hq * d), lambda b, i: (b, i + qoff, 0)),
                  pl.BlockSpec((1, T, Hk * d), lambda b, i: (b, 0, k_blk)),
                  pl.BlockSpec((1, T, Hk * 2 * d), lambda b, i: (b, 0, 1))],
        out_specs=pl.BlockSpec((1, ROW_BLOCK, Hq * d), lambda b, i: (b, i, 0)),
        out_shape=jax.ShapeDtypeStruct((Bn, nq * ROW_BLOCK, Hq * d), BF16),
        compiler_params=_cparams(("parallel", "parallel")), name="swa")(sink, qkv, qkv, qkv)


def _swa_kernel(sink_ref, q_ref, k_ref, v_ref, o_ref, *, C, S, ctx_tile, qoff):
    qi = pl.program_id(1) + qoff
    d = HEAD_DIM
    G = SWA_HEADS // SWA_KV_HEADS
    W = SWA_WINDOW
    WIN = ROW_BLOCK + 2 * W
    nt = (((1,), (1,)), ((), ()))
    heads = range(SWA_HEADS)

    def q_of(h):
        return q_ref[0, :, h * d:(h + 1) * d]

    def finish(parts):
        m16 = []
        for h in heads:
            m = jnp.full((ROW_BLOCK, 1), sink_ref[h], F32)
            for s, _ in parts[h]:
                m = jnp.maximum(m, jnp.max(s, axis=-1, keepdims=True).astype(F32))
            m16.append(m.astype(BF16))
        for h in heads:
            o = None
            for s, v in parts[h]:
                pv = jnp.dot(jnp.exp(s - m16[h]), v, preferred_element_type=F32)
                o = pv if o is None else o + pv
            den = jnp.exp(sink_ref[h] - m16[h].astype(F32)) + o[:, d:d + 1]
            o_ref[0, :, h * d:(h + 1) * d] = (o[:, :d] / den).astype(o_ref.dtype)

    def ctx_scores(h):
        kc = k_ref[0, :C, (h // G) * d:(h // G + 1) * d]
        return (lax.dot_general(q_of(h), kc, nt, preferred_element_type=F32).astype(BF16),
                v_ref[0, :C, (h // G) * 2 * d:(h // G + 1) * 2 * d])

    def run_ctx():
        finish([[ctx_scores(h)] for h in heads])

    def run_lat():
        q0 = (qi - (C // ROW_BLOCK)) * ROW_BLOCK
        ws = jnp.clip(q0 - W, 0, S - WIN)
        start = pl.multiple_of(C + ws, LANES)
        qpos = q0 + lax.broadcasted_iota(jnp.int32, (ROW_BLOCK, WIN), 0)
        kpos = ws + lax.broadcasted_iota(jnp.int32, (ROW_BLOCK, WIN), 1)
        valid = jnp.abs(kpos - qpos) <= W
        parts = []
        for h in heads:
            kw = k_ref[0, pl.ds(start, WIN), (h // G) * d:(h // G + 1) * d]
            vw = v_ref[0, pl.ds(start, WIN), (h // G) * 2 * d:(h // G + 1) * 2 * d]
            s_loc = jnp.where(valid, lax.dot_general(q_of(h), kw, nt, preferred_element_type=F32),
                              NEG_INF).astype(BF16)
            parts.append([(s_loc, vw), ctx_scores(h)])
        finish(parts)

    if ctx_tile:
        pl.when(qi == 0)(run_ctx)
        pl.when(qi > 0)(run_lat)
    else:
        run_lat()


def _gdn_prep_kernel(x_ref, w_ref, o_ref, pad_ref, *, C, T):
    j = pl.program_id(1)
    g = SUBLANES
    half = (GDN_CONV - 1) // 2
    assert half <= g
    zeros = jnp.zeros((g, LANES), F32)
    segments = ((0, C, g), (C, T - C, 2 * g + C))
    pad_ref[0:g, :] = zeros
    for lo, n, base in segments:
        pad_ref[base:base + n, :] = x_ref[0, lo:lo + n, :]
        pad_ref[base + n:base + n + g, :] = zeros
    nh = GDN_HEADS
    for lo, n, base in segments:
        acc = None
        for tap in range(GDN_CONV):
            term = pad_ref[base + tap - half:base + tap - half + n, :] * w_ref[tap:tap + 1, :]
            acc = term if acc is None else acc + term
        y = acc * jax.nn.sigmoid(acc)
        inv = lax.rsqrt(_mean_sq(y) * float(LANES) + 1e-6)
        f = jnp.where(j < 2 * nh, inv, 1.0) * jnp.where(j < nh, float(HEAD_DIM ** -0.5), 1.0)
        o_ref[0, lo:lo + n, :] = y * f


def _gdn_prep(P3, conv_w8, C):
    Bn, T, _ = P3.shape
    nblk = 3 * GDN_HEADS
    kern = functools.partial(_gdn_prep_kernel, C=C, T=T)
    return pl.pallas_call(
        kern, grid=(Bn, nblk),
        in_specs=[pl.BlockSpec((1, T, LANES), lambda b, j: (b, 0, P_GDN // LANES + j)),
                  pl.BlockSpec((8, LANES), lambda b, j: (0, j))],
        out_specs=pl.BlockSpec((1, T, LANES), lambda b, j: (b, 0, j)),
        out_shape=jax.ShapeDtypeStruct((Bn, T, nblk * LANES), F32),
        scratch_shapes=[pltpu.VMEM((T + 3 * SUBLANES, LANES), F32)],
        compiler_params=_cparams(("parallel", "parallel")), name="gdn_prep")(P3, conv_w8)


_GDN_INV_BASE = 8
GDN_BATCH_PER_STEP = 4


def _dot16(a, b):
    return jnp.dot(a, b, preferred_element_type=F32)


def _dot16_nt(a, b):
    return lax.dot_general(a, b, (((1,), (1,)), ((), ())), preferred_element_type=F32)


def _dot16_tn(a, b):
    return lax.dot_general(a, b, (((0,), (0,)), ((), ())), preferred_element_type=F32)


def _mask_dot(mask16, x):
    hi = x.astype(BF16)
    r1 = x - hi.astype(F32)
    mid = r1.astype(BF16)
    lo = (r1 - mid.astype(F32)).astype(BF16)
    w = x.shape[1]
    y = jnp.dot(mask16, jnp.concatenate([hi, mid, lo], axis=1), preferred_element_type=F32)
    return y[:, :w] + y[:, w:2 * w] + y[:, 2 * w:]


def _gdn_chunk_kernel(qf_ref, qb_ref, abf_ref, abb_ref, par_ref, of_ref, ob_ref, s_ref):
    Cc = GDN_CHUNK
    H = GDN_HEADS
    d = HEAD_DIM

    @pl.when(pl.program_id(1) == 0)
    def _():
        s_ref[...] = jnp.zeros_like(s_ref)

    ri = lax.broadcasted_iota(jnp.int32, (Cc, Cc), 0)
    ci = lax.broadcasted_iota(jnp.int32, (Cc, Cc), 1)
    eye = (ri == ci).astype(F32)
    neg_a = par_ref[0:1, :]
    dt_b = par_ref[1:2, :]
    base = _GDN_INV_BASE
    diag_blk = (ri // base) == (ci // base)
    off_blks = []
    s = base
    while s < Cc:
        off_blks.append(jnp.logical_and((ri // (2 * s)) == (ci // (2 * s)), (ri // s) != (ci // s)))
        s *= 2

    dirs = []
    for bb, direction in [(bb, direction) for bb in range(qf_ref.shape[0]) for direction in range(2)]:
        ab = (abf_ref if direction == 0 else abb_ref)[bb]
        if direction == 0:
            incl, incl_t, strict, last = ri >= ci, ri <= ci, ri > ci, Cc - 1
        else:
            incl, incl_t, strict, last = ri <= ci, ri >= ci, ri < ci, 0
        z = ab + dt_b
        sp = jnp.maximum(z, 0.0) + jnp.log1p(jnp.exp(-jnp.abs(z)))
        gl_all = neg_a * sp
        gc_all = _mask_dot(incl.astype(BF16), gl_all)
        g_tot = gc_all[last:last + 1, :]
        gc_sq = jnp.concatenate([gc_all, jnp.zeros((LANES - Cc, LANES), F32)], axis=0)
        dirs.append(dict(
            bb=bb, direction=direction,
            x_ref=qf_ref if direction == 0 else qb_ref, o_ref=of_ref if direction == 0 else ob_ref,
            incl=incl, strict=strict, gc_all=gc_all,
            gct_all=gc_sq.T[:, :Cc],
            beta_all=jax.nn.sigmoid(ab),
            eg_all=jnp.exp(gc_all), ekg_all=jnp.exp(g_tot - gc_all), egt_all=jnp.exp(g_tot)))

    units = [(dd, dd["direction"] * H + h, h) for dd in dirs for h in range(H)]
    U = range(len(units))

    k16, kb, decay, a_mat = [], [], [], []
    for dd, lane, h in units:
        k = dd["x_ref"][dd["bb"],:, (H + h) * d:(H + h + 1) * d]
        beta = dd["beta_all"][:, 2 * H + lane:2 * H + lane + 1]
        diff = dd["gc_all"][:, lane:lane + 1] - dd["gct_all"][lane:lane + 1, :]
        decay.append(jnp.where(dd["incl"], jnp.exp(jnp.where(dd["incl"], diff, 0.0)), 0.0))
        kb.append(k * beta)
        k16.append(k.astype(BF16))
    for i in U:
        a_mat.append(jnp.where(units[i][0]["strict"],
                               _dot16_nt(kb[i].astype(BF16), k16[i]) * decay[i], 0.0))

    npow = [jnp.where(diag_blk, a, 0.0).astype(BF16) for a in a_mat]
    t = [eye - jnp.where(diag_blk, a, 0.0) for a in a_mat]
    sq = 2 * base
    while sq > 4:
        npow = [_dot16(n, n).astype(BF16) for n in npow]
        t = [t[i] + _dot16(t[i].astype(BF16), npow[i]) for i in U]
        sq //= 2
    for off_blk in off_blks:
        t16 = [x.astype(BF16) for x in t]
        xo = [_dot16(t16[i], jnp.where(off_blk, a_mat[i], 0.0).astype(BF16)).astype(BF16) for i in U]
        t = [t[i] - _dot16(xo[i], t16[i]) for i in U]

    uw, intra, qg = [], [], []
    for i in U:
        dd, lane, h = units[i]
        q = dd["x_ref"][dd["bb"],:, h * d:(h + 1) * d]
        v = dd["x_ref"][dd["bb"],:, (2 * H + h) * d:(2 * H + h + 1) * d]
        beta = dd["beta_all"][:, 2 * H + lane:2 * H + lane + 1]
        eg = dd["eg_all"][:, lane:lane + 1]
        rhs = jnp.concatenate([v * beta, kb[i] * eg], axis=1).astype(BF16)
        uw.append(_dot16(t[i].astype(BF16), rhs))
        intra.append(jnp.where(dd["incl"], _dot16_nt(q.astype(BF16), k16[i]) * decay[i], 0.0)
                     .astype(BF16))
        qg.append((q * eg).astype(BF16))
    state = [s_ref[units[i][0]["bb"] * 2 * H + units[i][1]] for i in U]
    s16 = [x.astype(BF16) for x in state]
    v_new = [uw[i][:, :d] - _dot16(uw[i][:, d:].astype(BF16), s16[i]) for i in U]
    vn16 = [x.astype(BF16) for x in v_new]
    for i in U:
        dd, lane, h = units[i]
        dd["o_ref"][dd["bb"], :, h * d:(h + 1) * d] = (_dot16(qg[i], s16[i])
                                                       + _dot16(intra[i], vn16[i]))
    for i in U:
        dd, lane, h = units[i]
        k = dd["x_ref"][dd["bb"],:, (H + h) * d:(H + h + 1) * d]
        kg = (k * dd["ekg_all"][:, lane:lane + 1]).astype(BF16)
        s_ref[dd["bb"] * 2 * H + lane] = (state[i] * dd["egt_all"][:, lane:lane + 1]
                                          + _dot16_tn(kg, vn16[i]))


def _gdn_chunks(qkv, P3, par, C):
    Bn, T, _ = qkv.shape
    Cc = GDN_CHUNK
    nc = T // Cc
    ncc = C // Cc
    H = GDN_HEADS

    def bwd_chunk(s):
        return jnp.where(s < ncc, ncc - 1 - s, nc + ncc - 1 - s)

    ab_blk = 0
    nbb = _pick_tile(Bn, (GDN_BATCH_PER_STEP, 1))
    return pl.pallas_call(
        _gdn_chunk_kernel, grid=(Bn // nbb, nc),
        in_specs=[pl.BlockSpec((nbb, Cc, 3 * H * HEAD_DIM), lambda b, s: (b, s, 0)),
                  pl.BlockSpec((nbb, Cc, 3 * H * HEAD_DIM), lambda b, s: (b, bwd_chunk(s), 0)),
                  pl.BlockSpec((nbb, Cc, LANES), lambda b, s: (b, s, ab_blk)),
                  pl.BlockSpec((nbb, Cc, LANES), lambda b, s: (b, bwd_chunk(s), ab_blk)),
                  pl.BlockSpec((8, LANES), lambda b, s: (0, 0))],
        out_specs=[pl.BlockSpec((nbb, Cc, H * HEAD_DIM), lambda b, s: (b, s, 0)),
                   pl.BlockSpec((nbb, Cc, H * HEAD_DIM), lambda b, s: (b, bwd_chunk(s), 0))],
        out_shape=[jax.ShapeDtypeStruct((Bn, T, H * HEAD_DIM), F32),
                   jax.ShapeDtypeStruct((Bn, T, H * HEAD_DIM), F32)],
        scratch_shapes=[pltpu.VMEM((nbb * 2 * H, HEAD_DIM, HEAD_DIM), F32)],
        compiler_params=_cparams(("parallel", "arbitrary")), name="gdn_chunks")(
            qkv, qkv, P3, P3, par)


def _gdn_post_kernel(of_ref, ob_ref, z_ref, g_ref, o_ref):
    d = HEAD_DIM
    for h in range(GDN_HEADS):
        o = of_ref[:, h * d:(h + 1) * d] + ob_ref[:, h * d:(h + 1) * d]
        y = o * lax.rsqrt(_mean_sq(o) + NORM_EPS) * g_ref[...]
        z = z_ref[:, h * d:(h + 1) * d]
        o_ref[:, h * d:(h + 1) * d] = (y * (z * jax.nn.sigmoid(z))).astype(o_ref.dtype)


def _gdn_post(of, ob, P, g):
    R, W = of.shape
    zblk = (P_GDN + 3 * W) // W
    return pl.pallas_call(
        _gdn_post_kernel, grid=(R // ROW_BLOCK,),
        in_specs=[pl.BlockSpec((ROW_BLOCK, W), lambda i: (i, 0)),
                  pl.BlockSpec((ROW_BLOCK, W), lambda i: (i, 0)),
                  pl.BlockSpec((ROW_BLOCK, W), lambda i: (i, zblk)),
                  pl.BlockSpec((1, HEAD_DIM), lambda i: (0, 0))],
        out_specs=pl.BlockSpec((ROW_BLOCK, W), lambda i: (i, 0)),
        out_shape=jax.ShapeDtypeStruct((R, W), BF16),
        compiler_params=_cparams(("parallel",)), name="gdn_post")(of, ob, P, g)


def _gather_rows_kernel(tok_ref, tok_next_ref, h_hbm, o_ref, buf, sem):
    i = pl.program_id(0)
    nt = pl.num_programs(0)
    n = buf.shape[1]
    slot = i % 2

    def issue(t_ref, s):
        def body(r, c):
            pltpu.make_async_copy(h_hbm.at[pl.ds(t_ref[0, 0, r], 1), :],
                                  buf.at[s, pl.ds(r, 1), :], sem.at[s]).start()
            return c
        lax.fori_loop(0, n, body, 0, unroll=8)

    @pl.when(i == 0)
    def _():
        issue(tok_ref, 0)

    @pl.when(i + 1 < nt)
    def _():
        issue(tok_next_ref, 1 - slot)

    pltpu.make_async_copy(h_hbm.at[pl.ds(0, n), :], buf.at[slot], sem.at[slot]).wait()
    o_ref[...] = buf[slot].astype(o_ref.dtype)


def _gather_rows(h, src_tok):
    D = h.shape[1]
    n = src_tok.shape[0]
    nt = n // ROW_BLOCK
    return pl.pallas_call(
        _gather_rows_kernel, grid=(nt,),
        in_specs=[pl.BlockSpec((1, 1, ROW_BLOCK), lambda i: (i, 0, 0), memory_space=pltpu.SMEM),
                  pl.BlockSpec((1, 1, ROW_BLOCK), lambda i: (jnp.minimum(i + 1, nt - 1), 0, 0),
                               memory_space=pltpu.SMEM),
                  pl.BlockSpec(memory_space=pl.ANY)],
        out_specs=pl.BlockSpec((ROW_BLOCK, D), lambda i: (i, 0)),
        out_shape=jax.ShapeDtypeStruct((n, D), BF16),
        scratch_shapes=[pltpu.VMEM((2, ROW_BLOCK, D), F32), pltpu.SemaphoreType.DMA((2,))],
        compiler_params=_cparams(("arbitrary",)), name="moe_gather")(
            src_tok.reshape(nt, 1, ROW_BLOCK), src_tok.reshape(nt, 1, ROW_BLOCK), h)


def _gmm_kernel(te_ref, tv_ref, x_ref, w1_ref, w3_ref, w2_ref, o_ref):
    i = pl.program_id(0)
    f = pl.program_id(1)

    @pl.when(f == 0)
    def _():
        o_ref[...] = jnp.zeros_like(o_ref)

    @pl.when(tv_ref[i] == 1)
    def _():
        x = x_ref[...]
        h1 = jnp.dot(x, w1_ref[0], preferred_element_type=F32)
        h3 = jnp.dot(x, w3_ref[0], preferred_element_type=F32)
        hid = (h1 * jax.nn.sigmoid(h1) * h3).astype(BF16)
        o_ref[...] += jnp.dot(hid, w2_ref[0], preferred_element_type=F32)


def _gmm(xs, w1, w3, w2, tile_e, tile_valid):
    n, D = xs.shape
    E, _, F = w1.shape
    tf = 1024
    nt = n // MOE_TILE
    grid_spec = pltpu.PrefetchScalarGridSpec(
        num_scalar_prefetch=2, grid=(nt, F // tf),
        in_specs=[pl.BlockSpec((MOE_TILE, D), lambda i, f, te, tv: (i, 0)),
                  pl.BlockSpec((1, D, tf), lambda i, f, te, tv: (te[i], 0, f)),
                  pl.BlockSpec((1, D, tf), lambda i, f, te, tv: (te[i], 0, f)),
                  pl.BlockSpec((1, tf, D), lambda i, f, te, tv: (te[i], f, 0))],
        out_specs=pl.BlockSpec((MOE_TILE, D), lambda i, f, te, tv: (i, 0)))
    return pl.pallas_call(
        _gmm_kernel, grid_spec=grid_spec,
        out_shape=jax.ShapeDtypeStruct((n, D), F32),
        compiler_params=_cparams(("parallel", "arbitrary")), name="moe_gmm")(
            tile_e, tile_valid, xs, w1, w3, w2)


def _combine_kernel(pos_ref, pos_next_ref, ys_hbm, x_ref, eg_ref, tab_ref, fg_ref, o_ref, buf, sem,
                    *, nb, ncb, ctx_row, which, final_norm):
    i = pl.program_id(0)
    nt = pl.num_programs(0)
    n = buf.shape[2]
    slot = i % 2
    idx = _table_row(i, nb, ncb, ctx_row)

    def issue(p_ref, s):
        def body(r, c):
            for k in range(TOP_K):
                pltpu.make_async_copy(ys_hbm.at[pl.ds(p_ref[0, k, r], 1), :],
                                      buf.at[s, k, pl.ds(r, 1), :], sem.at[s]).start()
            return c
        lax.fori_loop(0, n, body, 0, unroll=4)

    @pl.when(i == 0)
    def _():
        issue(pos_ref, 0)

    @pl.when(i + 1 < nt)
    def _():
        issue(pos_next_ref, 1 - slot)

    for k in range(TOP_K):
        pltpu.make_async_copy(ys_hbm.at[pl.ds(0, n), :], buf.at[slot, k], sem.at[slot]).wait()
    gate = tab_ref[pl.ds(idx * 6 + which, 1), :]
    eg = eg_ref[...]
    y = eg[:, 0:1] * buf[slot, 0]
    for k in range(1, TOP_K):
        y = y + eg[:, k:k + 1] * buf[slot, k]
    out = x_ref[...] + gate * y
    if final_norm:
        out = out * lax.rsqrt(jnp.mean(out * out, axis=-1, keepdims=True) + NORM_EPS) * fg_ref[...]
    o_ref[...] = out


def _combine(ys, pos, egate, x, tab, final_g, *, nb, ncb, ctx_row, which, final_norm):
    R, D = x.shape
    nt = R // ROW_BLOCK
    pos3 = pos.reshape(nt, ROW_BLOCK, TOP_K).transpose(0, 2, 1)
    kern = functools.partial(_combine_kernel, nb=nb, ncb=ncb, ctx_row=ctx_row, which=which,
                             final_norm=final_norm)
    return pl.pallas_call(
        kern, grid=(nt,),
        in_specs=[pl.BlockSpec((1, TOP_K, ROW_BLOCK), lambda i: (i, 0, 0), memory_space=pltpu.SMEM),
                  pl.BlockSpec((1, TOP_K, ROW_BLOCK), lambda i: (jnp.minimum(i + 1, nt - 1), 0, 0),
                               memory_space=pltpu.SMEM),
                  pl.BlockSpec(memory_space=pl.ANY),
                  pl.BlockSpec((ROW_BLOCK, D), lambda i: (i, 0)),
                  pl.BlockSpec((ROW_BLOCK, LANES), lambda i: (i, 0)),
                  pl.BlockSpec(tab.shape, lambda i: (0, 0)),
                  pl.BlockSpec((1, D), lambda i: (0, 0))],
        out_specs=pl.BlockSpec((ROW_BLOCK, D), lambda i: (i, 0)),
        out_shape=jax.ShapeDtypeStruct((R, D), F32),
        scratch_shapes=[pltpu.VMEM((2, TOP_K, ROW_BLOCK, D), F32), pltpu.SemaphoreType.DMA((2,))],
        compiler_params=_cparams(("arbitrary",)), name="moe_combine")(pos3, pos3, ys, x, egate, tab, final_g.reshape(1, D))


def _route(logits):
    R = logits.shape[0]
    E = N_EXPERTS
    top_v, top_i = lax.top_k(logits, TOP_K)
    gate = jax.nn.softmax(top_v, axis=-1)
    onehot = jnp.sum((top_i[:, :, None] == jnp.arange(E)[None, None, :]).astype(jnp.int32), axis=1)
    before = jnp.cumsum(onehot, axis=0) - onehot
    sizes = jnp.sum(onehot, axis=0)
    padded = ((sizes + MOE_TILE - 1) // MOE_TILE) * MOE_TILE
    ends = jnp.cumsum(padded)
    off = ends - padded
    pos = off[top_i] + jnp.take_along_axis(before, top_i, axis=1)
    nt = (TOP_K * R) // MOE_TILE + E
    n = nt * MOE_TILE
    flat = pos.reshape(-1)
    src_tok = jnp.zeros((n,), jnp.int32).at[flat].set(jnp.repeat(jnp.arange(R, dtype=jnp.int32), TOP_K))
    egate = jnp.pad(gate, ((0, 0), (0, LANES - TOP_K)))
    tile_start = jnp.arange(nt, dtype=jnp.int32) * MOE_TILE
    tile_valid = (tile_start < ends[-1]).astype(jnp.int32)
    tile_e = jnp.searchsorted(ends, tile_start, side="right").astype(jnp.int32)
    last_e = jnp.max(jnp.where(tile_valid == 1, tile_e, 0))
    tile_e = jnp.where(tile_valid == 1, tile_e, last_e)
    return pos.astype(jnp.int32), src_tok, egate, tile_e, tile_valid


def _prep_w_in(w):
    D = w.shape[0]
    z = lambda n: jnp.zeros((D, n), w.dtype)
    w_g = w[:, 1728:3776]
    w_a = jnp.concatenate([w[:, 704:1728], w[:, 3792:4816], w[:, 0:704], z(PA_WIDTH - 2048 - 704)],
                          axis=1)
    w_ab = jnp.concatenate([w[:, 3776:3792], z(LANES - 16)], axis=1)
    assert w_g.shape[1] == PG_WIDTH and w_a.shape[1] == PA_WIDTH
    return jnp.concatenate([w_g, w_a], axis=1).astype(BF16), w_ab.astype(BF16)


def _prep_w_uq(w):
    r = w.shape[0]
    w = w.reshape(r, MLA_HEADS, MLA_NOPE + MLA_ROPE)
    w = jnp.pad(w, ((0, 0), (0, 0), (0, 2 * LANES - (MLA_NOPE + MLA_ROPE))))
    return w.reshape(r, MLA_HEADS * 2 * LANES).astype(BF16)


def _prep_w_ukv(w):
    r = w.shape[0]
    w = w.reshape(r, MLA_HEADS, MLA_NOPE + MLA_V)
    return jnp.concatenate([w[:, :, :MLA_NOPE].reshape(r, -1), w[:, :, MLA_NOPE:].reshape(r, -1)],
                           axis=1).astype(BF16)


def kernel(x, c, ctx, c_ctx, norm1_g, norm2_g, w_mod, b_mod, w_in, mla_q_norm_g, mla_kv_norm_g,
           mla_w_uq, mla_w_ukv, gqa_q_norm_g, gqa_k_norm_g, gdn_conv_w, gdn_a_log, gdn_dt_bias,
           gdn_norm_g, swa_sink, w_out, ffn_w1, ffn_w3, ffn_w2, moe_router, moe_w1, moe_w3, moe_w2,
           final_norm_g):
    Bn, S, D = x.shape
    C = ctx.shape[1]
    T = C + S
    R = Bn * T
    depth = w_in.shape[0]
    nbt = T // ROW_BLOCK
    ncb = C // ROW_BLOCK
    nbl = S // ROW_BLOCK

    m_rows = ((Bn + 1 + 7) // 8) * 8
    cmat = jnp.zeros((m_rows, D), F32).at[:Bn].set(c).at[Bn].set(c_ctx)
    mods = _mods(cmat, w_mod, b_mod).reshape(depth, m_rows * 6, D)

    rope_mla = _rope_tables(S, C, MLA_ROPE)
    rope_head = _rope_tables(S, C, HEAD_DIM)

    xc = jnp.concatenate([ctx, x], axis=1).reshape(R, D)
    lay = dict(nb=nbt, ncb=ncb, ctx_row=Bn)

    for layer in range(depth):
        with_ctx = layer < depth - 1
        tab = mods[layer]
        Pg, Pa, Pab = _in_proj(xc, norm1_g[layer], tab, *_prep_w_in(w_in[layer]), n_cols_g=PG_WIDTH,
                          which=0, **lay)
        Pg3 = Pg.reshape(Bn, T, PG_WIDTH)

        qa, ka, va = _mla_prep(Pa, mla_q_norm_g[layer].reshape(1, -1), mla_kv_norm_g[layer].reshape(1, -1),
                               _prep_w_uq(mla_w_uq[layer]), _prep_w_ukv(mla_w_ukv[layer]), rope_mla, nbt)
        ya = _attention(qa.reshape(Bn, T, -1), ka.reshape(Bn, T, -1), va.reshape(Bn, T, -1),
                        Hk=MLA_HEADS, G=1, dq=2 * LANES, dv=MLA_V, q_col0=0, k_col0=0, v_col0=0,
                        C=C, with_ctx=with_ctx)
        qkv_b = _head_prep(Pa, P_GQA, gqa_q_norm_g[layer].reshape(1, -1), gqa_k_norm_g[layer].reshape(1, -1),
                           rope_head, nbt, norm=True, n_q=GQA_HEADS, n_k=GQA_KV_HEADS).reshape(Bn, T, -1)
        Gb = GQA_HEADS // GQA_KV_HEADS
        yb = _attention(qkv_b, qkv_b, qkv_b, Hk=GQA_KV_HEADS, G=Gb, dq=HEAD_DIM, dv=HEAD_DIM,
                        q_col0=0, v_col0=GQA_HEADS * HEAD_DIM,
                        k_col0=(GQA_HEADS + 2 * GQA_KV_HEADS) * HEAD_DIM, C=C, with_ctx=with_ctx)
        conv_w8 = jnp.pad(gdn_conv_w[layer], ((0, 8 - GDN_CONV), (0, 0)))
        qkv_c = _gdn_prep(Pg3, conv_w8, C)
        par = jnp.zeros((8, LANES), F32)
        par = par.at[0, :2 * GDN_HEADS].set(-jnp.exp(gdn_a_log[layer].reshape(-1)))
        par = par.at[1, :2 * GDN_HEADS].set(gdn_dt_bias[layer].reshape(-1))
        of, ob = _gdn_chunks(qkv_c, Pab.reshape(Bn, T, LANES), par, C)
        yc = _gdn_post(of.reshape(R, -1), ob.reshape(R, -1), Pg, gdn_norm_g[layer].reshape(1, -1))
        qkv_d = _head_prep(Pa, P_SWA, gqa_q_norm_g[layer].reshape(1, -1), gqa_k_norm_g[layer].reshape(1, -1),
                           rope_head, nbt, norm=False, n_q=SWA_HEADS, n_k=SWA_KV_HEADS).reshape(Bn, T, -1)
        yd = _swa_attention(qkv_d, swa_sink[layer], C=C, with_ctx=with_ctx)

        mixed = [ya.reshape(-1, GROUP_WIDTH), yb.reshape(-1, GROUP_WIDTH), yc,
                 yd.reshape(-1, GROUP_WIDTH)]
        w_o = w_out[layer].astype(BF16)
        if with_ctx:
            xc = _mm_res(mixed, w_o, xc, tab, which=2, tm=_pick_tile(R, (1024, 512, 256)),
                         tn=_pick_tile(D, (1024, 512)), **lay)
        else:
            lay = dict(nb=nbl, ncb=0, ctx_row=Bn)
            xc = _mm_res(mixed, w_o, xc, tab, which=2, tm=ROW_BLOCK, tn=D, **lay,
                         in_row_map=lambda i: (i // nbl) * nbt + ncb + i % nbl, out_rows=Bn * S)

        i2 = layer // 2
        if layer % 2 == 0:
            hid = _norm_swiglu(xc, norm2_g[layer], tab, ffn_w1[i2].astype(BF16),
                               ffn_w3[i2].astype(BF16), which=3, **lay)
            xc = _mm_res([hid], ffn_w2[i2].astype(BF16), xc, tab, which=5, tm=_pick_tile(xc.shape[0], (1024, 512)), tn=512, **lay)
        else:
            router_p = jnp.pad(moe_router[i2], ((0, 0), (0, LANES - N_EXPERTS)))
            h2, logits = _norm_router(xc, norm2_g[layer], tab, router_p, which=3, **lay)
            pos, src_tok, egate, tile_e, tile_valid = _route(logits[:, :N_EXPERTS])
            xs = _gather_rows(h2, src_tok)
            ys = _gmm(xs, moe_w1[i2].astype(BF16), moe_w3[i2].astype(BF16), moe_w2[i2].astype(BF16),
                      tile_e, tile_valid)
            xc = _combine(ys, pos, egate, xc, tab, final_norm_g, which=5,
                          final_norm=layer == depth - 1, **lay)

    if depth % 2 == 1:
        xc = _final_norm(xc, final_norm_g)
    return xc.reshape(Bn, S, D)
```

```python
import functools

import numpy as np
import jax
import jax.numpy as jnp
from jax import lax
from jax.experimental import pallas as pl
from jax.experimental.pallas import tpu as pltpu

F32 = jnp.float32
BF16 = jnp.bfloat16
HIGHEST = lax.Precision.HIGHEST

GRID_W = 64
HEAD_DIM = 128
ROPE_THETA = 10000.0
NORM_EPS = 1e-6
NEG_INF = -1e30
MLA_HEADS = 4
MLA_NOPE = 128
MLA_ROPE = 64
MLA_V = 128
MLA_Q_RANK = 384
MLA_KV_RANK = 256
GQA_HEADS = 4
GQA_KV_HEADS = 2
GDN_HEADS = 4
GDN_CONV = 5
GDN_CHUNK = 64
SWA_HEADS = 4
SWA_KV_HEADS = 2
SWA_WINDOW = 128
N_EXPERTS = 8
TOP_K = 2
GROUP_WIDTH = 512

ROW_BLOCK = 256
LANES = 128
SUBLANES = 8
MOE_TILE = 1024
ATTN_CHAINS_PER_STEP = 4
VMEM_LIMIT = 56 * 1024 * 1024

P_GDN = 0
PG_WIDTH = 2048
P_GQA = 0
P_SWA = 1024
P_MLA = 2048
PA_WIDTH = 3072


def _cparams(sem, vmem=VMEM_LIMIT):
    return pltpu.CompilerParams(dimension_semantics=sem, vmem_limit_bytes=vmem)


def _table_row(gi, nb, ncb, ctx_row):
    b = gi // nb
    r = gi - b * nb
    return jnp.where(r < ncb, ctx_row, b)


def _mods_kernel(c_ref, w_ref, b_ref, o_ref):
    c = c_ref[...]
    a = (c * jax.nn.sigmoid(c)).astype(BF16)
    o_ref[0] = jnp.dot(a, w_ref[0].astype(BF16), preferred_element_type=F32) + b_ref[0]


def _mods(cmat, w_mod, b_mod):
    L, D, N = w_mod.shape
    M = cmat.shape[0]
    tn = 1024
    return pl.pallas_call(
        _mods_kernel,
        grid=(L, N // tn),
        in_specs=[pl.BlockSpec((M, D), lambda l, j: (0, 0)),
                  pl.BlockSpec((1, D, tn), lambda l, j: (l, 0, j)),
                  pl.BlockSpec((1, 1, tn), lambda l, j: (l, 0, j))],
        out_specs=pl.BlockSpec((1, M, tn), lambda l, j: (l, 0, j)),
        out_shape=jax.ShapeDtypeStruct((L, M, N), F32),
        compiler_params=_cparams(("parallel", "parallel")),
        name="mods",
    )(cmat, w_mod, b_mod.reshape(L, 1, N))


def _norm_router_kernel(x_ref, g_ref, tab_ref, r_ref, h_ref, l_ref, *, nb, ncb, ctx_row, which):
    idx = _table_row(pl.program_id(0), nb, ncb, ctx_row)
    x = x_ref[...]
    y = x * lax.rsqrt(jnp.mean(x * x, axis=-1, keepdims=True) + NORM_EPS) * g_ref[...]
    shift = tab_ref[pl.ds(idx * 6 + which, 1), :]
    scale = tab_ref[pl.ds(idx * 6 + which + 1, 1), :]
    h = y * (1.0 + scale) + shift
    h_ref[...] = h
    l_ref[...] = jnp.dot(h, r_ref[...], preferred_element_type=F32, precision=HIGHEST)


def _norm_router(x, g, tab, router, *, nb, ncb, ctx_row, which):
    R, D = x.shape
    kern = functools.partial(_norm_router_kernel, nb=nb, ncb=ncb, ctx_row=ctx_row, which=which)
    return pl.pallas_call(
        kern, grid=(R // ROW_BLOCK,),
        in_specs=[pl.BlockSpec((ROW_BLOCK, D), lambda i: (i, 0)),
                  pl.BlockSpec((1, D), lambda i: (0, 0)),
                  pl.BlockSpec(tab.shape, lambda i: (0, 0)),
                  pl.BlockSpec(router.shape, lambda i: (0, 0))],
        out_specs=[pl.BlockSpec((ROW_BLOCK, D), lambda i: (i, 0)),
                   pl.BlockSpec((ROW_BLOCK, LANES), lambda i: (i, 0))],
        out_shape=[jax.ShapeDtypeStruct((R, D), F32), jax.ShapeDtypeStruct((R, LANES), F32)],
        compiler_params=_cparams(("parallel",)), name="norm_router")(
            x, g.reshape(1, D), tab, router)


def _final_norm_kernel(x_ref, g_ref, o_ref):
    x = x_ref[...]
    o_ref[...] = x * lax.rsqrt(jnp.mean(x * x, axis=-1, keepdims=True) + NORM_EPS) * g_ref[...]


def _final_norm(x, g):
    R, D = x.shape
    return pl.pallas_call(
        _final_norm_kernel, grid=(R // ROW_BLOCK,),
        in_specs=[pl.BlockSpec((ROW_BLOCK, D), lambda i: (i, 0)),
                  pl.BlockSpec((1, D), lambda i: (0, 0))],
        out_specs=pl.BlockSpec((ROW_BLOCK, D), lambda i: (i, 0)),
        out_shape=jax.ShapeDtypeStruct((R, D), F32),
        compiler_params=_cparams(("parallel",)), name="final_norm")(x, g.reshape(1, D))


def _pick_tile(n, options):
    for t in options:
        if n % t == 0:
            return t
    raise ValueError(f"no tile for {n}")


def _norm_mod_rows(x_ref, g_ref, tab_ref, h_ref, *, tm, nb, ncb, ctx_row, which):
    i = pl.program_id(0)
    for c in range(tm // ROW_BLOCK):
        idx = _table_row(i * (tm // ROW_BLOCK) + c, nb, ncb, ctx_row)
        rows = slice(c * ROW_BLOCK, (c + 1) * ROW_BLOCK)
        x = x_ref[rows, :]
        y = x * lax.rsqrt(jnp.mean(x * x, axis=-1, keepdims=True) + NORM_EPS) * g_ref[...]
        shift = tab_ref[pl.ds(idx * 6 + which, 1), :]
        scale = tab_ref[pl.ds(idx * 6 + which + 1, 1), :]
        h_ref[rows, :] = (y * (1.0 + scale) + shift).astype(h_ref.dtype)


def _in_proj_kernel(x_ref, g_ref, tab_ref, w_ref, wab_ref, og_ref, oa_ref, oab_ref, h_ref, *, n_g,
                    **norm_args):
    j = pl.program_id(1)

    @pl.when(j == 0)
    def _():
        _norm_mod_rows(x_ref, g_ref, tab_ref, h_ref, **norm_args)
        oab_ref[...] = jnp.dot(h_ref[...], wab_ref[...], preferred_element_type=F32)

    acc = jnp.dot(h_ref[...], w_ref[...], preferred_element_type=F32)

    @pl.when(j < n_g)
    def _():
        og_ref[...] = acc

    @pl.when(j >= n_g)
    def _():
        oa_ref[...] = acc.astype(oa_ref.dtype)


def _in_proj(x, g, tab, w, w_ab, *, n_cols_g, nb, ncb, ctx_row, which):
    M, D = x.shape
    N = w.shape[1]
    tm = _pick_tile(M, (1024, 512, 256))
    tn = 1024
    assert n_cols_g % tn == 0 and (N - n_cols_g) % tn == 0
    n_g = n_cols_g // tn
    kern = functools.partial(_in_proj_kernel, n_g=n_g, tm=tm, nb=nb, ncb=ncb, ctx_row=ctx_row,
                             which=which)
    return pl.pallas_call(
        kern, grid=(M // tm, N // tn),
        in_specs=[pl.BlockSpec((tm, D), lambda i, j: (i, 0)),
                  pl.BlockSpec((1, D), lambda i, j: (0, 0)),
                  pl.BlockSpec(tab.shape, lambda i, j: (0, 0)),
                  pl.BlockSpec((D, tn), lambda i, j: (0, j)),
                  pl.BlockSpec(w_ab.shape, lambda i, j: (0, 0))],
        out_specs=[pl.BlockSpec((tm, tn), lambda i, j: (i, jnp.minimum(j, n_g - 1))),
                   pl.BlockSpec((tm, tn), lambda i, j: (i, jnp.maximum(j - n_g, 0))),
                   pl.BlockSpec((tm, w_ab.shape[1]), lambda i, j: (i, 0))],
        out_shape=[jax.ShapeDtypeStruct((M, n_cols_g), F32),
                   jax.ShapeDtypeStruct((M, N - n_cols_g), BF16),
                   jax.ShapeDtypeStruct((M, w_ab.shape[1]), F32)],
        scratch_shapes=[pltpu.VMEM((tm, D), BF16)],
        compiler_params=_cparams(("parallel", "arbitrary")), name="in_proj")(
            x, g.reshape(1, D), tab, w, w_ab)


def _norm_swiglu_kernel(x_ref, g_ref, tab_ref, w1_ref, w3_ref, o_ref, h_ref, **norm_args):
    @pl.when(pl.program_id(1) == 0)
    def _():
        _norm_mod_rows(x_ref, g_ref, tab_ref, h_ref, **norm_args)

    a = h_ref[...]
    h1 = jnp.dot(a, w1_ref[...], preferred_element_type=F32)
    h3 = jnp.dot(a, w3_ref[...], preferred_element_type=F32)
    o_ref[...] = (h1 * jax.nn.sigmoid(h1) * h3).astype(o_ref.dtype)


def _norm_swiglu(x, g, tab, w1, w3, *, nb, ncb, ctx_row, which):
    M, D = x.shape
    N = w1.shape[1]
    tm = _pick_tile(M, (1024, 512, 256))
    tn = _pick_tile(N, (512, 256, 128))
    kern = functools.partial(_norm_swiglu_kernel, tm=tm, nb=nb, ncb=ncb, ctx_row=ctx_row, which=which)
    return pl.pallas_call(
        kern, grid=(M // tm, N // tn),
        in_specs=[pl.BlockSpec((tm, D), lambda i, j: (i, 0)),
                  pl.BlockSpec((1, D), lambda i, j: (0, 0)),
                  pl.BlockSpec(tab.shape, lambda i, j: (0, 0)),
                  pl.BlockSpec((D, tn), lambda i, j: (0, j)),
                  pl.BlockSpec((D, tn), lambda i, j: (0, j))],
        out_specs=pl.BlockSpec((tm, tn), lambda i, j: (i, j)),
        out_shape=jax.ShapeDtypeStruct((M, N), BF16),
        scratch_shapes=[pltpu.VMEM((tm, D), BF16)],
        compiler_params=_cparams(("parallel", "arbitrary")), name="norm_swiglu")(
            x, g.reshape(1, D), tab, w1, w3)


def _mm_res_kernel(*refs, n_a, tm, tn, nb, ncb, ctx_row, which):
    a_refs = refs[:n_a]
    w_ref, res_ref, tab_ref, o_ref = refs[n_a:]
    i = pl.program_id(0)
    j = pl.program_id(1)
    a = a_refs[0][...] if n_a == 1 else jnp.concatenate([r[...] for r in a_refs], axis=1)
    acc = jnp.dot(a, w_ref[...], preferred_element_type=F32)
    for c in range(tm // ROW_BLOCK):
        idx = _table_row(i * (tm // ROW_BLOCK) + c, nb, ncb, ctx_row)
        gate = tab_ref[pl.ds(idx * 6 + which, 1), pl.ds(pl.multiple_of(j * tn, tn), tn)]
        rows = slice(c * ROW_BLOCK, (c + 1) * ROW_BLOCK)
        o_ref[rows, :] = res_ref[rows, :] + gate * acc[rows, :]


def _mm_res(a_list, w, res, tab, *, nb, ncb, ctx_row, which, tm, tn,
            in_row_map=None, out_rows=None):
    K, N = w.shape
    M = res.shape[0] if out_rows is None else out_rows
    rm = (lambda i: i) if in_row_map is None else in_row_map
    kern = functools.partial(_mm_res_kernel, n_a=len(a_list), tm=tm, tn=tn, nb=nb, ncb=ncb,
                             ctx_row=ctx_row, which=which)

    def a_spec(a):
        if a.shape[0] == res.shape[0]:
            return pl.BlockSpec((tm, a.shape[1]), lambda i, j: (rm(i), 0))
        assert a.shape[0] == M
        return pl.BlockSpec((tm, a.shape[1]), lambda i, j: (i, 0))

    in_specs = [a_spec(a) for a in a_list]
    in_specs += [pl.BlockSpec((K, tn), lambda i, j: (0, j)),
                 pl.BlockSpec((tm, tn), lambda i, j: (rm(i), j)),
                 pl.BlockSpec(tab.shape, lambda i, j: (0, 0))]
    return pl.pallas_call(
        kern, grid=(M // tm, N // tn), in_specs=in_specs,
        out_specs=pl.BlockSpec((tm, tn), lambda i, j: (i, j)),
        out_shape=jax.ShapeDtypeStruct((M, N), F32),
        compiler_params=_cparams(("parallel", "parallel")), name="mm_res")(*a_list, w, res, tab)


def _ones_column(rows):
    lane = lax.broadcasted_iota(jnp.int32, (rows, LANES), 1)
    return jnp.where(lane == 0, 1.0, 0.0).astype(BF16)


def _rope_lanes(x, cos, sin, rot):
    return x * cos + jnp.dot(x.astype(BF16), rot, preferred_element_type=F32) * sin


def _mean_sq(x):
    w = x.shape[1]
    return jnp.dot((x * x).astype(BF16), jnp.ones((w, w), BF16), preferred_element_type=F32) * (1.0 / w)


def _rope_tables(S, C, rot_dim):
    rows = S // GRID_W
    row = jnp.repeat(jnp.arange(rows), GRID_W).astype(F32)
    col = jnp.tile(jnp.arange(GRID_W), rows).astype(F32)
    half = rot_dim // 2
    inv_freq = ROPE_THETA ** (-jnp.arange(0, half, 2, dtype=F32) / half)
    ang_r = row[:, None] * inv_freq
    ang_c = col[:, None] * inv_freq
    ang = jnp.concatenate([ang_r, ang_r, ang_c, ang_c], axis=-1)
    cos, sin = jnp.cos(ang), jnp.sin(ang)
    quarter = rot_dim // 4
    pad = LANES - rot_dim

    def full(t, ctx_val):
        t = jnp.pad(t, ((0, 0), (0, pad)), constant_values=ctx_val)
        return jnp.concatenate([jnp.full((C, LANES), ctx_val, F32), t], axis=0)

    j = np.arange(rot_dim)
    even = (j // quarter) % 2 == 0
    rot = np.zeros((LANES, LANES), np.float32)
    rot[np.where(even, j + quarter, j - quarter), j] = np.where(even, -1.0, 1.0)
    return full(cos, 1.0), full(sin, 0.0), jnp.asarray(rot, BF16)


def _mla_prep_kernel(p_ref, qg_ref, kvg_ref, wq_ref, wkv_ref, cos_ref, sin_ref, rot_ref,
                     q_ref, k_ref, v_ref, *, scale):
    cos, sin, rot = cos_ref[...], sin_ref[...], rot_ref[...]
    cq = p_ref[:, 0:MLA_Q_RANK].astype(F32)
    qn = cq * lax.rsqrt(jnp.mean(cq * cq, axis=-1, keepdims=True) + NORM_EPS) * qg_ref[...]
    q_raw = jnp.dot(qn.astype(BF16), wq_ref[...], preferred_element_type=F32)
    ckv = p_ref[:, MLA_Q_RANK:MLA_Q_RANK + MLA_KV_RANK].astype(F32)
    kvn = ckv * lax.rsqrt(jnp.mean(ckv * ckv, axis=-1, keepdims=True) + NORM_EPS) * kvg_ref[...]
    kv = jnp.dot(kvn.astype(BF16), wkv_ref[...], preferred_element_type=F32)
    kr = p_ref[:, MLA_Q_RANK + MLA_KV_RANK:MLA_Q_RANK + MLA_KV_RANK + LANES].astype(F32)
    kr = _rope_lanes(kr, cos, sin, rot).astype(BF16)
    for h in range(MLA_HEADS):
        c0 = 2 * LANES * h
        q_ref[:, c0:c0 + LANES] = (q_raw[:, c0:c0 + LANES] * scale).astype(BF16)
        qr = _rope_lanes(q_raw[:, c0 + LANES:c0 + 2 * LANES], cos, sin, rot)
        q_ref[:, c0 + LANES:c0 + 2 * LANES] = (qr * scale).astype(BF16)
        k_ref[:, c0:c0 + LANES] = kv[:, LANES * h:LANES * (h + 1)].astype(BF16)
        k_ref[:, c0 + LANES:c0 + 2 * LANES] = kr
    ones_col = _ones_column(p_ref.shape[0])
    for h in range(MLA_HEADS):
        c0 = 2 * LANES * h
        v_ref[:, c0:c0 + LANES] = kv[:, (MLA_HEADS + h) * LANES:(MLA_HEADS + h + 1) * LANES].astype(BF16)
        v_ref[:, c0 + LANES:c0 + 2 * LANES] = ones_col


def _mla_prep(P, qg, kvg, wq_p, wkv_p, tabs, nbt):
    R = P.shape[0]
    width = MLA_HEADS * 2 * LANES
    kern = functools.partial(_mla_prep_kernel, scale=float((MLA_NOPE + MLA_ROPE) ** -0.5))
    tab_spec = pl.BlockSpec((ROW_BLOCK, LANES), lambda i: (i % nbt, 0))
    const = lambda a: pl.BlockSpec(a.shape, lambda i: (0, 0))
    return pl.pallas_call(
        kern, grid=(R // ROW_BLOCK,),
        in_specs=[pl.BlockSpec((ROW_BLOCK, 1024), lambda i: (i, P_MLA // 1024)),
                  const(qg), const(kvg), const(wq_p), const(wkv_p), tab_spec, tab_spec, const(tabs[2])],
        out_specs=[pl.BlockSpec((ROW_BLOCK, width), lambda i: (i, 0)),
                   pl.BlockSpec((ROW_BLOCK, width), lambda i: (i, 0)),
                   pl.BlockSpec((ROW_BLOCK, width), lambda i: (i, 0))],
        out_shape=[jax.ShapeDtypeStruct((R, width), BF16), jax.ShapeDtypeStruct((R, width), BF16),
                   jax.ShapeDtypeStruct((R, width), BF16)],
        compiler_params=_cparams(("parallel",)), name="mla_prep")(P, qg, kvg, wq_p, wkv_p, *tabs)


def _head_prep_kernel(p_ref, qg_ref, kg_ref, cos_ref, sin_ref, rot_ref, o_ref, *, norm, scale,
                      n_q, n_k):
    cos, sin, rot = cos_ref[...], sin_ref[...], rot_ref[...]
    for h in range(n_q + n_k):
        x = p_ref[:, LANES * h:LANES * (h + 1)].astype(F32)
        if norm:
            g = qg_ref[...] if h < n_q else kg_ref[...]
            x = x * lax.rsqrt(_mean_sq(x) + NORM_EPS) * g
        x = _rope_lanes(x, cos, sin, rot)
        if h < n_q:
            x = x * scale
        c0 = LANES * h if h < n_q else LANES * (h + 2 * n_k)
        o_ref[:, c0:c0 + LANES] = x.astype(BF16)
    v_in = LANES * (n_q + n_k)
    v_out = LANES * n_q
    ones_col = _ones_column(p_ref.shape[0])
    for h in range(n_k):
        o_ref[:, v_out + 2 * LANES * h:v_out + 2 * LANES * h + LANES] = (
            p_ref[:, v_in + LANES * h:v_in + LANES * (h + 1)])
        o_ref[:, v_out + 2 * LANES * h + LANES:v_out + 2 * LANES * (h + 1)] = ones_col


def _head_prep(P, col0, qg, kg, tabs, nbt, *, norm, n_q, n_k):
    R = P.shape[0]
    kern = functools.partial(_head_prep_kernel, norm=norm, scale=float(HEAD_DIM ** -0.5),
                             n_q=n_q, n_k=n_k)
    tab_spec = pl.BlockSpec((ROW_BLOCK, LANES), lambda i: (i % nbt, 0))
    const = lambda a: pl.BlockSpec(a.shape, lambda i: (0, 0))
    return pl.pallas_call(
        kern, grid=(R // ROW_BLOCK,),
        in_specs=[pl.BlockSpec((ROW_BLOCK, 1024), lambda i: (i, col0 // 1024)),
                  const(qg), const(kg), tab_spec, tab_spec, const(tabs[2])],
        out_specs=pl.BlockSpec((ROW_BLOCK, 1024 + n_k * LANES), lambda i: (i, 0)),
        out_shape=jax.ShapeDtypeStruct((R, 1024 + n_k * LANES), BF16),
        compiler_params=_cparams(("parallel",)), name="head_prep")(P, qg, kg, *tabs)


def _attn_kernel(q_ref, k_ref, v_ref, o_ref, *, HS, G, dq, dv, C, T, ctx_tile):
    def run(nk):
        nt = (((1,), (1,)), ((), ()))
        chains = [(hs, g) for hs in range(HS) for g in range(G)]
        s = []
        for hs, g in chains:
            q = q_ref[0, :, (hs * G + g) * dq:(hs * G + g + 1) * dq]
            k = k_ref[0, :nk, hs * dq:(hs + 1) * dq]
            s.append(lax.dot_general(q, k, nt, preferred_element_type=F32).astype(BF16))
        p = [jnp.exp(x - jnp.max(x, axis=-1, keepdims=True)) for x in s]
        for (hs, g), pc in zip(chains, p):
            o = jnp.dot(pc, v_ref[0, :nk, hs * 2 * dv:(hs + 1) * 2 * dv], preferred_element_type=F32)
            c0 = (hs * G + g) * dv
            o_ref[0, :, c0:c0 + dv] = (o[:, :dv] / o[:, dv:dv + 1]).astype(o_ref.dtype)

    if ctx_tile:
        qi = pl.program_id(2)
        pl.when(qi == 0)(lambda: run(C))
        pl.when(qi > 0)(lambda: run(T))
    else:
        run(T)


def _attention(q, k, v, *, Hk, G, dq, dv, q_col0, k_col0, v_col0, C, with_ctx):
    HS = ATTN_CHAINS_PER_STEP // G
    Bn, T, _ = q.shape
    ncb = C // ROW_BLOCK
    assert ncb == 1
    nq = T // ROW_BLOCK if with_ctx else (T - C) // ROW_BLOCK
    qoff = 0 if with_ctx else ncb
    assert Hk % HS == 0
    kern = functools.partial(_attn_kernel, HS=HS, G=G, dq=dq, dv=dv, C=C, T=T, ctx_tile=with_ctx)
    qw, kw, vw = HS * G * dq, HS * dq, HS * 2 * dv
    assert q_col0 % qw == 0 and k_col0 % kw == 0 and v_col0 % vw == 0
    return pl.pallas_call(
        kern, grid=(Bn, Hk // HS, nq),
        in_specs=[pl.BlockSpec((1, ROW_BLOCK, qw), lambda b, h, i: (b, i + qoff, q_col0 // qw + h)),
                  pl.BlockSpec((1, T, kw), lambda b, h, i: (b, 0, k_col0 // kw + h)),
                  pl.BlockSpec((1, T, vw), lambda b, h, i: (b, 0, v_col0 // vw + h))],
        out_specs=pl.BlockSpec((1, ROW_BLOCK, HS * G * dv), lambda b, h, i: (b, i, h)),
        out_shape=jax.ShapeDtypeStruct((Bn, nq * ROW_BLOCK, Hk * G * dv), BF16),
        compiler_params=_cparams(("parallel", "parallel", "parallel")), name="attention")(q, k, v)


def _swa_attention(qkv, sink, *, C, with_ctx):
    Bn, T, _ = qkv.shape
    Hq, Hk, d = SWA_HEADS, SWA_KV_HEADS, HEAD_DIM
    ncb = C // ROW_BLOCK
    assert ncb == 1 and T - C >= ROW_BLOCK + 2 * SWA_WINDOW
    nq = T // ROW_BLOCK if with_ctx else (T - C) // ROW_BLOCK
    qoff = 0 if with_ctx else ncb
    kern = functools.partial(_swa_kernel, C=C, S=T - C, ctx_tile=with_ctx, qoff=qoff)
    k_blk = (Hq * d + Hk * 2 * d) // (Hk * d)
    return pl.pallas_call(
        kern, grid=(Bn, nq),
        in_specs=[pl.BlockSpec(memory_space=pltpu.SMEM),
                  pl.BlockSpec((1, ROW_BLOCK, Hq * d), lambda b, i: (b, i + qoff, 0)),
                  pl.BlockSpec((1, T, Hk * d), lambda b, i: (b, 0, k_blk)),
                  pl.BlockSpec((1, T, Hk * 2 * d), lambda b, i: (b, 0, 1))],
        out_specs=pl.BlockSpec((1, ROW_BLOCK, Hq * d), lambda b, i: (b, i, 0)),
        out_shape=jax.ShapeDtypeStruct((Bn, nq * ROW_BLOCK, Hq * d), BF16),
        compiler_params=_cparams(("parallel", "parallel")), name="swa")(sink, qkv, qkv, qkv)


def _swa_kernel(sink_ref, q_ref, k_ref, v_ref, o_ref, *, C, S, ctx_tile, qoff):
    qi = pl.program_id(1) + qoff
    d = HEAD_DIM
    G = SWA_HEADS // SWA_KV_HEADS
    W = SWA_WINDOW
    WIN = ROW_BLOCK + 2 * W
    nt = (((1,), (1,)), ((), ()))
    heads = range(SWA_HEADS)

    def q_of(h):
        return q_ref[0, :, h * d:(h + 1) * d]

    def finish(parts):
        m16 = []
        for h in heads:
            m = jnp.full((ROW_BLOCK, 1), sink_ref[h], F32)
            for s, _ in parts[h]:
                m = jnp.maximum(m, jnp.max(s, axis=-1, keepdims=True).astype(F32))
            m16.append(m.astype(BF16))
        for h in heads:
            o = None
            for s, v in parts[h]:
                pv = jnp.dot(jnp.exp(s - m16[h]), v, preferred_element_type=F32)
                o = pv if o is None else o + pv
            den = jnp.exp(sink_ref[h] - m16[h].astype(F32)) + o[:, d:d + 1]
            o_ref[0, :, h * d:(h + 1) * d] = (o[:, :d] / den).astype(o_ref.dtype)

    def ctx_scores(h):
        kc = k_ref[0, :C, (h // G) * d:(h // G + 1) * d]
        return (lax.dot_general(q_of(h), kc, nt, preferred_element_type=F32).astype(BF16),
                v_ref[0, :C, (h // G) * 2 * d:(h // G + 1) * 2 * d])

    def run_ctx():
        finish([[ctx_scores(h)] for h in heads])

    def run_lat():
        q0 = (qi - (C // ROW_BLOCK)) * ROW_BLOCK
        ws = jnp.clip(q0 - W, 0, S - WIN)
        start = pl.multiple_of(C + ws, LANES)
        qpos = q0 + lax.broadcasted_iota(jnp.int32, (ROW_BLOCK, WIN), 0)
        kpos = ws + lax.broadcasted_iota(jnp.int32, (ROW_BLOCK, WIN), 1)
        valid = jnp.abs(kpos - qpos) <= W
        parts = []
        for h in heads:
            kw = k_ref[0, pl.ds(start, WIN), (h // G) * d:(h // G + 1) * d]
            vw = v_ref[0, pl.ds(start, WIN), (h // G) * 2 * d:(h // G + 1) * 2 * d]
            s_loc = jnp.where(valid, lax.dot_general(q_of(h), kw, nt, preferred_element_type=F32),
                              NEG_INF).astype(BF16)
            parts.append([(s_loc, vw), ctx_scores(h)])
        finish(parts)

    if ctx_tile:
        pl.when(qi == 0)(run_ctx)
        pl.when(qi > 0)(run_lat)
    else:
        run_lat()


def _gdn_prep_kernel(x_ref, w_ref, o_ref, pad_ref, *, C, T):
    j = pl.program_id(1)
    g = SUBLANES
    half = (GDN_CONV - 1) // 2
    assert half <= g
    zeros = jnp.zeros((g, LANES), F32)
    segments = ((0, C, g), (C, T - C, 2 * g + C))
    pad_ref[0:g, :] = zeros
    for lo, n, base in segments:
        pad_ref[base:base + n, :] = x_ref[0, lo:lo + n, :]
        pad_ref[base + n:base + n + g, :] = zeros
    nh = GDN_HEADS
    for lo, n, base in segments:
        acc = None
        for tap in range(GDN_CONV):
            term = pad_ref[base + tap - half:base + tap - half + n, :] * w_ref[tap:tap + 1, :]
            acc = term if acc is None else acc + term
        y = acc * jax.nn.sigmoid(acc)
        inv = lax.rsqrt(_mean_sq(y) * float(LANES) + 1e-6)
        f = jnp.where(j < 2 * nh, inv, 1.0) * jnp.where(j < nh, float(HEAD_DIM ** -0.5), 1.0)
        o_ref[0, lo:lo + n, :] = y * f


def _gdn_prep(P3, conv_w8, C):
    Bn, T, _ = P3.shape
    nblk = 3 * GDN_HEADS
    kern = functools.partial(_gdn_prep_kernel, C=C, T=T)
    return pl.pallas_call(
        kern, grid=(Bn, nblk),
        in_specs=[pl.BlockSpec((1, T, LANES), lambda b, j: (b, 0, P_GDN // LANES + j)),
                  pl.BlockSpec((8, LANES), lambda b, j: (0, j))],
        out_specs=pl.BlockSpec((1, T, LANES), lambda b, j: (b, 0, j)),
        out_shape=jax.ShapeDtypeStruct((Bn, T, nblk * LANES), F32),
        scratch_shapes=[pltpu.VMEM((T + 3 * SUBLANES, LANES), F32)],
        compiler_params=_cparams(("parallel", "parallel")), name="gdn_prep")(P3, conv_w8)


_GDN_INV_BASE = 8
GDN_BATCH_PER_STEP = 4


def _dot16(a, b):
    return jnp.dot(a, b, preferred_element_type=F32)


def _dot16_nt(a, b):
    return lax.dot_general(a, b, (((1,), (1,)), ((), ())), preferred_element_type=F32)


def _dot16_tn(a, b):
    return lax.dot_general(a, b, (((0,), (0,)), ((), ())), preferred_element_type=F32)


def _mask_dot(mask16, x):
    hi = x.astype(BF16)
    r1 = x - hi.astype(F32)
    mid = r1.astype(BF16)
    lo = (r1 - mid.astype(F32)).astype(BF16)
    w = x.shape[1]
    y = jnp.dot(mask16, jnp.concatenate([hi, mid, lo], axis=1), preferred_element_type=F32)
    return y[:, :w] + y[:, w:2 * w] + y[:, 2 * w:]


def _gdn_chunk_kernel(qf_ref, qb_ref, abf_ref, abb_ref, par_ref, of_ref, ob_ref, s_ref):
    Cc = GDN_CHUNK
    H = GDN_HEADS
    d = HEAD_DIM

    @pl.when(pl.program_id(1) == 0)
    def _():
        s_ref[...] = jnp.zeros_like(s_ref)

    ri = lax.broadcasted_iota(jnp.int32, (Cc, Cc), 0)
    ci = lax.broadcasted_iota(jnp.int32, (Cc, Cc), 1)
    eye = (ri == ci).astype(F32)
    neg_a = par_ref[0:1, :]
    dt_b = par_ref[1:2, :]
    base = _GDN_INV_BASE
    diag_blk = (ri // base) == (ci // base)
    off_blks = []
    s = base
    while s < Cc:
        off_blks.append(jnp.logical_and((ri // (2 * s)) == (ci // (2 * s)), (ri // s) != (ci // s)))
        s *= 2

    dirs = []
    for bb, direction in [(bb, direction) for bb in range(qf_ref.shape[0]) for direction in range(2)]:
        ab = (abf_ref if direction == 0 else abb_ref)[bb]
        if direction == 0:
            incl, incl_t, strict, last = ri >= ci, ri <= ci, ri > ci, Cc - 1
        else:
            incl, incl_t, strict, last = ri <= ci, ri >= ci, ri < ci, 0
        z = ab + dt_b
        sp = jnp.maximum(z, 0.0) + jnp.log1p(jnp.exp(-jnp.abs(z)))
        gl_all = neg_a * sp
        gc_all = _mask_dot(incl.astype(BF16), gl_all)
        g_tot = gc_all[last:last + 1, :]
        gc_sq = jnp.concatenate([gc_all, jnp.zeros((LANES - Cc, LANES), F32)], axis=0)
        dirs.append(dict(
            bb=bb, direction=direction,
            x_ref=qf_ref if direction == 0 else qb_ref, o_ref=of_ref if direction == 0 else ob_ref,
            incl=incl, strict=strict, gc_all=gc_all,
            gct_all=gc_sq.T[:, :Cc],
            beta_all=jax.nn.sigmoid(ab),
            eg_all=jnp.exp(gc_all), ekg_all=jnp.exp(g_tot - gc_all), egt_all=jnp.exp(g_tot)))

    units = [(dd, dd["direction"] * H + h, h) for dd in dirs for h in range(H)]
    U = range(len(units))

    k16, kb, decay, a_mat = [], [], [], []
    for dd, lane, h in units:
        k = dd["x_ref"][dd["bb"],:, (H + h) * d:(H + h + 1) * d]
        beta = dd["beta_all"][:, 2 * H + lane:2 * H + lane + 1]
        diff = dd["gc_all"][:, lane:lane + 1] - dd["gct_all"][lane:lane + 1, :]
        decay.append(jnp.where(dd["incl"], jnp.exp(jnp.where(dd["incl"], diff, 0.0)), 0.0))
        kb.append(k * beta)
        k16.append(k.astype(BF16))
    for i in U:
        a_mat.append(jnp.where(units[i][0]["strict"],
                               _dot16_nt(kb[i].astype(BF16), k16[i]) * decay[i], 0.0))

    npow = [jnp.where(diag_blk, a, 0.0).astype(BF16) for a in a_mat]
    t = [eye - jnp.where(diag_blk, a, 0.0) for a in a_mat]
    sq = 2 * base
    while sq > 4:
        npow = [_dot16(n, n).astype(BF16) for n in npow]
        t = [t[i] + _dot16(t[i].astype(BF16), npow[i]) for i in U]
        sq //= 2
    for off_blk in off_blks:
        t16 = [x.astype(BF16) for x in t]
        xo = [_dot16(t16[i], jnp.where(off_blk, a_mat[i], 0.0).astype(BF16)).astype(BF16) for i in U]
        t = [t[i] - _dot16(xo[i], t16[i]) for i in U]

    uw, intra, qg = [], [], []
    for i in U:
        dd, lane, h = units[i]
        q = dd["x_ref"][dd["bb"],:, h * d:(h + 1) * d]
        v = dd["x_ref"][dd["bb"],:, (2 * H + h) * d:(2 * H + h + 1) * d]
        beta = dd["beta_all"][:, 2 * H + lane:2 * H + lane + 1]
        eg = dd["eg_all"][:, lane:lane + 1]
        rhs = jnp.concatenate([v * beta, kb[i] * eg], axis=1).astype(BF16)
        uw.append(_dot16(t[i].astype(BF16), rhs))
        intra.append(jnp.where(dd["incl"], _dot16_nt(q.astype(BF16), k16[i]) * decay[i], 0.0)
                     .astype(BF16))
        qg.append((q * eg).astype(BF16))
    state = [s_ref[units[i][0]["bb"] * 2 * H + units[i][1]] for i in U]
    s16 = [x.astype(BF16) for x in state]
    v_new = [uw[i][:, :d] - _dot16(uw[i][:, d:].astype(BF16), s16[i]) for i in U]
    vn16 = [x.astype(BF16) for x in v_new]
    for i in U:
        dd, lane, h = units[i]
        dd["o_ref"][dd["bb"], :, h * d:(h + 1) * d] = (_dot16(qg[i], s16[i])
                                                       + _dot16(intra[i], vn16[i]))
    for i in U:
        dd, lane, h = units[i]
        k = dd["x_ref"][dd["bb"],:, (H + h) * d:(H + h + 1) * d]
        kg = (k * dd["ekg_all"][:, lane:lane + 1]).astype(BF16)
        s_ref[dd["bb"] * 2 * H + lane] = (state[i] * dd["egt_all"][:, lane:lane + 1]
                                          + _dot16_tn(kg, vn16[i]))


def _gdn_chunks(qkv, P3, par, C):
    Bn, T, _ = qkv.shape
    Cc = GDN_CHUNK
    nc = T // Cc
    ncc = C // Cc
    H = GDN_HEADS

    def bwd_chunk(s):
        return jnp.where(s < ncc, ncc - 1 - s, nc + ncc - 1 - s)

    ab_blk = 0
    nbb = _pick_tile(Bn, (GDN_BATCH_PER_STEP, 1))
    return pl.pallas_call(
        _gdn_chunk_kernel, grid=(Bn // nbb, nc),
        in_specs=[pl.BlockSpec((nbb, Cc, 3 * H * HEAD_DIM), lambda b, s: (b, s, 0)),
                  pl.BlockSpec((nbb, Cc, 3 * H * HEAD_DIM), lambda b, s: (b, bwd_chunk(s), 0)),
                  pl.BlockSpec((nbb, Cc, LANES), lambda b, s: (b, s, ab_blk)),
                  pl.BlockSpec((nbb, Cc, LANES), lambda b, s: (b, bwd_chunk(s), ab_blk)),
                  pl.BlockSpec((8, LANES), lambda b, s: (0, 0))],
        out_specs=[pl.BlockSpec((nbb, Cc, H * HEAD_DIM), lambda b, s: (b, s, 0)),
                   pl.BlockSpec((nbb, Cc, H * HEAD_DIM), lambda b, s: (b, bwd_chunk(s), 0))],
        out_shape=[jax.ShapeDtypeStruct((Bn, T, H * HEAD_DIM), F32),
                   jax.ShapeDtypeStruct((Bn, T, H * HEAD_DIM), F32)],
        scratch_shapes=[pltpu.VMEM((nbb * 2 * H, HEAD_DIM, HEAD_DIM), F32)],
        compiler_params=_cparams(("parallel", "arbitrary")), name="gdn_chunks")(
            qkv, qkv, P3, P3, par)


def _gdn_post_kernel(of_ref, ob_ref, z_ref, g_ref, o_ref):
    d = HEAD_DIM
    for h in range(GDN_HEADS):
        o = of_ref[:, h * d:(h + 1) * d] + ob_ref[:, h * d:(h + 1) * d]
        y = o * lax.rsqrt(_mean_sq(o) + NORM_EPS) * g_ref[...]
        z = z_ref[:, h * d:(h + 1) * d]
        o_ref[:, h * d:(h + 1) * d] = (y * (z * jax.nn.sigmoid(z))).astype(o_ref.dtype)


def _gdn_post(of, ob, P, g):
    R, W = of.shape
    zblk = (P_GDN + 3 * W) // W
    return pl.pallas_call(
        _gdn_post_kernel, grid=(R // ROW_BLOCK,),
        in_specs=[pl.BlockSpec((ROW_BLOCK, W), lambda i: (i, 0)),
                  pl.BlockSpec((ROW_BLOCK, W), lambda i: (i, 0)),
                  pl.BlockSpec((ROW_BLOCK, W), lambda i: (i, zblk)),
                  pl.BlockSpec((1, HEAD_DIM), lambda i: (0, 0))],
        out_specs=pl.BlockSpec((ROW_BLOCK, W), lambda i: (i, 0)),
        out_shape=jax.ShapeDtypeStruct((R, W), BF16),
        compiler_params=_cparams(("parallel",)), name="gdn_post")(of, ob, P, g)


def _gather_rows_kernel(tok_ref, tok_next_ref, h_hbm, o_ref, buf, sem):
    i = pl.program_id(0)
    nt = pl.num_programs(0)
    n = buf.shape[1]
    slot = i % 2

    def issue(t_ref, s):
        def body(r, c):
            pltpu.make_async_copy(h_hbm.at[pl.ds(t_ref[0, 0, r], 1), :],
                                  buf.at[s, pl.ds(r, 1), :], sem.at[s]).start()
            return c
        lax.fori_loop(0, n, body, 0, unroll=8)

    @pl.when(i == 0)
    def _():
        issue(tok_ref, 0)

    @pl.when(i + 1 < nt)
    def _():
        issue(tok_next_ref, 1 - slot)

    pltpu.make_async_copy(h_hbm.at[pl.ds(0, n), :], buf.at[slot], sem.at[slot]).wait()
    o_ref[...] = buf[slot].astype(o_ref.dtype)


def _gather_rows(h, src_tok):
    D = h.shape[1]
    n = src_tok.shape[0]
    nt = n // ROW_BLOCK
    return pl.pallas_call(
        _gather_rows_kernel, grid=(nt,),
        in_specs=[pl.BlockSpec((1, 1, ROW_BLOCK), lambda i: (i, 0, 0), memory_space=pltpu.SMEM),
                  pl.BlockSpec((1, 1, ROW_BLOCK), lambda i: (jnp.minimum(i + 1, nt - 1), 0, 0),
                               memory_space=pltpu.SMEM),
                  pl.BlockSpec(memory_space=pl.ANY)],
        out_specs=pl.BlockSpec((ROW_BLOCK, D), lambda i: (i, 0)),
        out_shape=jax.ShapeDtypeStruct((n, D), BF16),
        scratch_shapes=[pltpu.VMEM((2, ROW_BLOCK, D), F32), pltpu.SemaphoreType.DMA((2,))],
        compiler_params=_cparams(("arbitrary",)), name="moe_gather")(
            src_tok.reshape(nt, 1, ROW_BLOCK), src_tok.reshape(nt, 1, ROW_BLOCK), h)


def _gmm_kernel(te_ref, tv_ref, x_ref, w1_ref, w3_ref, w2_ref, o_ref):
    i = pl.program_id(0)
    f = pl.program_id(1)

    @pl.when(f == 0)
    def _():
        o_ref[...] = jnp.zeros_like(o_ref)

    @pl.when(tv_ref[i] == 1)
    def _():
        x = x_ref[...]
        h1 = jnp.dot(x, w1_ref[0], preferred_element_type=F32)
        h3 = jnp.dot(x, w3_ref[0], preferred_element_type=F32)
        hid = (h1 * jax.nn.sigmoid(h1) * h3).astype(BF16)
        o_ref[...] += jnp.dot(hid, w2_ref[0], preferred_element_type=F32)


def _gmm(xs, w1, w3, w2, tile_e, tile_valid):
    n, D = xs.shape
    E, _, F = w1.shape
    tf = 512
    nt = n // MOE_TILE
    grid_spec = pltpu.PrefetchScalarGridSpec(
        num_scalar_prefetch=2, grid=(nt, F // tf),
        in_specs=[pl.BlockSpec((MOE_TILE, D), lambda i, f, te, tv: (i, 0)),
                  pl.BlockSpec((1, D, tf), lambda i, f, te, tv: (te[i], 0, f)),
                  pl.BlockSpec((1, D, tf), lambda i, f, te, tv: (te[i], 0, f)),
                  pl.BlockSpec((1, tf, D), lambda i, f, te, tv: (te[i], f, 0))],
        out_specs=pl.BlockSpec((MOE_TILE, D), lambda i, f, te, tv: (i, 0)))
    return pl.pallas_call(
        _gmm_kernel, grid_spec=grid_spec,
        out_shape=jax.ShapeDtypeStruct((n, D), F32),
        compiler_params=_cparams(("parallel", "arbitrary")), name="moe_gmm")(
            tile_e, tile_valid, xs, w1, w3, w2)


def _combine_kernel(pos_ref, pos_next_ref, ys_hbm, x_ref, eg_ref, tab_ref, fg_ref, o_ref, buf, sem,
                    *, nb, ncb, ctx_row, which, final_norm):
    i = pl.program_id(0)
    nt = pl.num_programs(0)
    n = buf.shape[2]
    slot = i % 2
    idx = _table_row(i, nb, ncb, ctx_row)

    def issue(p_ref, s):
        def body(r, c):
            for k in range(TOP_K):
                pltpu.make_async_copy(ys_hbm.at[pl.ds(p_ref[0, k, r], 1), :],
                                      buf.at[s, k, pl.ds(r, 1), :], sem.at[s]).start()
            return c
        lax.fori_loop(0, n, body, 0, unroll=4)

    @pl.when(i == 0)
    def _():
        issue(pos_ref, 0)

    @pl.when(i + 1 < nt)
    def _():
        issue(pos_next_ref, 1 - slot)

    for k in range(TOP_K):
        pltpu.make_async_copy(ys_hbm.at[pl.ds(0, n), :], buf.at[slot, k], sem.at[slot]).wait()
    gate = tab_ref[pl.ds(idx * 6 + which, 1), :]
    eg = eg_ref[...]
    y = eg[:, 0:1] * buf[slot, 0]
    for k in range(1, TOP_K):
        y = y + eg[:, k:k + 1] * buf[slot, k]
    out = x_ref[...] + gate * y
    if final_norm:
        out = out * lax.rsqrt(jnp.mean(out * out, axis=-1, keepdims=True) + NORM_EPS) * fg_ref[...]
    o_ref[...] = out


def _combine(ys, pos, egate, x, tab, final_g, *, nb, ncb, ctx_row, which, final_norm):
    R, D = x.shape
    nt = R // ROW_BLOCK
    pos3 = pos.reshape(nt, ROW_BLOCK, TOP_K).transpose(0, 2, 1)
    kern = functools.partial(_combine_kernel, nb=nb, ncb=ncb, ctx_row=ctx_row, which=which,
                             final_norm=final_norm)
    return pl.pallas_call(
        kern, grid=(nt,),
        in_specs=[pl.BlockSpec((1, TOP_K, ROW_BLOCK), lambda i: (i, 0, 0), memory_space=pltpu.SMEM),
                  pl.BlockSpec((1, TOP_K, ROW_BLOCK), lambda i: (jnp.minimum(i + 1, nt - 1), 0, 0),
                               memory_space=pltpu.SMEM),
                  pl.BlockSpec(memory_space=pl.ANY),
                  pl.BlockSpec((ROW_BLOCK, D), lambda i: (i, 0)),
                  pl.BlockSpec((ROW_BLOCK, LANES), lambda i: (i, 0)),
                  pl.BlockSpec(tab.shape, lambda i: (0, 0)),
                  pl.BlockSpec((1, D), lambda i: (0, 0))],
        out_specs=pl.BlockSpec((ROW_BLOCK, D), lambda i: (i, 0)),
        out_shape=jax.ShapeDtypeStruct((R, D), F32),
        scratch_shapes=[pltpu.VMEM((2, TOP_K, ROW_BLOCK, D), F32), pltpu.SemaphoreType.DMA((2,))],
        compiler_params=_cparams(("arbitrary",)), name="moe_combine")(pos3, pos3, ys, x, egate, tab, final_g.reshape(1, D))


def _route(logits):
    R = logits.shape[0]
    E = N_EXPERTS
    top_v, top_i = lax.top_k(logits, TOP_K)
    gate = jax.nn.softmax(top_v, axis=-1)
    onehot = jnp.sum((top_i[:, :, None] == jnp.arange(E)[None, None, :]).astype(jnp.int32), axis=1)
    before = jnp.cumsum(onehot, axis=0) - onehot
    sizes = jnp.sum(onehot, axis=0)
    padded = ((sizes + MOE_TILE - 1) // MOE_TILE) * MOE_TILE
    ends = jnp.cumsum(padded)
    off = ends - padded
    pos = off[top_i] + jnp.take_along_axis(before, top_i, axis=1)
    nt = (TOP_K * R) // MOE_TILE + E
    n = nt * MOE_TILE
    flat = pos.reshape(-1)
    src_tok = jnp.zeros((n,), jnp.int32).at[flat].set(jnp.repeat(jnp.arange(R, dtype=jnp.int32), TOP_K))
    egate = jnp.pad(gate, ((0, 0), (0, LANES - TOP_K)))
    tile_start = jnp.arange(nt, dtype=jnp.int32) * MOE_TILE
    tile_valid = (tile_start < ends[-1]).astype(jnp.int32)
    tile_e = jnp.searchsorted(ends, tile_start, side="right").astype(jnp.int32)
    last_e = jnp.max(jnp.where(tile_valid == 1, tile_e, 0))
    tile_e = jnp.where(tile_valid == 1, tile_e, last_e)
    return pos.astype(jnp.int32), src_tok, egate, tile_e, tile_valid


def _prep_w_in(w):
    D = w.shape[0]
    z = lambda n: jnp.zeros((D, n), w.dtype)
    w_g = w[:, 1728:3776]
    w_a = jnp.concatenate([w[:, 704:1728], w[:, 3792:4816], w[:, 0:704], z(PA_WIDTH - 2048 - 704)],
                          axis=1)
    w_ab = jnp.concatenate([w[:, 3776:3792], z(LANES - 16)], axis=1)
    assert w_g.shape[1] == PG_WIDTH and w_a.shape[1] == PA_WIDTH
    return jnp.concatenate([w_g, w_a], axis=1).astype(BF16), w_ab.astype(BF16)


def _prep_w_uq(w):
    r = w.shape[0]
    w = w.reshape(r, MLA_HEADS, MLA_NOPE + MLA_ROPE)
    w = jnp.pad(w, ((0, 0), (0, 0), (0, 2 * LANES - (MLA_NOPE + MLA_ROPE))))
    return w.reshape(r, MLA_HEADS * 2 * LANES).astype(BF16)


def _prep_w_ukv(w):
    r = w.shape[0]
    w = w.reshape(r, MLA_HEADS, MLA_NOPE + MLA_V)
    return jnp.concatenate([w[:, :, :MLA_NOPE].reshape(r, -1), w[:, :, MLA_NOPE:].reshape(r, -1)],
                           axis=1).astype(BF16)


def kernel(x, c, ctx, c_ctx, norm1_g, norm2_g, w_mod, b_mod, w_in, mla_q_norm_g, mla_kv_norm_g,
           mla_w_uq, mla_w_ukv, gqa_q_norm_g, gqa_k_norm_g, gdn_conv_w, gdn_a_log, gdn_dt_bias,
           gdn_norm_g, swa_sink, w_out, ffn_w1, ffn_w3, ffn_w2, moe_router, moe_w1, moe_w3, moe_w2,
           final_norm_g):
    Bn, S, D = x.shape
    C = ctx.shape[1]
    T = C + S
    R = Bn * T
    depth = w_in.shape[0]
    nbt = T // ROW_BLOCK
    ncb = C // ROW_BLOCK
    nbl = S // ROW_BLOCK

    m_rows = ((Bn + 1 + 7) // 8) * 8
    cmat = jnp.zeros((m_rows, D), F32).at[:Bn].set(c).at[Bn].set(c_ctx)
    mods = _mods(cmat, w_mod, b_mod).reshape(depth, m_rows * 6, D)

    rope_mla = _rope_tables(S, C, MLA_ROPE)
    rope_head = _rope_tables(S, C, HEAD_DIM)

    xc = jnp.concatenate([ctx, x], axis=1).reshape(R, D)
    lay = dict(nb=nbt, ncb=ncb, ctx_row=Bn)

    for layer in range(depth):
        with_ctx = layer < depth - 1
        tab = mods[layer]
        Pg, Pa, Pab = _in_proj(xc, norm1_g[layer], tab, *_prep_w_in(w_in[layer]), n_cols_g=PG_WIDTH,
                          which=0, **lay)
        Pg3 = Pg.reshape(Bn, T, PG_WIDTH)

        qa, ka, va = _mla_prep(Pa, mla_q_norm_g[layer].reshape(1, -1), mla_kv_norm_g[layer].reshape(1, -1),
                               _prep_w_uq(mla_w_uq[layer]), _prep_w_ukv(mla_w_ukv[layer]), rope_mla, nbt)
        ya = _attention(qa.reshape(Bn, T, -1), ka.reshape(Bn, T, -1), va.reshape(Bn, T, -1),
                        Hk=MLA_HEADS, G=1, dq=2 * LANES, dv=MLA_V, q_col0=0, k_col0=0, v_col0=0,
                        C=C, with_ctx=with_ctx)
        qkv_b = _head_prep(Pa, P_GQA, gqa_q_norm_g[layer].reshape(1, -1), gqa_k_norm_g[layer].reshape(1, -1),
                           rope_head, nbt, norm=True, n_q=GQA_HEADS, n_k=GQA_KV_HEADS).reshape(Bn, T, -1)
        Gb = GQA_HEADS // GQA_KV_HEADS
        yb = _attention(qkv_b, qkv_b, qkv_b, Hk=GQA_KV_HEADS, G=Gb, dq=HEAD_DIM, dv=HEAD_DIM,
                        q_col0=0, v_col0=GQA_HEADS * HEAD_DIM,
                        k_col0=(GQA_HEADS + 2 * GQA_KV_HEADS) * HEAD_DIM, C=C, with_ctx=with_ctx)
        conv_w8 = jnp.pad(gdn_conv_w[layer], ((0, 8 - GDN_CONV), (0, 0)))
        qkv_c = _gdn_prep(Pg3, conv_w8, C)
        par = jnp.zeros((8, LANES), F32)
        par = par.at[0, :2 * GDN_HEADS].set(-jnp.exp(gdn_a_log[layer].reshape(-1)))
        par = par.at[1, :2 * GDN_HEADS].set(gdn_dt_bias[layer].reshape(-1))
        of, ob = _gdn_chunks(qkv_c, Pab.reshape(Bn, T, LANES), par, C)
        yc = _gdn_post(of.reshape(R, -1), ob.reshape(R, -1), Pg, gdn_norm_g[layer].reshape(1, -1))
        qkv_d = _head_prep(Pa, P_SWA, gqa_q_norm_g[layer].reshape(1, -1), gqa_k_norm_g[layer].reshape(1, -1),
                           rope_head, nbt, norm=False, n_q=SWA_HEADS, n_k=SWA_KV_HEADS).reshape(Bn, T, -1)
        yd = _swa_attention(qkv_d, swa_sink[layer], C=C, with_ctx=with_ctx)

        mixed = [ya.reshape(-1, GROUP_WIDTH), yb.reshape(-1, GROUP_WIDTH), yc,
                 yd.reshape(-1, GROUP_WIDTH)]
        w_o = w_out[layer].astype(BF16)
        if with_ctx:
            xc = _mm_res(mixed, w_o, xc, tab, which=2, tm=_pick_tile(R, (1024, 512, 256)),
                         tn=_pick_tile(D, (1024, 512)), **lay)
        else:
            lay = dict(nb=nbl, ncb=0, ctx_row=Bn)
            xc = _mm_res(mixed, w_o, xc, tab, which=2, tm=ROW_BLOCK, tn=D, **lay,
                         in_row_map=lambda i: (i // nbl) * nbt + ncb + i % nbl, out_rows=Bn * S)

        i2 = layer // 2
        if layer % 2 == 0:
            hid = _norm_swiglu(xc, norm2_g[layer], tab, ffn_w1[i2].astype(BF16),
                               ffn_w3[i2].astype(BF16), which=3, **lay)
            xc = _mm_res([hid], ffn_w2[i2].astype(BF16), xc, tab, which=5, tm=_pick_tile(xc.shape[0], (1024, 512)), tn=512, **lay)
        else:
            router_p = jnp.pad(moe_router[i2], ((0, 0), (0, LANES - N_EXPERTS)))
            h2, logits = _norm_router(xc, norm2_g[layer], tab, router_p, which=3, **lay)
            pos, src_tok, egate, tile_e, tile_valid = _route(logits[:, :N_EXPERTS])
            xs = _gather_rows(h2, src_tok)
            ys = _gmm(xs, moe_w1[i2].astype(BF16), moe_w3[i2].astype(BF16), moe_w2[i2].astype(BF16),
                      tile_e, tile_valid)
            xc = _combine(ys, pos, egate, xc, tab, final_norm_g, which=5,
                          final_norm=layer == depth - 1, **lay)

    if depth % 2 == 1:
        xc = _final_norm(xc, final_norm_g)
    return xc.reshape(Bn, S, D)
```
